```python
import jax, jax.numpy as jnp
from jax import lax
import numpy as np

D_MODEL = 1024
BATCH = 8
SEQ = 2048
DEPTH = 2
DEC_BATCH = 128
DEC_SEQ = 4
PAST_LEN = 16384
PAGE_SIZE = 128

N_MIXERS = 2
N_CONV_LAYERS = (DEPTH + N_MIXERS - 1) // N_MIXERS
N_RET_LAYERS = DEPTH // N_MIXERS
CONV_DIM = D_MODEL
CONV_WIDTH = 3
RET_HEADS = 4
RET_DK = D_MODEL // RET_HEADS
RET_DV = 2 * RET_DK
RET_CHUNK = 128
ROPE_BASE = 10000.0
N_MEM = 256
XA_HEADS = 4
XA_DIM = D_MODEL // XA_HEADS
XA_WIDTH = XA_HEADS * XA_DIM
N_GROUPS = 4
EXPERTS_PER_GROUP = 8
N_EXPERTS = N_GROUPS * EXPERTS_PER_GROUP
TOP_K_IN_GROUP = 2
EXPERT_FF = 256
EPS = 1e-6

CONV_MIX_W = 3 * CONV_DIM
CONV_IN = CONV_MIX_W + XA_WIDTH
CONV_OUT_IN = CONV_DIM + XA_WIDTH
RET_MIX_W = 2 * RET_HEADS * RET_DK + 2 * RET_HEADS * RET_DV
RET_IN = RET_MIX_W + XA_WIDTH
RET_OUT_IN = RET_HEADS * RET_DV + XA_WIDTH

kernel_name = 'hybrid_conv_retention_hmoe_step'


def rmsnorm(x, g):
    xf = x.astype(jnp.float32)
    y = xf * lax.rsqrt(jnp.mean(xf * xf, axis=-1, keepdims=True) + EPS) * g.astype(jnp.float32)
    return y.astype(x.dtype)


def rotary(x, pos):
    half = x.shape[-1] // 2
    inv = ROPE_BASE ** (-jnp.arange(half, dtype=jnp.float32) / half)
    ang = pos.astype(jnp.float32)[:, None] * inv[None, :]
    cos = jnp.cos(ang)[None, :, None, :]
    sin = jnp.sin(ang)[None, :, None, :]
    x1, x2 = x[..., :half], x[..., half:]
    return jnp.concatenate([x1 * cos - x2 * sin, x1 * sin + x2 * cos], axis=-1)


def causal_conv(u, state, w):
    T = u.shape[1]
    full = jnp.concatenate([state.astype(u.dtype), u], axis=1)
    y = sum(w[k].astype(u.dtype) * full[:, k:k + T] for k in range(CONV_WIDTH))
    return y, full[:, T:]


def conv_mixer(z, state, w):
    b_gate = z[..., :CONV_DIM]
    c_gate = z[..., CONV_DIM:2 * CONV_DIM]
    v = z[..., 2 * CONV_DIM:3 * CONV_DIM]
    y, new_state = causal_conv(c_gate * v, state, w)
    return b_gate * y, new_state


def retention_scan(q, k, v, s0, log_g):
    B, T = q.shape[0], q.shape[1]
    L = RET_CHUNK if T % RET_CHUNK == 0 else T
    n = T // L
    idx = jnp.arange(L)
    diff = idx[:, None] - idx[None, :]
    decay = jnp.where(diff[None] >= 0,
                      jnp.exp(jnp.maximum(diff, 0)[None].astype(jnp.float32) * log_g[:, None, None]),
                      0.0)
    inner = jnp.exp((idx + 1).astype(jnp.float32)[:, None] * log_g[None, :])
    tail = jnp.exp((L - 1 - idx).astype(jnp.float32)[:, None] * log_g[None, :])
    full = jnp.exp(L * log_g)

    def to_chunks(a):
        return jnp.moveaxis(a.reshape((B, n, L) + a.shape[2:]), 1, 0)

    def step(s, qkv):
        qc, kc, vc = qkv
        scores = jnp.einsum('blhd,bmhd->bhlm', qc, kc) * decay[None]
        o_intra = jnp.einsum('bhlm,bmhe->blhe', scores, vc)
        o_cross = jnp.einsum('blhd,bhde->blhe', qc, s) * inner[None, :, :, None]
        s_new = full[None, :, None, None] * s + jnp.einsum(
            'blhd,blhe->bhde', kc * tail[None, :, :, None], vc)
        return s_new, o_intra + o_cross

    s_fin, o = lax.scan(step, s0, (to_chunks(q), to_chunks(k), to_chunks(v)))
    o = jnp.moveaxis(o, 0, 1).reshape(B, T, RET_HEADS, RET_DV)
    return o, s_fin


def retention_mixer(z, state, pos, gn_g):
    B, T = z.shape[0], z.shape[1]
    qd = RET_HEADS * RET_DK
    vd = RET_HEADS * RET_DV
    q = z[..., :qd].reshape(B, T, RET_HEADS, RET_DK).astype(jnp.float32)
    k = z[..., qd:2 * qd].reshape(B, T, RET_HEADS, RET_DK).astype(jnp.float32)
    v = z[..., 2 * qd:2 * qd + vd].reshape(B, T, RET_HEADS, RET_DV).astype(jnp.float32)
    g = z[..., 2 * qd + vd:2 * qd + 2 * vd].astype(jnp.float32)
    q = rotary(q, pos)
    k = rotary(k, pos) * (RET_DK ** -0.5)
    log_g = jnp.log1p(-jnp.exp2(-5.0 - jnp.arange(RET_HEADS, dtype=jnp.float32)))
    o, s_new = retention_scan(q, k, v, state.astype(jnp.float32), log_g)
    mu = jnp.mean(o, axis=-1, keepdims=True)
    var = jnp.mean(jnp.square(o - mu), axis=-1, keepdims=True)
    o = ((o - mu) * lax.rsqrt(var + EPS)).reshape(B, T, vd) * gn_g.astype(jnp.float32)
    return (jax.nn.silu(g) * o).astype(z.dtype), s_new


def cross_attn(q, mem_k, mem_v):
    B, T = q.shape[0], q.shape[1]
    s = jnp.einsum('bthd,bmhd->bhtm', q.astype(jnp.float32), mem_k.astype(jnp.float32)) * (XA_DIM ** -0.5)
    p = jax.nn.softmax(s, axis=-1)
    o = jnp.einsum('bhtm,bmhd->bthd', p, mem_v.astype(jnp.float32))
    return o.reshape(B, T, XA_WIDTH).astype(q.dtype)


def hier_moe(h, w_group, w_router, w_gate_up, w_down):
    def one_seq(hs):
        hf = hs.astype(jnp.float32)
        lg = hf @ w_group.astype(jnp.float32)
        pg = jax.nn.softmax(lg, axis=-1)
        g = jnp.argmax(lg, axis=-1)
        pg_sel = jnp.take_along_axis(pg, g[:, None], axis=1)[:, 0]
        le = (hf @ w_router.astype(jnp.float32)).reshape(-1, N_GROUPS, EXPERTS_PER_GROUP)
        sel = jnp.take_along_axis(le, g[:, None, None], axis=1)[:, 0]
        pe = jax.nn.softmax(sel, axis=-1)
        topv, topi = lax.top_k(pe, TOP_K_IN_GROUP)
        wts = topv / jnp.sum(topv, axis=-1, keepdims=True) * pg_sel[:, None]
        eidx = g[:, None] * EXPERTS_PER_GROUP + topi
        gate = jnp.sum(jax.nn.one_hot(eidx, N_EXPERTS, dtype=jnp.float32) * wts[..., None], axis=1)
        gu = jnp.einsum('td,edf->tef', hs, w_gate_up)
        act = jax.nn.silu(gu[..., :EXPERT_FF]) * gu[..., EXPERT_FF:] * gate[..., None].astype(hs.dtype)
        return jnp.einsum('tef,efd->td', act, w_down)
    return lax.map(one_seq, h)


def trunk(x, pos, conv_states, ret_states, mem_ks, mem_vs, norm_mix, norm_ffn, conv_w_in, conv_w,
          conv_w_out, ret_w_in, ret_gn, ret_w_out, moe_w_group, moe_w_router, moe_w_gate_up,
          moe_w_down, final_norm):
    B, T = x.shape[0], x.shape[1]
    new_conv, new_ret = [], []
    for i in range(DEPTH):
        h = rmsnorm(x, norm_mix[i])
        j = i // N_MIXERS
        if i % N_MIXERS == 0:
            z = h @ conv_w_in[j]
            mix, ns = conv_mixer(z[..., :CONV_MIX_W], conv_states[j], conv_w[j])
            new_conv.append(ns)
            q_x = z[..., CONV_MIX_W:]
            w_out = conv_w_out[j]
        else:
            z = h @ ret_w_in[j]
            mix, ns = retention_mixer(z[..., :RET_MIX_W], ret_states[j], pos, ret_gn[j])
            new_ret.append(ns)
            q_x = z[..., RET_MIX_W:]
            w_out = ret_w_out[j]
        xa = cross_attn(q_x.reshape(B, T, XA_HEADS, XA_DIM), mem_ks[i], mem_vs[i])
        x = x + jnp.concatenate([mix, xa], axis=-1) @ w_out
        x = x + hier_moe(rmsnorm(x, norm_ffn[i]), moe_w_group[i], moe_w_router[i],
                         moe_w_gate_up[i], moe_w_down[i])
    return rmsnorm(x, final_norm), jnp.stack(new_conv), jnp.stack(new_ret)


def setup_inputs(seed: int = 0) -> dict:
    key = jax.random.key(seed)
    ks = jax.random.split(key, 24)
    f32 = jnp.float32

    def nrm(k, shape, scale):
        return jax.random.normal(k, shape, f32) * scale

    return {
        'x_prompt': nrm(ks[0], (BATCH, SEQ, D_MODEL), 1.0),
        'x_sample': nrm(ks[1], (DEC_BATCH, DEC_SEQ, D_MODEL), 1.0),
        'state_conv': nrm(ks[2], (N_CONV_LAYERS, DEC_BATCH, CONV_WIDTH - 1, CONV_DIM), 1.0),
        'state_ret': nrm(ks[3], (N_RET_LAYERS, DEC_BATCH, RET_HEADS, RET_DK, RET_DV), 0.25),
        'cache_mem_k': nrm(ks[4], (DEPTH, DEC_BATCH, N_MEM, XA_HEADS, XA_DIM), 1.0),
        'cache_mem_v': nrm(ks[5], (DEPTH, DEC_BATCH, N_MEM, XA_HEADS, XA_DIM), 1.0),
        'mem_prompt': nrm(ks[6], (BATCH, N_MEM, D_MODEL), 1.0),
        'norm_mix': 1.0 + nrm(ks[7], (DEPTH, D_MODEL), 0.02),
        'norm_ffn': 1.0 + nrm(ks[8], (DEPTH, D_MODEL), 0.02),
        'mem_norm': 1.0 + nrm(ks[9], (DEPTH, D_MODEL), 0.02),
        'w_mem_kv': nrm(ks[10], (DEPTH, D_MODEL, 2 * XA_WIDTH), D_MODEL ** -0.5),
        'conv_w_in': nrm(ks[11], (N_CONV_LAYERS, D_MODEL, CONV_IN), D_MODEL ** -0.5),
        'conv_w': nrm(ks[12], (N_CONV_LAYERS, CONV_WIDTH, CONV_DIM), CONV_WIDTH ** -0.5),
        'conv_w_out': nrm(ks[13], (N_CONV_LAYERS, CONV_OUT_IN, D_MODEL), CONV_OUT_IN ** -0.5),
        'ret_w_in': nrm(ks[14], (N_RET_LAYERS, D_MODEL, RET_IN), D_MODEL ** -0.5),
        'ret_gn': 1.0 + nrm(ks[15], (N_RET_LAYERS, RET_HEADS * RET_DV), 0.02),
        'ret_w_out': nrm(ks[16], (N_RET_LAYERS, RET_OUT_IN, D_MODEL), RET_OUT_IN ** -0.5),
        'moe_w_group': nrm(ks[17], (DEPTH, D_MODEL, N_GROUPS), D_MODEL ** -0.5),
        'moe_w_router': nrm(ks[18], (DEPTH, D_MODEL, N_EXPERTS), D_MODEL ** -0.5),
        'moe_w_gate_up': nrm(ks[19], (DEPTH, N_EXPERTS, D_MODEL, 2 * EXPERT_FF), D_MODEL ** -0.5),
        'moe_w_down': nrm(ks[20], (DEPTH, N_EXPERTS, EXPERT_FF, D_MODEL), EXPERT_FF ** -0.5),
        'final_norm': 1.0 + nrm(ks[21], (D_MODEL,), 0.02),
    }


def reference(x_prompt, x_sample, state_conv, state_ret, cache_mem_k, cache_mem_v, mem_prompt,
              norm_mix, norm_ffn, mem_norm, w_mem_kv, conv_w_in, conv_w, conv_w_out, ret_w_in,
              ret_gn, ret_w_out, moe_w_group, moe_w_router, moe_w_gate_up, moe_w_down, final_norm):
    B = x_prompt.shape[0]
    mk, mv = [], []
    for i in range(DEPTH):
        kv = rmsnorm(mem_prompt, mem_norm[i]) @ w_mem_kv[i]
        mk.append(kv[..., :XA_WIDTH].reshape(B, N_MEM, XA_HEADS, XA_DIM))
        mv.append(kv[..., XA_WIDTH:].reshape(B, N_MEM, XA_HEADS, XA_DIM))
    mem_k_prompt = jnp.stack(mk)
    mem_v_prompt = jnp.stack(mv)

    pos_prompt = jnp.arange(SEQ, dtype=jnp.int32)
    pos_sample = PAST_LEN + jnp.arange(DEC_SEQ, dtype=jnp.int32)
    conv0 = [jnp.zeros((B, CONV_WIDTH - 1, CONV_DIM), x_prompt.dtype) for _ in range(N_CONV_LAYERS)]
    ret0 = [jnp.zeros((B, RET_HEADS, RET_DK, RET_DV), jnp.float32) for _ in range(N_RET_LAYERS)]

    y_prompt, conv_prompt, ret_prompt = trunk(
        x_prompt, pos_prompt, conv0, ret0,
        [mem_k_prompt[i] for i in range(DEPTH)], [mem_v_prompt[i] for i in range(DEPTH)],
        norm_mix, norm_ffn, conv_w_in, conv_w, conv_w_out, ret_w_in, ret_gn, ret_w_out,
        moe_w_group, moe_w_router, moe_w_gate_up, moe_w_down, final_norm)
    y_sample, conv_sample, ret_sample = trunk(
        x_sample, pos_sample,
        [state_conv[j] for j in range(N_CONV_LAYERS)], [state_ret[j] for j in range(N_RET_LAYERS)],
        [cache_mem_k[i] for i in range(DEPTH)], [cache_mem_v[i] for i in range(DEPTH)],
        norm_mix, norm_ffn, conv_w_in, conv_w, conv_w_out, ret_w_in, ret_gn, ret_w_out,
        moe_w_group, moe_w_router, moe_w_gate_up, moe_w_down, final_norm)
    return (y_prompt, y_sample, conv_prompt, conv_sample, ret_prompt, ret_sample, mem_k_prompt, mem_v_prompt)
```

```python
import functools

import jax
import jax.numpy as jnp
from jax import lax
from jax.experimental import pallas as pl
from jax.experimental.pallas import tpu as pltpu

F32 = jnp.float32
BF16 = jnp.bfloat16
I32 = jnp.int32

D_MODEL = 1024
DEPTH = 2
CONV_WIDTH = 3
RET_HEADS = 4
RET_DK = 256
RET_DV = 512
RET_CHUNK = 128
ROPE_BASE = 10000.0
N_MEM = 256
XA_HEADS = 4
XA_DIM = 256
N_GROUPS = 4
EXPERTS_PER_GROUP = 8
N_EXPERTS = N_GROUPS * EXPERTS_PER_GROUP
EXPERT_FF = 256
EPS = 1e-6

LANES = 128
AUG_W = D_MODEL + LANES
GATE_LANE0 = D_MODEL
GID_LANE = EXPERTS_PER_GROUP
ROUTER_W = LANES
VMEM_LIMIT = 56 * 1024 * 1024
MOE_TM = 256
SHORT_ROWS = 8
PAST_LEN = 16384
NT_DIMS = (((1,), (1,)), ((), ()))
TN_DIMS = (((0,), (0,)), ((), ()))


def _cparams(sem):
    return pltpu.CompilerParams(dimension_semantics=sem, vmem_limit_bytes=VMEM_LIMIT)


def _rms(x, g):
    return x * lax.rsqrt(jnp.mean(x * x, axis=-1, keepdims=True) + EPS) * g


def _silu(x):
    return x * jax.nn.sigmoid(x)


def _resident(shape):
    nd = len(shape)
    return pl.BlockSpec(shape, lambda *_: (0,) * nd, pipeline_mode=pl.Buffered(1))


def _memkv_kernel(x_ref, g_ref, w_ref, k_ref, v_ref):
    h = _rms(x_ref[...], g_ref[...]).astype(BF16)
    kv = jnp.dot(h, w_ref[...], preferred_element_type=F32)
    k_ref[...] = kv[:, :D_MODEL]
    v_ref[...] = kv[:, D_MODEL:]


def _mem_kv(mem2d, mem_norm, w_kv):
    rows = mem2d.shape[0]
    tm = min(512, rows)
    out = jax.ShapeDtypeStruct((DEPTH, rows, D_MODEL), F32)
    return pl.pallas_call(
        _memkv_kernel,
        grid=(DEPTH, rows // tm),
        in_specs=[
            pl.BlockSpec((tm, D_MODEL), lambda l, j: (j, 0)),
            pl.BlockSpec((None, 1, D_MODEL), lambda l, j: (l, 0, 0)),
            pl.BlockSpec((None, D_MODEL, 2 * D_MODEL), lambda l, j: (l, 0, 0)),
        ],
        out_specs=[
            pl.BlockSpec((None, tm, D_MODEL), lambda l, j: (l, j, 0)),
            pl.BlockSpec((None, tm, D_MODEL), lambda l, j: (l, j, 0)),
        ],
        out_shape=[out, out],
        compiler_params=_cparams(("arbitrary", "arbitrary")),
        name="mem_kv",
    )(mem2d, mem_norm.reshape(DEPTH, 1, D_MODEL), w_kv)


def _inproj_kernel(x_ref, g_ref, w_ref, o_ref, *, tn):
    h = _rms(x_ref[...], g_ref[...]).astype(BF16)
    for j in range(o_ref.shape[1] // tn):
        sl = slice(j * tn, (j + 1) * tn)
        o_ref[:, sl] = jnp.dot(h, w_ref[:, sl], preferred_element_type=F32).astype(BF16)


def _inproj(x, g, w):
    m, n = x.shape[0], w.shape[1]
    tm = min(512, m)
    return pl.pallas_call(
        functools.partial(_inproj_kernel, tn=512),
        grid=(m // tm,),
        in_specs=[
            pl.BlockSpec((tm, D_MODEL), lambda i: (i, 0)),
            _resident((1, D_MODEL)),
            _resident((D_MODEL, n)),
        ],
        out_specs=pl.BlockSpec((tm, n), lambda i: (i, 0)),
        out_shape=jax.ShapeDtypeStruct((m, n), BF16),
        compiler_params=_cparams(("arbitrary",)),
        name="inproj",
    )(x, g.reshape(1, D_MODEL), w)


def _conv_taps(u, prev1, prev2, w_ref):
    return (w_ref[0:1, :] * prev2 + w_ref[1:2, :] * prev1) + w_ref[2:3, :] * u


def _conv_long_kernel(b_ref, c_ref, v_ref, w_ref, st0_ref, mix_ref, st_ref, carry, *, tiles_per_seq):
    i = pl.program_id(0)
    tt = c_ref.shape[0]

    @pl.when(i % tiles_per_seq == 0)
    def _():
        carry[6:8, :] = st0_ref[0]

    u = c_ref[...].astype(F32) * v_ref[...].astype(F32)
    pm2 = carry[6:7, :]
    pm1 = carry[7:8, :]
    row = lax.broadcasted_iota(I32, u.shape, 0)
    prev1 = jnp.where(row == 0, pm1, pltpu.roll(u, 1, 0))
    prev2 = jnp.where(row == 0, pm2, jnp.where(row == 1, pm1, pltpu.roll(u, 2, 0)))
    mix_ref[...] = (b_ref[...].astype(F32) * _conv_taps(u, prev1, prev2, w_ref)).astype(BF16)
    carry[...] = u[tt - 8:tt, :]
    st_ref[0] = carry[6:8, :]


def _conv_long(z, conv_w, state0, batch, seq):
    tt = min(512, seq)
    tps = seq // tt
    col = lambda j: pl.BlockSpec((tt, D_MODEL), lambda i, j=j: (i, j))
    return pl.pallas_call(
        functools.partial(_conv_long_kernel, tiles_per_seq=tps),
        grid=(batch * tps,),
        in_specs=[col(0), col(1), col(2), _resident((CONV_WIDTH, D_MODEL)),
                  pl.BlockSpec((1, 2, D_MODEL), lambda i: (i // tps, 0, 0))],
        out_specs=[pl.BlockSpec((tt, D_MODEL), lambda i: (i, 0)),
                   pl.BlockSpec((1, 2, D_MODEL), lambda i: (i // tps, 0, 0))],
        out_shape=[jax.ShapeDtypeStruct((batch * seq, D_MODEL), BF16),
                   jax.ShapeDtypeStruct((batch, 2, D_MODEL), F32)],
        scratch_shapes=[pltpu.VMEM((8, D_MODEL), F32)],
        compiler_params=_cparams(("arbitrary",)),
        name="conv_long",
    )(z, z, z, conv_w, state0)


def _conv_short_kernel(b_ref, c_ref, v_ref, w_ref, p1_ref, p2_ref, mix_ref, u_ref, *, seq):
    u = c_ref[...].astype(F32) * v_ref[...].astype(F32)
    t = lax.broadcasted_iota(I32, u.shape, 0) % seq
    prev1 = jnp.where(t == 0, p1_ref[...], pltpu.roll(u, 1, 0))
    prev2 = jnp.where(t <= 1, p2_ref[...], pltpu.roll(u, 2, 0))
    mix_ref[...] = (b_ref[...].astype(F32) * _conv_taps(u, prev1, prev2, w_ref)).astype(BF16)
    u_ref[...] = u


def _conv_short(z, conv_w, state0, batch, seq):
    m = batch * seq
    zeros = jnp.zeros((batch, seq - 1, D_MODEL), F32)
    p1 = jnp.concatenate([state0[:, 1:2], zeros], axis=1).reshape(m, D_MODEL)
    p2 = jnp.concatenate([state0, zeros[:, 1:]], axis=1).reshape(m, D_MODEL)
    col = lambda j: pl.BlockSpec((m, D_MODEL), lambda i, j=j: (0, j))
    full = pl.BlockSpec((m, D_MODEL), lambda i: (0, 0))
    mix, u = pl.pallas_call(
        functools.partial(_conv_short_kernel, seq=seq),
        grid=(1,),
        in_specs=[col(0), col(1), col(2), _resident((CONV_WIDTH, D_MODEL)), full, full],
        out_specs=[full, full],
        out_shape=[jax.ShapeDtypeStruct((m, D_MODEL), BF16), jax.ShapeDtypeStruct((m, D_MODEL), F32)],
        compiler_params=_cparams(("arbitrary",)),
        name="conv_short",
    )(z, z, z, conv_w, p1, p2)
    return mix, u.reshape(batch, seq, D_MODEL)[:, seq - 2:]


def _xattn_kernel(q_ref, k_ref, v_ref, o_ref, *, nb):
    scale = XA_DIM ** -0.5
    for b in range(nb):
        for h in range(XA_HEADS):
            sl = slice(h * XA_DIM, (h + 1) * XA_DIM)
            q = q_ref[b, :, sl]
            k = k_ref[b, :, sl].astype(BF16)
            v = v_ref[b, :, sl].astype(BF16)
            s = lax.dot_general(q, k, NT_DIMS, preferred_element_type=F32) * scale
            p = jnp.exp(s - jnp.max(s, axis=-1, keepdims=True))
            p = p / jnp.sum(p, axis=-1, keepdims=True)
            o_ref[b, :, sl] = jnp.dot(p.astype(BF16), v, preferred_element_type=F32).astype(BF16)


def _xattn(z3, qcol, mem_k, mem_v, mem_off, nb, tq):
    batch, seq = z3.shape[0], z3.shape[1]
    mo = mem_off // nb
    return pl.pallas_call(
        functools.partial(_xattn_kernel, nb=nb),
        grid=(batch // nb, seq // tq),
        in_specs=[
            pl.BlockSpec((nb, tq, D_MODEL), lambda i, t: (i, t, qcol)),
            pl.BlockSpec((nb, N_MEM, D_MODEL), lambda i, t: (mo + i, 0, 0)),
            pl.BlockSpec((nb, N_MEM, D_MODEL), lambda i, t: (mo + i, 0, 0)),
        ],
        out_specs=pl.BlockSpec((nb, tq, D_MODEL), lambda i, t: (i, t, 0)),
        out_shape=jax.ShapeDtypeStruct((batch, seq, D_MODEL), BF16),
        compiler_params=_cparams(("arbitrary", "arbitrary")),
        name="xattn",
    )(z3, mem_k, mem_v)


def _rotary(x, cos, sin):
    half = x.shape[-1] // 2
    x1, x2 = x[:, :half], x[:, half:]
    return jnp.concatenate([x1 * cos - x2 * sin, x1 * sin + x2 * cos], axis=-1)


def _group_norm_gate(o, gate, gn):
    mu = jnp.mean(o, axis=-1, keepdims=True)
    var = jnp.mean(jnp.square(o - mu), axis=-1, keepdims=True)
    return _silu(gate) * ((o - mu) * lax.rsqrt(var + EPS) * gn)


def _ret_long_kernel(full_ref, q_ref, k_ref, v_ref, g_ref, cos_ref, sin_ref, dec_ref, inn_ref, tail_ref,
                     gn_ref, s0_ref, mix_ref, sfin_ref, s_acc, *, chunk):
    h = pl.program_id(1)
    t = pl.program_id(2)

    @pl.when(t == 0)
    def _():
        s_acc[...] = s0_ref[0, 0]

    cos, sin = cos_ref[...], sin_ref[...]
    q = _rotary(q_ref[0].astype(F32), cos, sin)
    k = _rotary(k_ref[0].astype(F32), cos, sin) * (RET_DK ** -0.5)
    dec = dec_ref[0]
    inn = inn_ref[0][:, 0:1]
    tail = tail_ref[0][:, 0:1]
    full = full_ref[h]
    gn = gn_ref[...]
    for c in range(q.shape[0] // chunk):
        sl = slice(c * chunk, (c + 1) * chunk)
        qc = q[sl].astype(BF16)
        kc = k[sl]
        vc = v_ref[0, sl, :]
        s_old = s_acc[...]
        scores = lax.dot_general(qc, kc.astype(BF16), NT_DIMS, preferred_element_type=F32) * dec
        o = jnp.dot(scores.astype(BF16), vc, preferred_element_type=F32)
        o = o + jnp.dot(qc, s_old.astype(BF16), preferred_element_type=F32) * inn
        s_acc[...] = full * s_old + lax.dot_general((kc * tail).astype(BF16), vc, TN_DIMS,
                                                    preferred_element_type=F32)
        mix_ref[0, sl, :] = _group_norm_gate(o, g_ref[0, sl, :].astype(F32), gn).astype(BF16)

    @pl.when(t == pl.num_programs(2) - 1)
    def _():
        sfin_ref[0, 0] = s_acc[...]


def _ret_tables(pos, chunk):
    half = RET_DK // 2
    inv = ROPE_BASE ** (-jnp.arange(half, dtype=F32) / half)
    ang = pos.astype(F32)[:, None] * inv[None, :]
    log_g = jnp.log1p(-jnp.exp2(-5.0 - jnp.arange(RET_HEADS, dtype=F32)))
    idx = jnp.arange(chunk)
    diff = idx[:, None] - idx[None, :]
    decay = jnp.where(diff[None] >= 0,
                      jnp.exp(jnp.maximum(diff, 0)[None].astype(F32) * log_g[:, None, None]), 0.0)
    inner = jnp.exp((idx + 1).astype(F32)[None, :] * log_g[:, None])
    tail = jnp.exp((chunk - 1 - idx).astype(F32)[None, :] * log_g[:, None])
    full = jnp.exp(chunk * log_g)
    return jnp.cos(ang), jnp.sin(ang), decay, inner, tail, full


def _ret_long(z3, state0, gn, pos):
    batch, seq = z3.shape[0], z3.shape[1]
    chunk = RET_CHUNK
    tc = min(512, seq)
    cos, sin, decay, inner, tail, full = _ret_tables(pos, chunk)
    lane_b = lambda a: jnp.broadcast_to(a[:, :, None], (RET_HEADS, chunk, LANES))
    kq = RET_HEADS * RET_DK // RET_DK
    vq = 2 * RET_HEADS * RET_DK // RET_DV
    gq = vq + RET_HEADS
    tab = pl.BlockSpec((1, chunk, LANES), lambda b, h, t: (h, 0, 0))
    return pl.pallas_call(
        functools.partial(_ret_long_kernel, chunk=chunk),
        grid=(batch, RET_HEADS, seq // tc),
        in_specs=[
            pl.BlockSpec(memory_space=pltpu.SMEM),
            pl.BlockSpec((1, tc, RET_DK), lambda b, h, t: (b, t, h)),
            pl.BlockSpec((1, tc, RET_DK), lambda b, h, t: (b, t, kq + h)),
            pl.BlockSpec((1, tc, RET_DV), lambda b, h, t: (b, t, vq + h)),
            pl.BlockSpec((1, tc, RET_DV), lambda b, h, t: (b, t, gq + h)),
            pl.BlockSpec((tc, RET_DK // 2), lambda b, h, t: (t, 0)),
            pl.BlockSpec((tc, RET_DK // 2), lambda b, h, t: (t, 0)),
            pl.BlockSpec((1, chunk, chunk), lambda b, h, t: (h, 0, 0)),
            tab, tab,
            pl.BlockSpec((1, RET_DV), lambda b, h, t: (0, h)),
            pl.BlockSpec((1, 1, RET_DK, RET_DV), lambda b, h, t: (b, h, 0, 0)),
        ],
        out_specs=[
            pl.BlockSpec((1, tc, RET_DV), lambda b, h, t: (b, t, h)),
            pl.BlockSpec((1, 1, RET_DK, RET_DV), lambda b, h, t: (b, h, 0, 0)),
        ],
        out_shape=[jax.ShapeDtypeStruct((batch, seq, RET_HEADS * RET_DV), BF16),
                   jax.ShapeDtypeStruct((batch, RET_HEADS, RET_DK, RET_DV), F32)],
        scratch_shapes=[pltpu.VMEM((RET_DK, RET_DV), F32)],
        compiler_params=_cparams(("arbitrary", "arbitrary", "arbitrary")),
        name="ret_long",
    )(full, z3, z3, z3, z3, cos, sin, decay, lane_b(inner), lane_b(tail), gn.reshape(1, -1), state0)


def _ret_short_kernel(full_ref, q_ref, k_ref, v_ref, g_ref, cos_ref, sin_ref, dec_ref, inn_ref, tail_ref,
                      gn_ref, s0_ref, mix_ref, snew_ref, *, nb, seq):
    def pad_rows(a):
        return jnp.concatenate([a, jnp.zeros((SHORT_ROWS - seq, a.shape[1]), F32)], axis=0)

    cos, sin = cos_ref[...], sin_ref[...]
    for b in range(nb):
        for h in range(RET_HEADS):
            qs = slice(h * RET_DK, (h + 1) * RET_DK)
            vs = slice(h * RET_DV, (h + 1) * RET_DV)
            q = pad_rows(_rotary(q_ref[b, :, qs].astype(F32), cos, sin)).astype(BF16)
            k = pad_rows(_rotary(k_ref[b, :, qs].astype(F32), cos, sin) * (RET_DK ** -0.5))
            v = pad_rows(v_ref[b, :, vs].astype(F32)).astype(BF16)
            s_old = s0_ref[b, h]
            scores = lax.dot_general(q, k.astype(BF16), NT_DIMS, preferred_element_type=F32) * dec_ref[h]
            o = jnp.dot(scores.astype(BF16), v, preferred_element_type=F32)
            o = o + jnp.dot(q, s_old.astype(BF16), preferred_element_type=F32) * inn_ref[h]
            snew_ref[b, h] = full_ref[h] * s_old + lax.dot_general((k * tail_ref[h]).astype(BF16), v, TN_DIMS,
                                                                   preferred_element_type=F32)
            mix_ref[b, :, vs] = _group_norm_gate(o[:seq], g_ref[b, :, vs].astype(F32), gn_ref[:, vs]).astype(BF16)


def _ret_short(z3, state0, gn, pos, nb=2):
    batch, seq = z3.shape[0], z3.shape[1]
    cos, sin, decay, inner, tail, full = _ret_tables(pos, seq)
    qw = RET_HEADS * RET_DK
    vw = RET_HEADS * RET_DV
    small = lambda a: _resident(a.shape)
    rp = SHORT_ROWS - seq
    decay = jnp.pad(decay, ((0, 0), (0, rp), (0, rp)))
    inner3 = jnp.pad(inner, ((0, 0), (0, rp)))[:, :, None]
    tail3 = jnp.pad(tail, ((0, 0), (0, rp)))[:, :, None]
    return pl.pallas_call(
        functools.partial(_ret_short_kernel, nb=nb, seq=seq),
        grid=(batch // nb,),
        in_specs=[
            pl.BlockSpec(memory_space=pltpu.SMEM),
            pl.BlockSpec((nb, seq, qw), lambda i: (i, 0, 0)),
            pl.BlockSpec((nb, seq, qw), lambda i: (i, 0, 1)),
            pl.BlockSpec((nb, seq, vw), lambda i: (i, 0, 1)),
            pl.BlockSpec((nb, seq, vw), lambda i: (i, 0, 2)),
            small(cos), small(sin), small(decay), small(inner3), small(tail3),
            _resident((1, vw)),
            pl.BlockSpec((nb, RET_HEADS, RET_DK, RET_DV), lambda i: (i, 0, 0, 0)),
        ],
        out_specs=[
            pl.BlockSpec((nb, seq, vw), lambda i: (i, 0, 0)),
            pl.BlockSpec((nb, RET_HEADS, RET_DK, RET_DV), lambda i: (i, 0, 0, 0)),
        ],
        out_shape=[jax.ShapeDtypeStruct((batch, seq, vw), BF16),
                   jax.ShapeDtypeStruct((batch, RET_HEADS, RET_DK, RET_DV), F32)],
        compiler_params=_cparams(("arbitrary",)),
        name="ret_short",
    )(full, z3, z3, z3, z3, cos, sin, decay, inner3, tail3, gn.reshape(1, -1), state0)


def _route(logits):
    lane = lax.broadcasted_iota(I32, logits.shape, 1)
    neg = jnp.float32(-1e30)
    big = jnp.int32(LANES)
    is_g = lane < N_GROUPS
    lg = jnp.where(is_g, logits, neg)
    gmax = jnp.max(lg, axis=-1, keepdims=True)
    gidx = jnp.min(jnp.where(lg == gmax, lane, big), axis=-1, keepdims=True)
    pg_sel = 1.0 / jnp.sum(jnp.where(is_g, jnp.exp(lg - gmax), 0.0), axis=-1, keepdims=True)
    lo = N_GROUPS + EXPERTS_PER_GROUP * gidx
    in_grp = (lane >= lo) & (lane < lo + EXPERTS_PER_GROUP)
    sel = jnp.where(in_grp, logits, neg)
    e = jnp.where(in_grp, jnp.exp(sel - jnp.max(sel, axis=-1, keepdims=True)), 0.0)
    pe = jnp.where(in_grp, e / jnp.sum(e, axis=-1, keepdims=True), -1.0)
    t1 = jnp.max(pe, axis=-1, keepdims=True)
    i1 = jnp.min(jnp.where(pe == t1, lane, big), axis=-1, keepdims=True)
    pe2 = jnp.where(lane == i1, -1.0, pe)
    t2 = jnp.max(pe2, axis=-1, keepdims=True)
    i2 = jnp.min(jnp.where(pe2 == t2, lane, big), axis=-1, keepdims=True)
    den = t1 + t2
    gates = jnp.where(lane == i1, t1 / den * pg_sel, 0.0) + jnp.where(lane == i2, t2 / den * pg_sel, 0.0)
    out = jnp.zeros_like(gates)
    for g in range(N_GROUPS):
        shifted = pltpu.roll(gates, LANES - (N_GROUPS + EXPERTS_PER_GROUP * g), 1)
        out = jnp.where((gidx == g) & (lane < EXPERTS_PER_GROUP), shifted, out)
    return jnp.where(lane == GID_LANE, gidx.astype(F32), out)


def _outproj_kernel(x_ref, mix_ref, xa_ref, wa_ref, wb_ref, g_ref, wr_ref, o_ref):
    acc = jnp.dot(mix_ref[...], wa_ref[...], preferred_element_type=F32)
    acc = acc + jnp.dot(xa_ref[...], wb_ref[...], preferred_element_type=F32)
    xn = x_ref[...] + acc
    h2 = _rms(xn, g_ref[...])
    logits = jnp.dot(h2, wr_ref[...], preferred_element_type=F32, precision=lax.Precision.HIGHEST)
    o_ref[:, :D_MODEL] = xn
    o_ref[:, D_MODEL:] = _route(logits)


def _outproj_router(x, mix, xa, w_out, g_ffn, w_router):
    m, cm = mix.shape
    tm = min(512, m)
    row = lambda w: pl.BlockSpec((tm, w), lambda i: (i, 0))
    return pl.pallas_call(
        _outproj_kernel,
        grid=(m // tm,),
        in_specs=[row(D_MODEL), row(cm), row(D_MODEL), _resident((cm, D_MODEL)), _resident((D_MODEL, D_MODEL)),
                  _resident((1, D_MODEL)), _resident((D_MODEL, ROUTER_W))],
        out_specs=row(AUG_W),
        out_shape=jax.ShapeDtypeStruct((m, AUG_W), F32),
        compiler_params=_cparams(("arbitrary",)),
        name="outproj_router",
    )(x, mix, xa, w_out[:cm], w_out[cm:], g_ffn.reshape(1, D_MODEL), w_router)


def _moe_plan(gid, tm):
    m = gid.shape[0]
    nt = m // tm + N_GROUPS
    onehot = (gid[:, None] == jnp.arange(N_GROUPS, dtype=I32)[None, :]).astype(I32)
    csum = jnp.cumsum(onehot, axis=0)
    rank = jnp.sum(csum * onehot, axis=1) - 1
    counts = csum[-1]
    ntile = (counts + tm - 1) // tm
    tend = jnp.cumsum(ntile)
    tstart = tend - ntile
    pos = tstart[gid] * tm + rank
    row_tok = jnp.full((nt * tm,), -1, I32).at[pos].set(jnp.arange(m, dtype=I32))
    tiles = jnp.arange(nt, dtype=I32)
    tgid = jnp.minimum(jnp.sum((tiles[:, None] >= tend[None, :]).astype(I32), axis=1), N_GROUPS - 1)
    nvalid = jnp.clip(counts[tgid] - (tiles - tstart[tgid]) * tm, 0, tm)
    nvalid = jnp.where(tiles < tend[-1], nvalid, 0)
    return tgid.astype(I32), nvalid.astype(I32), row_tok


def _moe_kernel(tgid_ref, nv_ref, tok_ref, xaug_hbm, g_ref, fin_ref, wgu_ref, wd_ref, out_hbm,
                xbuf, ybuf, gsem, ssem, *, tm, final_norm):
    i = pl.program_id(0)
    nt = pl.num_programs(0)
    slot = i % 2

    def gather_copy(tile, r, sl):
        tok = jnp.maximum(tok_ref[tile * tm + r], 0)
        return pltpu.make_async_copy(xaug_hbm.at[pl.ds(tok, 1)], xbuf.at[sl, pl.ds(r, 1)], gsem.at[sl])

    def scatter_copy(tile, r, sl):
        tok = jnp.maximum(tok_ref[tile * tm + r], 0)
        return pltpu.make_async_copy(ybuf.at[sl, pl.ds(r, 1)], out_hbm.at[pl.ds(tok, 1)], ssem.at[sl])

    def for_rows(n, fn):
        lax.fori_loop(0, n, lambda r, c: (fn(r), c)[1], 0)

    @pl.when(i == 0)
    def _():
        for_rows(tm, lambda r: gather_copy(0, r, 0).start())

    nxt = jnp.minimum(i + 1, nt - 1)

    @pl.when((i + 1 < nt) & (nv_ref[nxt] > 0))
    def _():
        for_rows(tm, lambda r: gather_copy(nxt, r, 1 - slot).start())

    valid = nv_ref[i] > 0

    @pl.when(valid)
    def _():
        for_rows(tm, lambda r: gather_copy(i, r, slot).wait())
        xa = xbuf[slot]
        xn = xa[:, :D_MODEL]
        gl = xa[:, GATE_LANE0:]
        h2 = _rms(xn, g_ref[...]).astype(BF16)
        acc = jnp.zeros((tm, D_MODEL), F32)
        for j in range(EXPERTS_PER_GROUP):
            gu = jnp.dot(h2, wgu_ref[j], preferred_element_type=F32)
            act = _silu(gu[:, :EXPERT_FF]) * gu[:, EXPERT_FF:] * gl[:, j:j + 1]
            acc = acc + jnp.dot(act.astype(BF16), wd_ref[j], preferred_element_type=F32)
        y = xn + acc
        if final_norm:
            y = _rms(y, fin_ref[...])
        ybuf[slot] = y

    prv = jnp.maximum(i - 1, 0)

    @pl.when((i > 0) & (nv_ref[prv] > 0))
    def _():
        for_rows(nv_ref[prv], lambda r: scatter_copy(prv, r, 1 - slot).wait())

    @pl.when(valid)
    def _():
        for_rows(nv_ref[i], lambda r: scatter_copy(i, r, slot).start())

    @pl.when((i == nt - 1) & valid)
    def _():
        for_rows(nv_ref[i], lambda r: scatter_copy(i, r, slot).wait())


def _moe(xaug, g_ffn, final_g, w_gu, w_down, final_norm):
    m = xaug.shape[0]
    tm = min(MOE_TM, m)
    gid = xaug[:, GATE_LANE0 + GID_LANE].astype(I32)
    tgid, nvalid, row_tok = _moe_plan(gid, tm)
    nt = tgid.shape[0]
    epg = EXPERTS_PER_GROUP
    grid_spec = pltpu.PrefetchScalarGridSpec(
        num_scalar_prefetch=3,
        grid=(nt,),
        in_specs=[
            pl.BlockSpec(memory_space=pl.ANY),
            pl.BlockSpec((1, D_MODEL), lambda i, *_: (0, 0)),
            pl.BlockSpec((1, D_MODEL), lambda i, *_: (0, 0)),
            pl.BlockSpec((epg, D_MODEL, 2 * EXPERT_FF), lambda i, tg, nv, tok: (tg[i], 0, 0)),
            pl.BlockSpec((epg, EXPERT_FF, D_MODEL), lambda i, tg, nv, tok: (tg[i], 0, 0)),
        ],
        out_specs=pl.BlockSpec(memory_space=pl.ANY),
        scratch_shapes=[
            pltpu.VMEM((2, tm, AUG_W), F32),
            pltpu.VMEM((2, tm, D_MODEL), F32),
            pltpu.SemaphoreType.DMA((2,)),
            pltpu.SemaphoreType.DMA((2,)),
        ],
    )
    return pl.pallas_call(
        functools.partial(_moe_kernel, tm=tm, final_norm=final_norm),
        grid_spec=grid_spec,
        out_shape=jax.ShapeDtypeStruct((m, D_MODEL), F32),
        compiler_params=_cparams(("arbitrary",)),
        name="moe",
    )(tgid, nvalid, row_tok, xaug, g_ffn.reshape(1, D_MODEL), final_g.reshape(1, D_MODEL), w_gu, w_down)


def _trunk(x, pos, conv_state, ret_state, mem_k, mem_v, mem_stride, wts, long_seq):
    batch, seq = x.shape[0], x.shape[1]
    m = batch * seq
    x2 = x.reshape(m, D_MODEL)
    if long_seq:
        nb, tq = 1, min(512, seq)
    else:
        nb, tq = 4, seq
    new_conv = new_ret = None
    for i in range(DEPTH):
        if i % 2 == 0:
            z = _inproj(x2, wts["norm_mix"][i], wts["conv_w_in"])
            conv = _conv_long if long_seq else _conv_short
            mix, new_conv = conv(z, wts["conv_w"], conv_state, batch, seq)
            qcol = 3
            w_out = wts["conv_w_out"]
        else:
            z = _inproj(x2, wts["norm_mix"][i], wts["ret_w_in"])
            ret = _ret_long if long_seq else _ret_short
            mix, new_ret = ret(z.reshape(batch, seq, -1), ret_state, wts["ret_gn"], pos)
            mix = mix.reshape(m, -1)
            qcol = 6
            w_out = wts["ret_w_out"]
        xa = _xattn(z.reshape(batch, seq, -1), qcol, mem_k, mem_v, i * mem_stride, nb, tq).reshape(m, D_MODEL)
        xaug = _outproj_router(x2, mix, xa, w_out, wts["norm_ffn"][i], wts["w_router"][i])
        x2 = _moe(xaug, wts["norm_ffn"][i], wts["final_norm"], wts["moe_w_gate_up"][i], wts["moe_w_down"][i],
                  final_norm=(i == DEPTH - 1))
    return x2.reshape(batch, seq, D_MODEL), new_conv, new_ret


def kernel(x_prompt, x_sample, state_conv, state_ret, cache_mem_k, cache_mem_v, mem_prompt, norm_mix, norm_ffn,
           mem_norm, w_mem_kv, conv_w_in, conv_w, conv_w_out, ret_w_in, ret_gn, ret_w_out, moe_w_group,
           moe_w_router, moe_w_gate_up, moe_w_down, final_norm):
    batch, seq = x_prompt.shape[0], x_prompt.shape[1]
    dec_batch, dec_seq = x_sample.shape[0], x_sample.shape[1]
    w_router = jnp.concatenate(
        [moe_w_group, moe_w_router, jnp.zeros((DEPTH, D_MODEL, ROUTER_W - N_GROUPS - N_EXPERTS), F32)], axis=-1)
    wts = dict(
        norm_mix=norm_mix, norm_ffn=norm_ffn, final_norm=final_norm, w_router=w_router,
        conv_w_in=conv_w_in[0].astype(BF16), conv_w=conv_w[0], conv_w_out=conv_w_out[0].astype(BF16),
        ret_w_in=ret_w_in[0].astype(BF16), ret_gn=ret_gn[0], ret_w_out=ret_w_out[0].astype(BF16),
        moe_w_gate_up=moe_w_gate_up.astype(BF16), moe_w_down=moe_w_down.astype(BF16),
    )
    mk, mv = _mem_kv(mem_prompt.reshape(batch * N_MEM, D_MODEL), mem_norm, w_mem_kv.astype(BF16))
    mem_k_prompt = mk.reshape(DEPTH, batch, N_MEM, XA_HEADS, XA_DIM)
    mem_v_prompt = mv.reshape(DEPTH, batch, N_MEM, XA_HEADS, XA_DIM)

    y_prompt, conv_p, ret_p = _trunk(
        x_prompt, jnp.arange(seq, dtype=I32),
        jnp.zeros((batch, CONV_WIDTH - 1, D_MODEL), F32), jnp.zeros((batch, RET_HEADS, RET_DK, RET_DV), F32),
        mk.reshape(DEPTH * batch, N_MEM, D_MODEL), mv.reshape(DEPTH * batch, N_MEM, D_MODEL), batch, wts, True)
    y_sample, conv_s, ret_s = _trunk(
        x_sample, PAST_LEN + jnp.arange(dec_seq, dtype=I32), state_conv[0], state_ret[0],
        cache_mem_k.reshape(DEPTH * dec_batch, N_MEM, D_MODEL), cache_mem_v.reshape(DEPTH * dec_batch, N_MEM, D_MODEL),
        dec_batch, wts, False)
    return (y_prompt, y_sample, conv_p[None], conv_s[None], ret_p[None], ret_s[None], mem_k_prompt, mem_v_prompt)
```

```python
import functools

import jax
import jax.numpy as jnp
from jax import lax
from jax.experimental import pallas as pl
from jax.experimental.pallas import tpu as pltpu

F32 = jnp.float32
BF16 = jnp.bfloat16
I32 = jnp.int32

D_MODEL = 1024
DEPTH = 2
CONV_WIDTH = 3
RET_HEADS = 4
RET_DK = 256
RET_DV = 512
RET_CHUNK = 128
ROPE_BASE = 10000.0
N_MEM = 256
XA_HEADS = 4
XA_DIM = 256
N_GROUPS = 4
EXPERTS_PER_GROUP = 8
N_EXPERTS = N_GROUPS * EXPERTS_PER_GROUP
EXPERT_FF = 256
EPS = 1e-6

LANES = 128
AUG_W = D_MODEL + LANES
GATE_LANE0 = D_MODEL
GID_LANE = EXPERTS_PER_GROUP
ROUTER_W = LANES
VMEM_LIMIT = 56 * 1024 * 1024
MOE_TM = 256
SHORT_ROWS = 8
PAST_LEN = 16384
NT_DIMS = (((1,), (1,)), ((), ()))
TN_DIMS = (((0,), (0,)), ((), ()))


def _cparams(sem):
    return pltpu.CompilerParams(dimension_semantics=sem, vmem_limit_bytes=VMEM_LIMIT)


def _rms(x, g):
    return x * lax.rsqrt(jnp.mean(x * x, axis=-1, keepdims=True) + EPS) * g


def _silu(x):
    return x * jax.nn.sigmoid(x)


def _resident(shape):
    nd = len(shape)
    return pl.BlockSpec(shape, lambda *_: (0,) * nd, pipeline_mode=pl.Buffered(1))


def _memkv_kernel(x_ref, g_ref, w_ref, k_ref, v_ref, *, nb):
    h = _rms(x_ref[...], g_ref[...]).astype(BF16)
    kv = jnp.dot(h, w_ref[...], preferred_element_type=F32)
    for b in range(nb):
        rows = slice(b * N_MEM, (b + 1) * N_MEM)
        for hd in range(XA_HEADS):
            k_ref[b, :, hd, :] = kv[rows, hd * XA_DIM:(hd + 1) * XA_DIM]
            v_ref[b, :, hd, :] = kv[rows, D_MODEL + hd * XA_DIM:D_MODEL + (hd + 1) * XA_DIM]


def _mem_kv(mem2d, mem_norm, w_kv):
    batch = mem2d.shape[0] // N_MEM
    nb = min(2, batch)
    tm = nb * N_MEM
    out = jax.ShapeDtypeStruct((DEPTH, batch, N_MEM, XA_HEADS, XA_DIM), F32)
    out_spec = pl.BlockSpec((None, nb, N_MEM, XA_HEADS, XA_DIM), lambda l, j: (l, j, 0, 0, 0))
    return pl.pallas_call(
        functools.partial(_memkv_kernel, nb=nb),
        grid=(DEPTH, batch // nb),
        in_specs=[
            pl.BlockSpec((tm, D_MODEL), lambda l, j: (j, 0)),
            pl.BlockSpec((None, 1, D_MODEL), lambda l, j: (l, 0, 0)),
            pl.BlockSpec((None, D_MODEL, 2 * D_MODEL), lambda l, j: (l, 0, 0)),
        ],
        out_specs=[out_spec, out_spec],
        out_shape=[out, out],
        compiler_params=_cparams(("arbitrary", "arbitrary")),
        name="mem_kv",
    )(mem2d, mem_norm.reshape(DEPTH, 1, D_MODEL), w_kv)


def _inproj_kernel(x_ref, g_ref, w_ref, o_ref, *, tn):
    h = _rms(x_ref[...], g_ref[...]).astype(BF16)
    for j in range(o_ref.shape[1] // tn):
        sl = slice(j * tn, (j + 1) * tn)
        o_ref[:, sl] = jnp.dot(h, w_ref[:, sl], preferred_element_type=F32).astype(BF16)


def _inproj(x, g, w):
    m, n = x.shape[0], w.shape[1]
    tm = min(512, m)
    return pl.pallas_call(
        functools.partial(_inproj_kernel, tn=512),
        grid=(m // tm,),
        in_specs=[
            pl.BlockSpec((tm, D_MODEL), lambda i: (i, 0)),
            _resident((1, D_MODEL)),
            _resident((D_MODEL, n)),
        ],
        out_specs=pl.BlockSpec((tm, n), lambda i: (i, 0)),
        out_shape=jax.ShapeDtypeStruct((m, n), BF16),
        compiler_params=_cparams(("arbitrary",)),
        name="inproj",
    )(x, g.reshape(1, D_MODEL), w)


def _conv_taps(u, prev1, prev2, w_ref):
    return (w_ref[0:1, :] * prev2 + w_ref[1:2, :] * prev1) + w_ref[2:3, :] * u


def _conv_long_kernel(b_ref, c_ref, v_ref, w_ref, st0_ref, mix_ref, st_ref, carry, *, tiles_per_seq):
    i = pl.program_id(0)
    tt = c_ref.shape[0]

    @pl.when(i % tiles_per_seq == 0)
    def _():
        carry[6:8, :] = st0_ref[0]

    u = c_ref[...].astype(F32) * v_ref[...].astype(F32)
    pm2 = carry[6:7, :]
    pm1 = carry[7:8, :]
    row = lax.broadcasted_iota(I32, u.shape, 0)
    prev1 = jnp.where(row == 0, pm1, pltpu.roll(u, 1, 0))
    prev2 = jnp.where(row == 0, pm2, jnp.where(row == 1, pm1, pltpu.roll(u, 2, 0)))
    mix_ref[...] = (b_ref[...].astype(F32) * _conv_taps(u, prev1, prev2, w_ref)).astype(BF16)
    carry[...] = u[tt - 8:tt, :]
    st_ref[0] = carry[6:8, :]


def _conv_long(z, conv_w, state0, batch, seq):
    tt = min(512, seq)
    tps = seq // tt
    col = lambda j: pl.BlockSpec((tt, D_MODEL), lambda i, j=j: (i, j))
    return pl.pallas_call(
        functools.partial(_conv_long_kernel, tiles_per_seq=tps),
        grid=(batch * tps,),
        in_specs=[col(0), col(1), col(2), _resident((CONV_WIDTH, D_MODEL)),
                  pl.BlockSpec((1, 2, D_MODEL), lambda i: (i // tps, 0, 0))],
        out_specs=[pl.BlockSpec((tt, D_MODEL), lambda i: (i, 0)),
                   pl.BlockSpec((1, 2, D_MODEL), lambda i: (i // tps, 0, 0))],
        out_shape=[jax.ShapeDtypeStruct((batch * seq, D_MODEL), BF16),
                   jax.ShapeDtypeStruct((batch, 2, D_MODEL), F32)],
        scratch_shapes=[pltpu.VMEM((8, D_MODEL), F32)],
        compiler_params=_cparams(("arbitrary",)),
        name="conv_long",
    )(z, z, z, conv_w, state0)


def _conv_short_kernel(b_ref, c_ref, v_ref, w_ref, p1_ref, p2_ref, mix_ref, u_ref, *, seq):
    u = c_ref[...].astype(F32) * v_ref[...].astype(F32)
    t = lax.broadcasted_iota(I32, u.shape, 0) % seq
    prev1 = jnp.where(t == 0, p1_ref[...], pltpu.roll(u, 1, 0))
    prev2 = jnp.where(t <= 1, p2_ref[...], pltpu.roll(u, 2, 0))
    mix_ref[...] = (b_ref[...].astype(F32) * _conv_taps(u, prev1, prev2, w_ref)).astype(BF16)
    u_ref[...] = u


def _conv_short(z, conv_w, state0, batch, seq):
    m = batch * seq
    zeros = jnp.zeros((batch, seq - 1, D_MODEL), F32)
    p1 = jnp.concatenate([state0[:, 1:2], zeros], axis=1).reshape(m, D_MODEL)
    p2 = jnp.concatenate([state0, zeros[:, 1:]], axis=1).reshape(m, D_MODEL)
    col = lambda j: pl.BlockSpec((m, D_MODEL), lambda i, j=j: (0, j))
    full = pl.BlockSpec((m, D_MODEL), lambda i: (0, 0))
    mix, u = pl.pallas_call(
        functools.partial(_conv_short_kernel, seq=seq),
        grid=(1,),
        in_specs=[col(0), col(1), col(2), _resident((CONV_WIDTH, D_MODEL)), full, full],
        out_specs=[full, full],
        out_shape=[jax.ShapeDtypeStruct((m, D_MODEL), BF16), jax.ShapeDtypeStruct((m, D_MODEL), F32)],
        compiler_params=_cparams(("arbitrary",)),
        name="conv_short",
    )(z, z, z, conv_w, p1, p2)
    return mix, u.reshape(batch, seq, D_MODEL)[:, seq - 2:]


def _xattn_kernel(q_ref, k_ref, v_ref, o_ref, *, nb):
    scale = XA_DIM ** -0.5
    for b in range(nb):
        for h in range(XA_HEADS):
            sl = slice(h * XA_DIM, (h + 1) * XA_DIM)
            q = q_ref[b, :, sl]
            k = k_ref[b, :, h, :].astype(BF16)
            v = v_ref[b, :, h, :].astype(BF16)
            s = lax.dot_general(q, k, NT_DIMS, preferred_element_type=F32) * scale
            p = jnp.exp(s - jnp.max(s, axis=-1, keepdims=True))
            p = p / jnp.sum(p, axis=-1, keepdims=True)
            o_ref[b, :, sl] = jnp.dot(p.astype(BF16), v, preferred_element_type=F32).astype(BF16)


def _xattn(z3, qcol, mem_k, mem_v, layer, nb, tq):
    batch, seq = z3.shape[0], z3.shape[1]
    mem_spec = pl.BlockSpec((None, nb, N_MEM, XA_HEADS, XA_DIM), lambda i, t: (layer, i, 0, 0, 0))
    return pl.pallas_call(
        functools.partial(_xattn_kernel, nb=nb),
        grid=(batch // nb, seq // tq),
        in_specs=[pl.BlockSpec((nb, tq, D_MODEL), lambda i, t: (i, t, qcol)), mem_spec, mem_spec],
        out_specs=pl.BlockSpec((nb, tq, D_MODEL), lambda i, t: (i, t, 0)),
        out_shape=jax.ShapeDtypeStruct((batch, seq, D_MODEL), BF16),
        compiler_params=_cparams(("arbitrary", "arbitrary")),
        name="xattn",
    )(z3, mem_k, mem_v)


def _rotary(x, cos, sin):
    half = x.shape[-1] // 2
    x1, x2 = x[:, :half], x[:, half:]
    return jnp.concatenate([x1 * cos - x2 * sin, x1 * sin + x2 * cos], axis=-1)


def _group_norm_gate(o, gate, gn):
    mu = jnp.mean(o, axis=-1, keepdims=True)
    var = jnp.mean(jnp.square(o - mu), axis=-1, keepdims=True)
    return _silu(gate) * ((o - mu) * lax.rsqrt(var + EPS) * gn)


def _ret_long_kernel(full_ref, q_ref, k_ref, v_ref, g_ref, cos_ref, sin_ref, dec_ref, inn_ref, tail_ref,
                     gn_ref, s0_ref, mix_ref, sfin_ref, s_acc, *, chunk):
    h = pl.program_id(1)
    t = pl.program_id(2)

    @pl.when(t == 0)
    def _():
        s_acc[...] = s0_ref[0, 0]

    cos, sin = cos_ref[...], sin_ref[...]
    q = _rotary(q_ref[0].astype(F32), cos, sin)
    k = _rotary(k_ref[0].astype(F32), cos, sin) * (RET_DK ** -0.5)
    dec = dec_ref[0]
    inn = inn_ref[0][:, 0:1]
    tail = tail_ref[0][:, 0:1]
    full = full_ref[h]
    gn = gn_ref[...]
    for c in range(q.shape[0] // chunk):
        sl = slice(c * chunk, (c + 1) * chunk)
        qc = q[sl].astype(BF16)
        kc = k[sl]
        vc = v_ref[0, sl, :]
        s_old = s_acc[...]
        scores = lax.dot_general(qc, kc.astype(BF16), NT_DIMS, preferred_element_type=F32) * dec
        o = jnp.dot(scores.astype(BF16), vc, preferred_element_type=F32)
        o = o + jnp.dot(qc, s_old.astype(BF16), preferred_element_type=F32) * inn
        s_acc[...] = full * s_old + lax.dot_general((kc * tail).astype(BF16), vc, TN_DIMS,
                                                    preferred_element_type=F32)
        mix_ref[0, sl, :] = _group_norm_gate(o, g_ref[0, sl, :].astype(F32), gn).astype(BF16)

    @pl.when(t == pl.num_programs(2) - 1)
    def _():
        sfin_ref[0, 0] = s_acc[...]


def _ret_tables(pos, chunk):
    half = RET_DK // 2
    inv = ROPE_BASE ** (-jnp.arange(half, dtype=F32) / half)
    ang = pos.astype(F32)[:, None] * inv[None, :]
    log_g = jnp.log1p(-jnp.exp2(-5.0 - jnp.arange(RET_HEADS, dtype=F32)))
    idx = jnp.arange(chunk)
    diff = idx[:, None] - idx[None, :]
    decay = jnp.where(diff[None] >= 0,
                      jnp.exp(jnp.maximum(diff, 0)[None].astype(F32) * log_g[:, None, None]), 0.0)
    inner = jnp.exp((idx + 1).astype(F32)[None, :] * log_g[:, None])
    tail = jnp.exp((chunk - 1 - idx).astype(F32)[None, :] * log_g[:, None])
    full = jnp.exp(chunk * log_g)
    return jnp.cos(ang), jnp.sin(ang), decay, inner, tail, full


def _ret_long(z3, state0, gn, pos):
    batch, seq = z3.shape[0], z3.shape[1]
    chunk = RET_CHUNK
    tc = min(512, seq)
    cos, sin, decay, inner, tail, full = _ret_tables(pos, chunk)
    lane_b = lambda a: jnp.broadcast_to(a[:, :, None], (RET_HEADS, chunk, LANES))
    kq = RET_HEADS * RET_DK // RET_DK
    vq = 2 * RET_HEADS * RET_DK // RET_DV
    gq = vq + RET_HEADS
    tab = pl.BlockSpec((1, chunk, LANES), lambda b, h, t: (h, 0, 0))
    return pl.pallas_call(
        functools.partial(_ret_long_kernel, chunk=chunk),
        grid=(batch, RET_HEADS, seq // tc),
        in_specs=[
            pl.BlockSpec(memory_space=pltpu.SMEM),
            pl.BlockSpec((1, tc, RET_DK), lambda b, h, t: (b, t, h)),
            pl.BlockSpec((1, tc, RET_DK), lambda b, h, t: (b, t, kq + h)),
            pl.BlockSpec((1, tc, RET_DV), lambda b, h, t: (b, t, vq + h)),
            pl.BlockSpec((1, tc, RET_DV), lambda b, h, t: (b, t, gq + h)),
            pl.BlockSpec((tc, RET_DK // 2), lambda b, h, t: (t, 0)),
            pl.BlockSpec((tc, RET_DK // 2), lambda b, h, t: (t, 0)),
            pl.BlockSpec((1, chunk, chunk), lambda b, h, t: (h, 0, 0)),
            tab, tab,
            pl.BlockSpec((1, RET_DV), lambda b, h, t: (0, h)),
            pl.BlockSpec((1, 1, RET_DK, RET_DV), lambda b, h, t: (b, h, 0, 0)),
        ],
        out_specs=[
            pl.BlockSpec((1, tc, RET_DV), lambda b, h, t: (b, t, h)),
            pl.BlockSpec((1, 1, RET_DK, RET_DV), lambda b, h, t: (b, h, 0, 0)),
        ],
        out_shape=[jax.ShapeDtypeStruct((batch, seq, RET_HEADS * RET_DV), BF16),
                   jax.ShapeDtypeStruct((batch, RET_HEADS, RET_DK, RET_DV), F32)],
        scratch_shapes=[pltpu.VMEM((RET_DK, RET_DV), F32)],
        compiler_params=_cparams(("arbitrary", "arbitrary", "arbitrary")),
        name="ret_long",
    )(full, z3, z3, z3, z3, cos, sin, decay, lane_b(inner), lane_b(tail), gn.reshape(1, -1), state0)


def _ret_short_kernel(full_ref, q_ref, k_ref, v_ref, g_ref, cos_ref, sin_ref, dec_ref, inn_ref, tail_ref,
                      gn_ref, s0_ref, mix_ref, snew_ref, *, nb, seq):
    def pad_rows(a):
        return jnp.concatenate([a, jnp.zeros((SHORT_ROWS - seq, a.shape[1]), F32)], axis=0)

    cos, sin = cos_ref[...], sin_ref[...]
    for b in range(nb):
        for h in range(RET_HEADS):
            qs = slice(h * RET_DK, (h + 1) * RET_DK)
            vs = slice(h * RET_DV, (h + 1) * RET_DV)
            q = pad_rows(_rotary(q_ref[b, :, qs].astype(F32), cos, sin)).astype(BF16)
            k = pad_rows(_rotary(k_ref[b, :, qs].astype(F32), cos, sin) * (RET_DK ** -0.5))
            v = pad_rows(v_ref[b, :, vs].astype(F32)).astype(BF16)
            s_old = s0_ref[b, h]
            scores = lax.dot_general(q, k.astype(BF16), NT_DIMS, preferred_element_type=F32) * dec_ref[h]
            o = jnp.dot(scores.astype(BF16), v, preferred_element_type=F32)
            o = o + jnp.dot(q, s_old.astype(BF16), preferred_element_type=F32) * inn_ref[h]
            snew_ref[b, h] = full_ref[h] * s_old + lax.dot_general((k * tail_ref[h]).astype(BF16), v, TN_DIMS,
                                                                   preferred_element_type=F32)
            mix_ref[b, :, vs] = _group_norm_gate(o[:seq], g_ref[b, :, vs].astype(F32), gn_ref[:, vs]).astype(BF16)


def _ret_short(z3, state0, gn, pos, nb=2):
    batch, seq = z3.shape[0], z3.shape[1]
    cos, sin, decay, inner, tail, full = _ret_tables(pos, seq)
    qw = RET_HEADS * RET_DK
    vw = RET_HEADS * RET_DV
    small = lambda a: _resident(a.shape)
    rp = SHORT_ROWS - seq
    decay = jnp.pad(decay, ((0, 0), (0, rp), (0, rp)))
    inner3 = jnp.pad(inner, ((0, 0), (0, rp)))[:, :, None]
    tail3 = jnp.pad(tail, ((0, 0), (0, rp)))[:, :, None]
    return pl.pallas_call(
        functools.partial(_ret_short_kernel, nb=nb, seq=seq),
        grid=(batch // nb,),
        in_specs=[
            pl.BlockSpec(memory_space=pltpu.SMEM),
            pl.BlockSpec((nb, seq, qw), lambda i: (i, 0, 0)),
            pl.BlockSpec((nb, seq, qw), lambda i: (i, 0, 1)),
            pl.BlockSpec((nb, seq, vw), lambda i: (i, 0, 1)),
            pl.BlockSpec((nb, seq, vw), lambda i: (i, 0, 2)),
            small(cos), small(sin), small(decay), small(inner3), small(tail3),
            _resident((1, vw)),
            pl.BlockSpec((nb, RET_HEADS, RET_DK, RET_DV), lambda i: (i, 0, 0, 0)),
        ],
        out_specs=[
            pl.BlockSpec((nb, seq, vw), lambda i: (i, 0, 0)),
            pl.BlockSpec((nb, RET_HEADS, RET_DK, RET_DV), lambda i: (i, 0, 0, 0)),
        ],
        out_shape=[jax.ShapeDtypeStruct((batch, seq, vw), BF16),
                   jax.ShapeDtypeStruct((batch, RET_HEADS, RET_DK, RET_DV), F32)],
        compiler_params=_cparams(("arbitrary",)),
        name="ret_short",
    )(full, z3, z3, z3, z3, cos, sin, decay, inner3, tail3, gn.reshape(1, -1), state0)


def _route(logits):
    lane = lax.broadcasted_iota(I32, logits.shape, 1)
    neg = jnp.float32(-1e30)
    big = jnp.int32(LANES)
    is_g = lane < N_GROUPS
    lg = jnp.where(is_g, logits, neg)
    gmax = jnp.max(lg, axis=-1, keepdims=True)
    gidx = jnp.min(jnp.where(lg == gmax, lane, big), axis=-1, keepdims=True)
    pg_sel = 1.0 / jnp.sum(jnp.where(is_g, jnp.exp(lg - gmax), 0.0), axis=-1, keepdims=True)
    lo = N_GROUPS + EXPERTS_PER_GROUP * gidx
    in_grp = (lane >= lo) & (lane < lo + EXPERTS_PER_GROUP)
    sel = jnp.where(in_grp, logits, neg)
    e = jnp.where(in_grp, jnp.exp(sel - jnp.max(sel, axis=-1, keepdims=True)), 0.0)
    pe = jnp.where(in_grp, e / jnp.sum(e, axis=-1, keepdims=True), -1.0)
    t1 = jnp.max(pe, axis=-1, keepdims=True)
    i1 = jnp.min(jnp.where(pe == t1, lane, big), axis=-1, keepdims=True)
    pe2 = jnp.where(lane == i1, -1.0, pe)
    t2 = jnp.max(pe2, axis=-1, keepdims=True)
    i2 = jnp.min(jnp.where(pe2 == t2, lane, big), axis=-1, keepdims=True)
    den = t1 + t2
    gates = jnp.where(lane == i1, t1 / den * pg_sel, 0.0) + jnp.where(lane == i2, t2 / den * pg_sel, 0.0)
    out = jnp.zeros_like(gates)
    for g in range(N_GROUPS):
        shifted = pltpu.roll(gates, LANES - (N_GROUPS + EXPERTS_PER_GROUP * g), 1)
        out = jnp.where((gidx == g) & (lane < EXPERTS_PER_GROUP), shifted, out)
    return jnp.where(lane == GID_LANE, gidx.astype(F32), out)


def _outproj_kernel(x_ref, mix_ref, xa_ref, wa_ref, wb_ref, g_ref, wr_ref, o_ref, *, sub):
    for s in range(x_ref.shape[0] // sub):
        rows = slice(s * sub, (s + 1) * sub)
        acc = jnp.dot(mix_ref[rows, :], wa_ref[...], preferred_element_type=F32)
        acc = acc + jnp.dot(xa_ref[rows, :], wb_ref[...], preferred_element_type=F32)
        xn = x_ref[rows, :] + acc
        h2 = _rms(xn, g_ref[...])
        hi = h2.astype(BF16)
        lo = (h2 - hi.astype(F32)).astype(BF16)
        p_hi = jnp.dot(hi, wr_ref[...], preferred_element_type=F32)
        p_lo = jnp.dot(lo, wr_ref[:, :ROUTER_W], preferred_element_type=F32)
        logits = (p_hi[:, :ROUTER_W] + p_lo) + p_hi[:, ROUTER_W:]
        o_ref[rows, :D_MODEL] = xn
        o_ref[rows, D_MODEL:] = _route(logits)


def _outproj_router(x, mix, xa, w_out, g_ffn, w_router2):
    m, cm = mix.shape
    tm = min(512, m)
    row = lambda w: pl.BlockSpec((tm, w), lambda i: (i, 0))
    return pl.pallas_call(
        functools.partial(_outproj_kernel, sub=min(128, tm)),
        grid=(m // tm,),
        in_specs=[row(D_MODEL), row(cm), row(D_MODEL), _resident((cm, D_MODEL)), _resident((D_MODEL, D_MODEL)),
                  _resident((1, D_MODEL)), _resident((D_MODEL, 2 * ROUTER_W))],
        out_specs=row(AUG_W),
        out_shape=jax.ShapeDtypeStruct((m, AUG_W), F32),
        compiler_params=_cparams(("arbitrary",)),
        name="outproj_router",
    )(x, mix, xa, w_out[:cm], w_out[cm:], g_ffn.reshape(1, D_MODEL), w_router2)


STEP_VALID, STEP_FIRST, STEP_LAST = 1, 2, 4


def _moe_plan(gid, tm):
    m = gid.shape[0]
    ntile = m // tm
    nstep = ntile + N_GROUPS - 1
    onehot = (gid[:, None] == jnp.arange(N_GROUPS, dtype=I32)[None, :]).astype(I32)
    csum = jnp.cumsum(onehot, axis=0)
    rank = jnp.sum(csum * onehot, axis=1) - 1
    counts = csum[-1]
    gend = jnp.cumsum(counts)
    gstart = gend - counts
    pos = gstart[gid] + rank
    row_tok = jnp.zeros((m,), I32).at[pos].set(jnp.arange(m, dtype=I32))
    lo = jnp.arange(ntile, dtype=I32)[:, None] * tm
    present = (gstart[None, :] < lo + tm) & (gend[None, :] > lo) & (counts[None, :] > 0)
    idx = jnp.nonzero(present.reshape(-1), size=nstep, fill_value=-1)[0].astype(I32)
    valid = idx >= 0
    idx = jnp.where(valid, idx, jnp.max(idx))
    st_tile = idx // N_GROUPS
    st_gid = idx % N_GROUPS
    prev_tile = jnp.concatenate([jnp.full((1,), -1, I32), st_tile[:-1]])
    next_tile = jnp.concatenate([st_tile[1:], jnp.full((1,), -1, I32)])
    next_valid = jnp.concatenate([valid[1:], jnp.zeros((1,), bool)])
    first = valid & (st_tile != prev_tile)
    last = valid & ((st_tile != next_tile) | ~next_valid)
    flags = valid * STEP_VALID + first * STEP_FIRST + last * STEP_LAST
    return st_tile, st_gid, flags.astype(I32), row_tok


def _moe_kernel(tile_ref, gid_ref, flag_ref, tok_ref, xaug_hbm, g_ref, fin_ref, wgu_ref, wd_ref, out_hbm,
                xbuf, ybuf, gsem, ssem, *, tm, ntile, final_norm):
    s = pl.program_id(0)
    tile = tile_ref[s]
    grp = gid_ref[s]
    flags = flag_ref[s]
    valid = (flags & STEP_VALID) > 0
    first = (flags & STEP_FIRST) > 0
    last = (flags & STEP_LAST) > 0
    slot = tile % 2

    def gather_start(t, sl):
        def body(r, c):
            tok = tok_ref[t * tm + r]
            pltpu.make_async_copy(xaug_hbm.at[pl.ds(tok, 1)], xbuf.at[sl, pl.ds(r, 1)], gsem.at[sl]).start()
            return c
        lax.fori_loop(0, tm, body, 0, unroll=8)

    def scatter_start(t, sl):
        def body(r, c):
            tok = tok_ref[t * tm + r]
            pltpu.make_async_copy(ybuf.at[sl, pl.ds(r, 1)], out_hbm.at[pl.ds(tok, 1)], ssem.at[sl]).start()
            return c
        lax.fori_loop(0, tm, body, 0, unroll=8)

    def gather_wait(sl):
        pltpu.make_async_copy(xaug_hbm.at[pl.ds(0, tm)], xbuf.at[sl], gsem.at[sl]).wait()

    def scatter_wait(sl):
        pltpu.make_async_copy(ybuf.at[sl], out_hbm.at[pl.ds(0, tm)], ssem.at[sl]).wait()

    @pl.when(s == 0)
    def _():
        gather_start(0, 0)

    @pl.when(first & (tile + 1 < ntile))
    def _():
        gather_start(tile + 1, 1 - slot)

    @pl.when(first)
    def _():
        gather_wait(slot)

        @pl.when(tile >= 2)
        def _():
            scatter_wait(slot)

        ybuf[slot] = xbuf[slot, :, 0:D_MODEL]

    @pl.when(valid)
    def _():
        xn = xbuf[slot, :, 0:D_MODEL]
        gl = xbuf[slot, :, GATE_LANE0:AUG_W]
        mine = gl[:, GID_LANE:GID_LANE + 1] == grp.astype(F32)
        h2 = _rms(xn, g_ref[...]).astype(BF16)
        acc = jnp.zeros((tm, D_MODEL), F32)
        for j in range(EXPERTS_PER_GROUP):
            gu = jnp.dot(h2, wgu_ref[j], preferred_element_type=F32)
            gate = jnp.where(mine, gl[:, j:j + 1], 0.0)
            act = _silu(gu[:, :EXPERT_FF]) * gu[:, EXPERT_FF:] * gate
            acc = acc + jnp.dot(act.astype(BF16), wd_ref[j], preferred_element_type=F32)
        ybuf[slot] = ybuf[slot] + acc

    @pl.when(last)
    def _():
        if final_norm:
            ybuf[slot] = _rms(ybuf[slot], fin_ref[...])
        scatter_start(tile, slot)

        @pl.when(tile == ntile - 1)
        def _():
            if ntile >= 2:
                scatter_wait(1 - slot)
            scatter_wait(slot)


def _moe(xaug, g_ffn, final_g, w_gu, w_down, final_norm):
    m = xaug.shape[0]
    tm = min(MOE_TM, m)
    gid = xaug[:, GATE_LANE0 + GID_LANE].astype(I32)
    st_tile, st_gid, flags, row_tok = _moe_plan(gid, tm)
    epg = EXPERTS_PER_GROUP
    grid_spec = pltpu.PrefetchScalarGridSpec(
        num_scalar_prefetch=4,
        grid=(st_tile.shape[0],),
        in_specs=[
            pl.BlockSpec(memory_space=pl.ANY),
            pl.BlockSpec((1, D_MODEL), lambda s, *_: (0, 0)),
            pl.BlockSpec((1, D_MODEL), lambda s, *_: (0, 0)),
            pl.BlockSpec((epg, D_MODEL, 2 * EXPERT_FF), lambda s, tl, gd, fl, tok: (gd[s], 0, 0)),
            pl.BlockSpec((epg, EXPERT_FF, D_MODEL), lambda s, tl, gd, fl, tok: (gd[s], 0, 0)),
        ],
        out_specs=pl.BlockSpec(memory_space=pl.ANY),
        scratch_shapes=[
            pltpu.VMEM((2, tm, AUG_W), F32),
            pltpu.VMEM((2, tm, D_MODEL), F32),
            pltpu.SemaphoreType.DMA((2,)),
            pltpu.SemaphoreType.DMA((2,)),
        ],
    )
    return pl.pallas_call(
        functools.partial(_moe_kernel, tm=tm, ntile=m // tm, final_norm=final_norm),
        grid_spec=grid_spec,
        out_shape=jax.ShapeDtypeStruct((m, D_MODEL), F32),
        compiler_params=_cparams(("arbitrary",)),
        name="moe",
    )(st_tile, st_gid, flags, row_tok, xaug, g_ffn.reshape(1, D_MODEL), final_g.reshape(1, D_MODEL), w_gu, w_down)


def _trunk(x, pos, conv_state, ret_state, mem_k, mem_v, wts, long_seq):
    batch, seq = x.shape[0], x.shape[1]
    m = batch * seq
    x2 = x.reshape(m, D_MODEL)
    if long_seq:
        nb, tq = 1, min(512, seq)
    else:
        nb, tq = 4, seq
    new_conv = new_ret = None
    for i in range(DEPTH):
        if i % 2 == 0:
            z = _inproj(x2, wts["norm_mix"][i], wts["conv_w_in"])
            conv = _conv_long if long_seq else _conv_short
            mix, new_conv = conv(z, wts["conv_w"], conv_state, batch, seq)
            qcol = 3
            w_out = wts["conv_w_out"]
        else:
            z = _inproj(x2, wts["norm_mix"][i], wts["ret_w_in"])
            ret = _ret_long if long_seq else _ret_short
            mix, new_ret = ret(z.reshape(batch, seq, -1), ret_state, wts["ret_gn"], pos)
            mix = mix.reshape(m, -1)
            qcol = 6
            w_out = wts["ret_w_out"]
        xa = _xattn(z.reshape(batch, seq, -1), qcol, mem_k, mem_v, i, nb, tq).reshape(m, D_MODEL)
        xaug = _outproj_router(x2, mix, xa, w_out, wts["norm_ffn"][i], wts["w_router"][i])
        x2 = _moe(xaug, wts["norm_ffn"][i], wts["final_norm"], wts["moe_w_gate_up"][i], wts["moe_w_down"][i],
                  final_norm=(i == DEPTH - 1))
    return x2.reshape(batch, seq, D_MODEL), new_conv, new_ret


def kernel(x_prompt, x_sample, state_conv, state_ret, cache_mem_k, cache_mem_v, mem_prompt, norm_mix, norm_ffn,
           mem_norm, w_mem_kv, conv_w_in, conv_w, conv_w_out, ret_w_in, ret_gn, ret_w_out, moe_w_group,
           moe_w_router, moe_w_gate_up, moe_w_down, final_norm):
    batch, seq = x_prompt.shape[0], x_prompt.shape[1]
    dec_batch, dec_seq = x_sample.shape[0], x_sample.shape[1]
    w_router = jnp.concatenate(
        [moe_w_group, moe_w_router, jnp.zeros((DEPTH, D_MODEL, ROUTER_W - N_GROUPS - N_EXPERTS), F32)], axis=-1)
    w_router_hi = w_router.astype(BF16)
    w_router = jnp.concatenate([w_router_hi, (w_router - w_router_hi.astype(F32)).astype(BF16)], axis=-1)
    wts = dict(
        norm_mix=norm_mix, norm_ffn=norm_ffn, final_norm=final_norm, w_router=w_router,
        conv_w_in=conv_w_in[0].astype(BF16), conv_w=conv_w[0], conv_w_out=conv_w_out[0].astype(BF16),
        ret_w_in=ret_w_in[0].astype(BF16), ret_gn=ret_gn[0], ret_w_out=ret_w_out[0].astype(BF16),
        moe_w_gate_up=moe_w_gate_up.astype(BF16), moe_w_down=moe_w_down.astype(BF16),
    )
    mem_k_prompt, mem_v_prompt = _mem_kv(mem_prompt.reshape(batch * N_MEM, D_MODEL), mem_norm, w_mem_kv.astype(BF16))

    y_prompt, conv_p, ret_p = _trunk(
        x_prompt, jnp.arange(seq, dtype=I32),
        jnp.zeros((batch, CONV_WIDTH - 1, D_MODEL), F32), jnp.zeros((batch, RET_HEADS, RET_DK, RET_DV), F32),
        mem_k_prompt, mem_v_prompt, wts, True)
    y_sample, conv_s, ret_s = _trunk(
        x_sample, PAST_LEN + jnp.arange(dec_seq, dtype=I32), state_conv[0], state_ret[0],
        cache_mem_k, cache_mem_v, wts, False)
    return (y_prompt, y_sample, conv_p[None], conv_s[None], ret_p[None], ret_s[None], mem_k_prompt, mem_v_prompt)
```

```python
import functools

import jax
import jax.numpy as jnp
from jax import lax
from jax.experimental import pallas as pl
from jax.experimental.pallas import tpu as pltpu

F32 = jnp.float32
BF16 = jnp.bfloat16
I32 = jnp.int32

D_MODEL = 1024
DEPTH = 2
CONV_WIDTH = 3
RET_HEADS = 4
RET_DK = 256
RET_DV = 512
RET_CHUNK = 128
ROPE_BASE = 10000.0
N_MEM = 256
XA_HEADS = 4
XA_DIM = 256
N_GROUPS = 4
EXPERTS_PER_GROUP = 8
N_EXPERTS = N_GROUPS * EXPERTS_PER_GROUP
EXPERT_FF = 256
EPS = 1e-6

LANES = 128
AUG_W = D_MODEL + LANES
GATE_LANE0 = D_MODEL
GID_LANE = EXPERTS_PER_GROUP
ROUTER_W = LANES
VMEM_LIMIT = 56 * 1024 * 1024
MOE_TM = 256
SHORT_ROWS = 8
PAST_LEN = 16384
NT_DIMS = (((1,), (1,)), ((), ()))
TN_DIMS = (((0,), (0,)), ((), ()))


def _cparams(sem):
    return pltpu.CompilerParams(dimension_semantics=sem, vmem_limit_bytes=VMEM_LIMIT)


def _rms(x, g):
    return x * lax.rsqrt(jnp.mean(x * x, axis=-1, keepdims=True) + EPS) * g


def _silu(x):
    return x * jax.nn.sigmoid(x)


def _resident(shape):
    nd = len(shape)
    return pl.BlockSpec(shape, lambda *_: (0,) * nd, pipeline_mode=pl.Buffered(1))


def _memkv_kernel(x_ref, g_ref, w_ref, k_ref, v_ref, *, nb):
    h = _rms(x_ref[...], g_ref[...]).astype(BF16)
    kv = jnp.dot(h, w_ref[...], preferred_element_type=F32)
    for b in range(nb):
        rows = slice(b * N_MEM, (b + 1) * N_MEM)
        for hd in range(XA_HEADS):
            k_ref[b, :, hd, :] = kv[rows, hd * XA_DIM:(hd + 1) * XA_DIM]
            v_ref[b, :, hd, :] = kv[rows, D_MODEL + hd * XA_DIM:D_MODEL + (hd + 1) * XA_DIM]


def _mem_kv(mem2d, mem_norm, w_kv):
    batch = mem2d.shape[0] // N_MEM
    nb = min(2, batch)
    tm = nb * N_MEM
    out = jax.ShapeDtypeStruct((DEPTH, batch, N_MEM, XA_HEADS, XA_DIM), F32)
    out_spec = pl.BlockSpec((None, nb, N_MEM, XA_HEADS, XA_DIM), lambda l, j: (l, j, 0, 0, 0))
    return pl.pallas_call(
        functools.partial(_memkv_kernel, nb=nb),
        grid=(DEPTH, batch // nb),
        in_specs=[
            pl.BlockSpec((tm, D_MODEL), lambda l, j: (j, 0)),
            pl.BlockSpec((None, 1, D_MODEL), lambda l, j: (l, 0, 0)),
            pl.BlockSpec((None, D_MODEL, 2 * D_MODEL), lambda l, j: (l, 0, 0)),
        ],
        out_specs=[out_spec, out_spec],
        out_shape=[out, out],
        compiler_params=_cparams(("arbitrary", "arbitrary")),
        name="mem_kv",
    )(mem2d, mem_norm.reshape(DEPTH, 1, D_MODEL), w_kv)


def _inproj_kernel(x_ref, g_ref, w_ref, o_ref, *, tn):
    h = _rms(x_ref[...], g_ref[...]).astype(BF16)
    for j in range(o_ref.shape[1] // tn):
        sl = slice(j * tn, (j + 1) * tn)
        o_ref[:, sl] = jnp.dot(h, w_ref[:, sl], preferred_element_type=F32).astype(BF16)


def _inproj(x, g, w):
    m, n = x.shape[0], w.shape[1]
    tm = min(512, m)
    return pl.pallas_call(
        functools.partial(_inproj_kernel, tn=512),
        grid=(m // tm,),
        in_specs=[
            pl.BlockSpec((tm, D_MODEL), lambda i: (i, 0)),
            _resident((1, D_MODEL)),
            _resident((D_MODEL, n)),
        ],
        out_specs=pl.BlockSpec((tm, n), lambda i: (i, 0)),
        out_shape=jax.ShapeDtypeStruct((m, n), BF16),
        compiler_params=_cparams(("arbitrary",)),
        name="inproj",
    )(x, g.reshape(1, D_MODEL), w)


def _conv_taps(u, prev1, prev2, w_ref):
    return (w_ref[0:1, :] * prev2 + w_ref[1:2, :] * prev1) + w_ref[2:3, :] * u


def _conv_long_kernel(b_ref, c_ref, v_ref, w_ref, st0_ref, mix_ref, st_ref, carry, *, tiles_per_seq):
    i = pl.program_id(0)
    tt = c_ref.shape[0]

    @pl.when(i % tiles_per_seq == 0)
    def _():
        carry[6:8, :] = st0_ref[0]

    u = c_ref[...].astype(F32) * v_ref[...].astype(F32)
    pm2 = carry[6:7, :]
    pm1 = carry[7:8, :]
    row = lax.broadcasted_iota(I32, u.shape, 0)
    prev1 = jnp.where(row == 0, pm1, pltpu.roll(u, 1, 0))
    prev2 = jnp.where(row == 0, pm2, jnp.where(row == 1, pm1, pltpu.roll(u, 2, 0)))
    mix_ref[...] = (b_ref[...].astype(F32) * _conv_taps(u, prev1, prev2, w_ref)).astype(BF16)
    carry[...] = u[tt - 8:tt, :]
    st_ref[0] = carry[6:8, :]


def _conv_long(z, conv_w, state0, batch, seq):
    tt = min(512, seq)
    tps = seq // tt
    col = lambda j: pl.BlockSpec((tt, D_MODEL), lambda i, j=j: (i, j))
    return pl.pallas_call(
        functools.partial(_conv_long_kernel, tiles_per_seq=tps),
        grid=(batch * tps,),
        in_specs=[col(0), col(1), col(2), _resident((CONV_WIDTH, D_MODEL)),
                  pl.BlockSpec((1, 2, D_MODEL), lambda i: (i // tps, 0, 0))],
        out_specs=[pl.BlockSpec((tt, D_MODEL), lambda i: (i, 0)),
                   pl.BlockSpec((1, 2, D_MODEL), lambda i: (i // tps, 0, 0))],
        out_shape=[jax.ShapeDtypeStruct((batch * seq, D_MODEL), BF16),
                   jax.ShapeDtypeStruct((batch, 2, D_MODEL), F32)],
        scratch_shapes=[pltpu.VMEM((8, D_MODEL), F32)],
        compiler_params=_cparams(("arbitrary",)),
        name="conv_long",
    )(z, z, z, conv_w, state0)


def _conv_short_kernel(b_ref, c_ref, v_ref, w_ref, p1_ref, p2_ref, mix_ref, u_ref, *, seq):
    u = c_ref[...].astype(F32) * v_ref[...].astype(F32)
    t = lax.broadcasted_iota(I32, u.shape, 0) % seq
    prev1 = jnp.where(t == 0, p1_ref[...], pltpu.roll(u, 1, 0))
    prev2 = jnp.where(t <= 1, p2_ref[...], pltpu.roll(u, 2, 0))
    mix_ref[...] = (b_ref[...].astype(F32) * _conv_taps(u, prev1, prev2, w_ref)).astype(BF16)
    u_ref[...] = u


def _conv_short(z, conv_w, state0, batch, seq):
    m = batch * seq
    zeros = jnp.zeros((batch, seq - 1, D_MODEL), F32)
    p1 = jnp.concatenate([state0[:, 1:2], zeros], axis=1).reshape(m, D_MODEL)
    p2 = jnp.concatenate([state0, zeros[:, 1:]], axis=1).reshape(m, D_MODEL)
    col = lambda j: pl.BlockSpec((m, D_MODEL), lambda i, j=j: (0, j))
    full = pl.BlockSpec((m, D_MODEL), lambda i: (0, 0))
    mix, u = pl.pallas_call(
        functools.partial(_conv_short_kernel, seq=seq),
        grid=(1,),
        in_specs=[col(0), col(1), col(2), _resident((CONV_WIDTH, D_MODEL)), full, full],
        out_specs=[full, full],
        out_shape=[jax.ShapeDtypeStruct((m, D_MODEL), BF16), jax.ShapeDtypeStruct((m, D_MODEL), F32)],
        compiler_params=_cparams(("arbitrary",)),
        name="conv_short",
    )(z, z, z, conv_w, p1, p2)
    return mix, u.reshape(batch, seq, D_MODEL)[:, seq - 2:]


def _xattn_kernel(q_ref, k_hbm, v_hbm, o_ref, kbuf, vbuf, sem, *, nb, layer, n_bt):
    i = pl.program_id(0)
    t = pl.program_id(1)
    slot = i % 2

    def copies(bt, sl):
        out = []
        for b in range(nb):
            for h in range(XA_HEADS):
                out.append(pltpu.make_async_copy(k_hbm.at[layer, bt * nb + b, :, h, :], kbuf.at[sl, b, h], sem.at[0, sl]))
                out.append(pltpu.make_async_copy(v_hbm.at[layer, bt * nb + b, :, h, :], vbuf.at[sl, b, h], sem.at[1, sl]))
        return out

    @pl.when((i == 0) & (t == 0))
    def _():
        for c in copies(0, 0):
            c.start()

    @pl.when(t == 0)
    def _():
        @pl.when(i + 1 < n_bt)
        def _():
            for c in copies(i + 1, 1 - slot):
                c.start()

        for c in copies(i, slot):
            c.wait()

    scale = XA_DIM ** -0.5
    tq = q_ref.shape[1]
    rows = -(-tq // SHORT_ROWS) * SHORT_ROWS
    pairs = [(b, h) for b in range(nb) for h in range(XA_HEADS)]
    scores = []
    for b, h in pairs:
        q = q_ref[b, :, h * XA_DIM:(h + 1) * XA_DIM]
        if rows != tq:
            q = jnp.concatenate([q.astype(F32), jnp.zeros((rows - tq, XA_DIM), F32)], axis=0).astype(BF16)
        k = kbuf[slot, b, h].astype(BF16)
        scores.append(lax.dot_general(q, k, NT_DIMS, preferred_element_type=F32) * scale)
    s = jnp.concatenate(scores, axis=0)
    p = jnp.exp(s - jnp.max(s, axis=-1, keepdims=True))
    p = (p / jnp.sum(p, axis=-1, keepdims=True)).astype(BF16)
    for n, (b, h) in enumerate(pairs):
        v = vbuf[slot, b, h].astype(BF16)
        o = jnp.dot(p[n * rows:(n + 1) * rows], v, preferred_element_type=F32)
        o_ref[b, :, h * XA_DIM:(h + 1) * XA_DIM] = o[:tq].astype(BF16)


def _xattn(z3, qcol, mem_k, mem_v, layer, nb, tq):
    batch, seq = z3.shape[0], z3.shape[1]
    n_bt = batch // nb
    plane = pltpu.VMEM((2, nb, XA_HEADS, N_MEM, XA_DIM), F32)
    return pl.pallas_call(
        functools.partial(_xattn_kernel, nb=nb, layer=layer, n_bt=n_bt),
        grid=(n_bt, seq // tq),
        in_specs=[pl.BlockSpec((nb, tq, D_MODEL), lambda i, t: (i, t, qcol)),
                  pl.BlockSpec(memory_space=pl.ANY), pl.BlockSpec(memory_space=pl.ANY)],
        out_specs=pl.BlockSpec((nb, tq, D_MODEL), lambda i, t: (i, t, 0)),
        out_shape=jax.ShapeDtypeStruct((batch, seq, D_MODEL), BF16),
        scratch_shapes=[plane, plane, pltpu.SemaphoreType.DMA((2, 2))],
        compiler_params=_cparams(("arbitrary", "arbitrary")),
        name="xattn",
    )(z3, mem_k, mem_v)


def _rotary(x, cos, sin):
    half = x.shape[-1] // 2
    x1, x2 = x[:, :half], x[:, half:]
    return jnp.concatenate([x1 * cos - x2 * sin, x1 * sin + x2 * cos], axis=-1)


def _group_norm_gate(o, gate, gn):
    mu = jnp.mean(o, axis=-1, keepdims=True)
    var = jnp.mean(jnp.square(o - mu), axis=-1, keepdims=True)
    return _silu(gate) * ((o - mu) * lax.rsqrt(var + EPS) * gn)


def _ret_long_kernel(full_ref, q_ref, k_ref, v_ref, g_ref, cos_ref, sin_ref, dec_ref, inn_ref, tail_ref,
                     gn_ref, s0_ref, mix_ref, sfin_ref, s_acc, *, chunk):
    h = pl.program_id(1)
    t = pl.program_id(2)

    @pl.when(t == 0)
    def _():
        s_acc[...] = s0_ref[0, 0]

    cos, sin = cos_ref[...], sin_ref[...]
    q = _rotary(q_ref[0].astype(F32), cos, sin)
    k = _rotary(k_ref[0].astype(F32), cos, sin) * (RET_DK ** -0.5)
    dec = dec_ref[0]
    inn = inn_ref[0][:, 0:1]
    tail = tail_ref[0][:, 0:1]
    full = full_ref[h]
    gn = gn_ref[...]
    for c in range(q.shape[0] // chunk):
        sl = slice(c * chunk, (c + 1) * chunk)
        qc = q[sl].astype(BF16)
        kc = k[sl]
        vc = v_ref[0, sl, :]
        s_old = s_acc[...]
        scores = lax.dot_general(qc, kc.astype(BF16), NT_DIMS, preferred_element_type=F32) * dec
        o = jnp.dot(scores.astype(BF16), vc, preferred_element_type=F32)
        o = o + jnp.dot(qc, s_old.astype(BF16), preferred_element_type=F32) * inn
        s_acc[...] = full * s_old + lax.dot_general((kc * tail).astype(BF16), vc, TN_DIMS,
                                                    preferred_element_type=F32)
        mix_ref[0, sl, :] = _group_norm_gate(o, g_ref[0, sl, :].astype(F32), gn).astype(BF16)

    @pl.when(t == pl.num_programs(2) - 1)
    def _():
        sfin_ref[0, 0] = s_acc[...]


def _ret_tables(pos, chunk):
    half = RET_DK // 2
    inv = ROPE_BASE ** (-jnp.arange(half, dtype=F32) / half)
    ang = pos.astype(F32)[:, None] * inv[None, :]
    log_g = jnp.log1p(-jnp.exp2(-5.0 - jnp.arange(RET_HEADS, dtype=F32)))
    idx = jnp.arange(chunk)
    diff = idx[:, None] - idx[None, :]
    decay = jnp.where(diff[None] >= 0,
                      jnp.exp(jnp.maximum(diff, 0)[None].astype(F32) * log_g[:, None, None]), 0.0)
    inner = jnp.exp((idx + 1).astype(F32)[None, :] * log_g[:, None])
    tail = jnp.exp((chunk - 1 - idx).astype(F32)[None, :] * log_g[:, None])
    full = jnp.exp(chunk * log_g)
    return jnp.cos(ang), jnp.sin(ang), decay, inner, tail, full


def _ret_long(z3, state0, gn, pos):
    batch, seq = z3.shape[0], z3.shape[1]
    chunk = RET_CHUNK
    tc = min(512, seq)
    cos, sin, decay, inner, tail, full = _ret_tables(pos, chunk)
    lane_b = lambda a: jnp.broadcast_to(a[:, :, None], (RET_HEADS, chunk, LANES))
    kq = RET_HEADS * RET_DK // RET_DK
    vq = 2 * RET_HEADS * RET_DK // RET_DV
    gq = vq + RET_HEADS
    tab = pl.BlockSpec((1, chunk, LANES), lambda b, h, t: (h, 0, 0))
    return pl.pallas_call(
        functools.partial(_ret_long_kernel, chunk=chunk),
        grid=(batch, RET_HEADS, seq // tc),
        in_specs=[
            pl.BlockSpec(memory_space=pltpu.SMEM),
            pl.BlockSpec((1, tc, RET_DK), lambda b, h, t: (b, t, h)),
            pl.BlockSpec((1, tc, RET_DK), lambda b, h, t: (b, t, kq + h)),
            pl.BlockSpec((1, tc, RET_DV), lambda b, h, t: (b, t, vq + h)),
            pl.BlockSpec((1, tc, RET_DV), lambda b, h, t: (b, t, gq + h)),
            pl.BlockSpec((tc, RET_DK // 2), lambda b, h, t: (t, 0)),
            pl.BlockSpec((tc, RET_DK // 2), lambda b, h, t: (t, 0)),
            pl.BlockSpec((1, chunk, chunk), lambda b, h, t: (h, 0, 0)),
            tab, tab,
            pl.BlockSpec((1, RET_DV), lambda b, h, t: (0, h)),
            pl.BlockSpec((1, 1, RET_DK, RET_DV), lambda b, h, t: (b, h, 0, 0)),
        ],
        out_specs=[
            pl.BlockSpec((1, tc, RET_DV), lambda b, h, t: (b, t, h)),
            pl.BlockSpec((1, 1, RET_DK, RET_DV), lambda b, h, t: (b, h, 0, 0)),
        ],
        out_shape=[jax.ShapeDtypeStruct((batch, seq, RET_HEADS * RET_DV), BF16),
                   jax.ShapeDtypeStruct((batch, RET_HEADS, RET_DK, RET_DV), F32)],
        scratch_shapes=[pltpu.VMEM((RET_DK, RET_DV), F32)],
        compiler_params=_cparams(("arbitrary", "arbitrary", "arbitrary")),
        name="ret_long",
    )(full, z3, z3, z3, z3, cos, sin, decay, lane_b(inner), lane_b(tail), gn.reshape(1, -1), state0)


def _ret_short_kernel(full_ref, q_ref, k_ref, v_ref, g_ref, cos_ref, sin_ref, dec_ref, inn_ref, tail_ref,
                      gn_ref, s0_ref, mix_ref, snew_ref, *, nb, seq):
    def pad_rows(a):
        return jnp.concatenate([a, jnp.zeros((SHORT_ROWS - seq, a.shape[1]), F32)], axis=0)

    cos, sin = cos_ref[...], sin_ref[...]
    for b in range(nb):
        for h in range(RET_HEADS):
            qs = slice(h * RET_DK, (h + 1) * RET_DK)
            vs = slice(h * RET_DV, (h + 1) * RET_DV)
            q = pad_rows(_rotary(q_ref[b, :, qs].astype(F32), cos, sin)).astype(BF16)
            k = pad_rows(_rotary(k_ref[b, :, qs].astype(F32), cos, sin) * (RET_DK ** -0.5))
            v = pad_rows(v_ref[b, :, vs].astype(F32)).astype(BF16)
            s_old = s0_ref[b, h]
            scores = lax.dot_general(q, k.astype(BF16), NT_DIMS, preferred_element_type=F32) * dec_ref[h]
            o = jnp.dot(scores.astype(BF16), v, preferred_element_type=F32)
            o = o + jnp.dot(q, s_old.astype(BF16), preferred_element_type=F32) * inn_ref[h]
            snew_ref[b, h] = full_ref[h] * s_old + lax.dot_general((k * tail_ref[h]).astype(BF16), v, TN_DIMS,
                                                                   preferred_element_type=F32)
            mix_ref[b, :, vs] = _group_norm_gate(o[:seq], g_ref[b, :, vs].astype(F32), gn_ref[:, vs]).astype(BF16)


def _ret_short(z3, state0, gn, pos, nb=2):
    batch, seq = z3.shape[0], z3.shape[1]
    cos, sin, decay, inner, tail, full = _ret_tables(pos, seq)
    qw = RET_HEADS * RET_DK
    vw = RET_HEADS * RET_DV
    small = lambda a: _resident(a.shape)
    rp = SHORT_ROWS - seq
    decay = jnp.pad(decay, ((0, 0), (0, rp), (0, rp)))
    inner3 = jnp.pad(inner, ((0, 0), (0, rp)))[:, :, None]
    tail3 = jnp.pad(tail, ((0, 0), (0, rp)))[:, :, None]
    return pl.pallas_call(
        functools.partial(_ret_short_kernel, nb=nb, seq=seq),
        grid=(batch // nb,),
        in_specs=[
            pl.BlockSpec(memory_space=pltpu.SMEM),
            pl.BlockSpec((nb, seq, qw), lambda i: (i, 0, 0)),
            pl.BlockSpec((nb, seq, qw), lambda i: (i, 0, 1)),
            pl.BlockSpec((nb, seq, vw), lambda i: (i, 0, 1)),
            pl.BlockSpec((nb, seq, vw), lambda i: (i, 0, 2)),
            small(cos), small(sin), small(decay), small(inner3), small(tail3),
            _resident((1, vw)),
            pl.BlockSpec((nb, RET_HEADS, RET_DK, RET_DV), lambda i: (i, 0, 0, 0)),
        ],
        out_specs=[
            pl.BlockSpec((nb, seq, vw), lambda i: (i, 0, 0)),
            pl.BlockSpec((nb, RET_HEADS, RET_DK, RET_DV), lambda i: (i, 0, 0, 0)),
        ],
        out_shape=[jax.ShapeDtypeStruct((batch, seq, vw), BF16),
                   jax.ShapeDtypeStruct((batch, RET_HEADS, RET_DK, RET_DV), F32)],
        compiler_params=_cparams(("arbitrary",)),
        name="ret_short",
    )(full, z3, z3, z3, z3, cos, sin, decay, inner3, tail3, gn.reshape(1, -1), state0)


def _route(logits):
    lane = lax.broadcasted_iota(I32, logits.shape, 1).astype(F32)
    neg = jnp.float32(-1e30)
    big = jnp.float32(LANES)
    is_g = lane < N_GROUPS
    lg = jnp.where(is_g, logits, neg)
    gmax = jnp.max(lg, axis=-1, keepdims=True)
    gidx = jnp.min(jnp.where(lg == gmax, lane, big), axis=-1, keepdims=True)
    pg_sel = 1.0 / jnp.sum(jnp.where(is_g, jnp.exp(lg - gmax), 0.0), axis=-1, keepdims=True)
    lo = N_GROUPS + EXPERTS_PER_GROUP * gidx
    in_grp = (lane >= lo) & (lane < lo + EXPERTS_PER_GROUP)
    sel = jnp.where(in_grp, logits, neg)
    e = jnp.where(in_grp, jnp.exp(sel - jnp.max(sel, axis=-1, keepdims=True)), 0.0)
    pe = jnp.where(in_grp, e / jnp.sum(e, axis=-1, keepdims=True), -1.0)
    t1 = jnp.max(pe, axis=-1, keepdims=True)
    i1 = jnp.min(jnp.where(pe == t1, lane, big), axis=-1, keepdims=True)
    pe2 = jnp.where(lane == i1, -1.0, pe)
    t2 = jnp.max(pe2, axis=-1, keepdims=True)
    i2 = jnp.min(jnp.where(pe2 == t2, lane, big), axis=-1, keepdims=True)
    den = t1 + t2
    gates = jnp.where(lane == i1, t1 / den * pg_sel, 0.0) + jnp.where(lane == i2, t2 / den * pg_sel, 0.0)
    out = jnp.zeros_like(gates)
    for g in range(N_GROUPS):
        shifted = pltpu.roll(gates, LANES - (N_GROUPS + EXPERTS_PER_GROUP * g), 1)
        out = jnp.where((gidx == g) & (lane < EXPERTS_PER_GROUP), shifted, out)
    return jnp.where(lane == GID_LANE, gidx, out)


def _outproj_kernel(x_ref, mix_ref, xa_ref, wa_ref, wb_ref, g_ref, wr_ref, o_ref, r_ref, *, sub):
    logits = []
    for s in range(x_ref.shape[0] // sub):
        rows = slice(s * sub, (s + 1) * sub)
        acc = jnp.dot(mix_ref[rows, :], wa_ref[...], preferred_element_type=F32)
        acc = acc + jnp.dot(xa_ref[rows, :], wb_ref[...], preferred_element_type=F32)
        xn = x_ref[rows, :] + acc
        o_ref[rows, :D_MODEL] = xn
        h2 = _rms(xn, g_ref[...])
        hi = h2.astype(BF16)
        lo = (h2 - hi.astype(F32)).astype(BF16)
        p_hi = jnp.dot(hi, wr_ref[...], preferred_element_type=F32)
        p_lo = jnp.dot(lo, wr_ref[:, :ROUTER_W], preferred_element_type=F32)
        logits.append((p_hi[:, :ROUTER_W] + p_lo) + p_hi[:, ROUTER_W:])
    route = _route(jnp.concatenate(logits, axis=0))
    o_ref[:, D_MODEL:] = route
    r_ref[...] = route


def _outproj_router(x, mix, xa, w_out, g_ffn, w_router2):
    m, cm = mix.shape
    tm = min(512, m)
    row = lambda w: pl.BlockSpec((tm, w), lambda i: (i, 0))
    return pl.pallas_call(
        functools.partial(_outproj_kernel, sub=min(128, tm)),
        grid=(m // tm,),
        in_specs=[row(D_MODEL), row(cm), row(D_MODEL), _resident((cm, D_MODEL)), _resident((D_MODEL, D_MODEL)),
                  _resident((1, D_MODEL)), _resident((D_MODEL, 2 * ROUTER_W))],
        out_specs=[row(AUG_W), row(LANES)],
        out_shape=[jax.ShapeDtypeStruct((m, AUG_W), F32), jax.ShapeDtypeStruct((m, LANES), F32)],
        compiler_params=_cparams(("arbitrary",)),
        name="outproj_router",
    )(x, mix, xa, w_out[:cm], w_out[cm:], g_ffn.reshape(1, D_MODEL), w_router2)


STEP_VALID, STEP_FIRST, STEP_LAST = 1, 2, 4


def _moe_plan(gid, tm):
    m = gid.shape[0]
    ntile = m // tm
    nstep = ntile + N_GROUPS - 1
    counts = jnp.sum((gid[:, None] == jnp.arange(N_GROUPS, dtype=I32)[None, :]).astype(I32), axis=0)
    gend = jnp.cumsum(counts)
    gstart = gend - counts
    row_tok = jnp.argsort(gid, stable=True).astype(I32)
    lo = jnp.arange(ntile, dtype=I32)[:, None] * tm
    present = (gstart[None, :] < lo + tm) & (gend[None, :] > lo) & (counts[None, :] > 0)
    idx = jnp.nonzero(present.reshape(-1), size=nstep, fill_value=-1)[0].astype(I32)
    valid = idx >= 0
    idx = jnp.where(valid, idx, jnp.max(idx))
    st_tile = idx // N_GROUPS
    st_gid = idx % N_GROUPS
    prev_tile = jnp.concatenate([jnp.full((1,), -1, I32), st_tile[:-1]])
    next_tile = jnp.concatenate([st_tile[1:], jnp.full((1,), -1, I32)])
    next_valid = jnp.concatenate([valid[1:], jnp.zeros((1,), bool)])
    first = valid & (st_tile != prev_tile)
    last = valid & ((st_tile != next_tile) | ~next_valid)
    flags = valid * STEP_VALID + first * STEP_FIRST + last * STEP_LAST
    return st_tile, st_gid, flags.astype(I32), row_tok


def _moe_kernel(tile_ref, gid_ref, flag_ref, tok_ref, xaug_hbm, g_ref, fin_ref, wgu_ref, wd_ref, out_hbm,
                xbuf, ybuf, gsem, ssem, *, tm, ntile, final_norm):
    s = pl.program_id(0)
    tile = tile_ref[s]
    grp = gid_ref[s]
    flags = flag_ref[s]
    valid = (flags & STEP_VALID) > 0
    first = (flags & STEP_FIRST) > 0
    last = (flags & STEP_LAST) > 0
    slot = tile % 2

    def gather_start(t, sl):
        for r in range(tm):
            tok = tok_ref[t * tm + r]
            pltpu.make_async_copy(xaug_hbm.at[pl.ds(tok, 1)], xbuf.at[sl, pl.ds(r, 1)], gsem.at[sl]).start()

    def scatter_start(t, sl):
        for r in range(tm):
            tok = tok_ref[t * tm + r]
            pltpu.make_async_copy(ybuf.at[sl, pl.ds(r, 1)], out_hbm.at[pl.ds(tok, 1)], ssem.at[sl]).start()

    def gather_wait(sl):
        pltpu.make_async_copy(xaug_hbm.at[pl.ds(0, tm)], xbuf.at[sl], gsem.at[sl]).wait()

    def scatter_wait(sl):
        pltpu.make_async_copy(ybuf.at[sl], out_hbm.at[pl.ds(0, tm)], ssem.at[sl]).wait()

    @pl.when(s == 0)
    def _():
        gather_start(0, 0)

    @pl.when(first & (tile + 1 < ntile))
    def _():
        gather_start(tile + 1, 1 - slot)

    @pl.when(first)
    def _():
        gather_wait(slot)

        @pl.when(tile >= 2)
        def _():
            scatter_wait(slot)

        ybuf[slot] = xbuf[slot, :, 0:D_MODEL]

    @pl.when(valid)
    def _():
        xn = xbuf[slot, :, 0:D_MODEL]
        gl = xbuf[slot, :, GATE_LANE0:AUG_W]
        mine = gl[:, GID_LANE:GID_LANE + 1] == grp.astype(F32)
        h2 = _rms(xn, g_ref[...]).astype(BF16)
        acc = jnp.zeros((tm, D_MODEL), F32)
        for j in range(EXPERTS_PER_GROUP):
            gu = jnp.dot(h2, wgu_ref[j], preferred_element_type=F32)
            gate = jnp.where(mine, gl[:, j:j + 1], 0.0)
            act = _silu(gu[:, :EXPERT_FF]) * gu[:, EXPERT_FF:] * gate
            acc = acc + jnp.dot(act.astype(BF16), wd_ref[j], preferred_element_type=F32)
        ybuf[slot] = ybuf[slot] + acc

    @pl.when(last)
    def _():
        if final_norm:
            ybuf[slot] = _rms(ybuf[slot], fin_ref[...])
        scatter_start(tile, slot)

        @pl.when(tile == ntile - 1)
        def _():
            if ntile >= 2:
                scatter_wait(1 - slot)
            scatter_wait(slot)


def _moe(xaug, route, g_ffn, final_g, w_gu, w_down, final_norm):
    m = xaug.shape[0]
    tm = min(MOE_TM, m)
    gid = route[:, GID_LANE].astype(I32)
    st_tile, st_gid, flags, row_tok = _moe_plan(gid, tm)
    epg = EXPERTS_PER_GROUP
    grid_spec = pltpu.PrefetchScalarGridSpec(
        num_scalar_prefetch=4,
        grid=(st_tile.shape[0],),
        in_specs=[
            pl.BlockSpec(memory_space=pl.ANY),
            pl.BlockSpec((1, D_MODEL), lambda s, *_: (0, 0)),
            pl.BlockSpec((1, D_MODEL), lambda s, *_: (0, 0)),
            pl.BlockSpec((epg, D_MODEL, 2 * EXPERT_FF), lambda s, tl, gd, fl, tok: (gd[s], 0, 0)),
            pl.BlockSpec((epg, EXPERT_FF, D_MODEL), lambda s, tl, gd, fl, tok: (gd[s], 0, 0)),
        ],
        out_specs=pl.BlockSpec(memory_space=pl.ANY),
        scratch_shapes=[
            pltpu.VMEM((2, tm, AUG_W), F32),
            pltpu.VMEM((2, tm, D_MODEL), F32),
            pltpu.SemaphoreType.DMA((2,)),
            pltpu.SemaphoreType.DMA((2,)),
        ],
    )
    return pl.pallas_call(
        functools.partial(_moe_kernel, tm=tm, ntile=m // tm, final_norm=final_norm),
        grid_spec=grid_spec,
        out_shape=jax.ShapeDtypeStruct((m, D_MODEL), F32),
        compiler_params=_cparams(("arbitrary",)),
        name="moe",
    )(st_tile, st_gid, flags, row_tok, xaug, g_ffn.reshape(1, D_MODEL), final_g.reshape(1, D_MODEL), w_gu, w_down)


def _trunk(x, pos, conv_state, ret_state, mem_k, mem_v, wts, long_seq):
    batch, seq = x.shape[0], x.shape[1]
    m = batch * seq
    x2 = x.reshape(m, D_MODEL)
    if long_seq:
        nb, tq = 1, min(512, seq)
    else:
        nb, tq = 4, seq
    new_conv = new_ret = None
    for i in range(DEPTH):
        if i % 2 == 0:
            z = _inproj(x2, wts["norm_mix"][i], wts["conv_w_in"])
            conv = _conv_long if long_seq else _conv_short
            mix, new_conv = conv(z, wts["conv_w"], conv_state, batch, seq)
            qcol = 3
            w_out = wts["conv_w_out"]
        else:
            z = _inproj(x2, wts["norm_mix"][i], wts["ret_w_in"])
            ret = _ret_long if long_seq else _ret_short
            mix, new_ret = ret(z.reshape(batch, seq, -1), ret_state, wts["ret_gn"], pos)
            mix = mix.reshape(m, -1)
            qcol = 6
            w_out = wts["ret_w_out"]
        xa = _xattn(z.reshape(batch, seq, -1), qcol, mem_k, mem_v, i, nb, tq).reshape(m, D_MODEL)
        xaug, route = _outproj_router(x2, mix, xa, w_out, wts["norm_ffn"][i], wts["w_router"][i])
        x2 = _moe(xaug, route, wts["norm_ffn"][i], wts["final_norm"], wts["moe_w_gate_up"][i], wts["moe_w_down"][i],
                  final_norm=(i == DEPTH - 1))
    return x2.reshape(batch, seq, D_MODEL), new_conv, new_ret


def kernel(x_prompt, x_sample, state_conv, state_ret, cache_mem_k, cache_mem_v, mem_prompt, norm_mix, norm_ffn,
           mem_norm, w_mem_kv, conv_w_in, conv_w, conv_w_out, ret_w_in, ret_gn, ret_w_out, moe_w_group,
           moe_w_router, moe_w_gate_up, moe_w_down, final_norm):
    batch, seq = x_prompt.shape[0], x_prompt.shape[1]
    dec_batch, dec_seq = x_sample.shape[0], x_sample.shape[1]
    w_router = jnp.concatenate(
        [moe_w_group, moe_w_router, jnp.zeros((DEPTH, D_MODEL, ROUTER_W - N_GROUPS - N_EXPERTS), F32)], axis=-1)
    w_router_hi = w_router.astype(BF16)
    w_router = jnp.concatenate([w_router_hi, (w_router - w_router_hi.astype(F32)).astype(BF16)], axis=-1)
    wts = dict(
        norm_mix=norm_mix, norm_ffn=norm_ffn, final_norm=final_norm, w_router=w_router,
        conv_w_in=conv_w_in[0].astype(BF16), conv_w=conv_w[0], conv_w_out=conv_w_out[0].astype(BF16),
        ret_w_in=ret_w_in[0].astype(BF16), ret_gn=ret_gn[0], ret_w_out=ret_w_out[0].astype(BF16),
        moe_w_gate_up=moe_w_gate_up.astype(BF16), moe_w_down=moe_w_down.astype(BF16),
    )
    mem_k_prompt, mem_v_prompt = _mem_kv(mem_prompt.reshape(batch * N_MEM, D_MODEL), mem_norm, w_mem_kv.astype(BF16))

    y_prompt, conv_p, ret_p = _trunk(
        x_prompt, jnp.arange(seq, dtype=I32),
        jnp.zeros((batch, CONV_WIDTH - 1, D_MODEL), F32), jnp.zeros((batch, RET_HEADS, RET_DK, RET_DV), F32),
        mem_k_prompt, mem_v_prompt, wts, True)
    y_sample, conv_s, ret_s = _trunk(
        x_sample, PAST_LEN + jnp.arange(dec_seq, dtype=I32), state_conv[0], state_ret[0],
        cache_mem_k, cache_mem_v, wts, False)
    return (y_prompt, y_sample, conv_p[None], conv_s[None], ret_p[None], ret_s[None], mem_k_prompt, mem_v_prompt)
```

```python
import functools

import jax
import jax.numpy as jnp
from jax import lax
from jax.experimental import pallas as pl
from jax.experimental.pallas import tpu as pltpu

F32 = jnp.float32
BF16 = jnp.bfloat16
I32 = jnp.int32

D_MODEL = 1024
DEPTH = 2
CONV_WIDTH = 3
RET_HEADS = 4
RET_DK = 256
RET_DV = 512
RET_CHUNK = 128
ROPE_BASE = 10000.0
N_MEM = 256
XA_HEADS = 4
XA_DIM = 256
N_GROUPS = 4
EXPERTS_PER_GROUP = 8
N_EXPERTS = N_GROUPS * EXPERTS_PER_GROUP
EXPERT_FF = 256
EPS = 1e-6

LANES = 128
AUG_W = D_MODEL + LANES
GATE_LANE0 = D_MODEL
GID_LANE = EXPERTS_PER_GROUP
ROUTER_W = LANES
VMEM_LIMIT = 56 * 1024 * 1024
MOE_TM = 256
SHORT_ROWS = 8
PAST_LEN = 16384
NT_DIMS = (((1,), (1,)), ((), ()))
TN_DIMS = (((0,), (0,)), ((), ()))


def _cparams(sem):
    return pltpu.CompilerParams(dimension_semantics=sem, vmem_limit_bytes=VMEM_LIMIT)


def _rms(x, g):
    return x * lax.rsqrt(jnp.mean(x * x, axis=-1, keepdims=True) + EPS) * g


def _silu(x):
    return x * jax.nn.sigmoid(x)


def _resident(shape):
    nd = len(shape)
    return pl.BlockSpec(shape, lambda *_: (0,) * nd, pipeline_mode=pl.Buffered(1))


def _memkv_kernel(x_ref, g_ref, w_ref, k_ref, v_ref, *, nb):
    h = _rms(x_ref[...], g_ref[...]).astype(BF16)
    kv = jnp.dot(h, w_ref[...], preferred_element_type=F32)
    for b in range(nb):
        rows = slice(b * N_MEM, (b + 1) * N_MEM)
        for hd in range(XA_HEADS):
            k_ref[b, :, hd, :] = kv[rows, hd * XA_DIM:(hd + 1) * XA_DIM]
            v_ref[b, :, hd, :] = kv[rows, D_MODEL + hd * XA_DIM:D_MODEL + (hd + 1) * XA_DIM]


def _mem_kv(mem2d, mem_norm, w_kv):
    batch = mem2d.shape[0] // N_MEM
    nb = min(2, batch)
    tm = nb * N_MEM
    out = jax.ShapeDtypeStruct((DEPTH, batch, N_MEM, XA_HEADS, XA_DIM), F32)
    out_spec = pl.BlockSpec((None, nb, N_MEM, XA_HEADS, XA_DIM), lambda l, j: (l, j, 0, 0, 0))
    return pl.pallas_call(
        functools.partial(_memkv_kernel, nb=nb),
        grid=(DEPTH, batch // nb),
        in_specs=[
            pl.BlockSpec((tm, D_MODEL), lambda l, j: (j, 0)),
            pl.BlockSpec((None, 1, D_MODEL), lambda l, j: (l, 0, 0)),
            pl.BlockSpec((None, D_MODEL, 2 * D_MODEL), lambda l, j: (l, 0, 0)),
        ],
        out_specs=[out_spec, out_spec],
        out_shape=[out, out],
        compiler_params=_cparams(("arbitrary", "arbitrary")),
        name="mem_kv",
    )(mem2d, mem_norm.reshape(DEPTH, 1, D_MODEL), w_kv)


def _inproj_kernel(x_ref, g_ref, w_ref, o_ref, *, tn):
    h = _rms(x_ref[...], g_ref[...]).astype(BF16)
    for j in range(o_ref.shape[1] // tn):
        sl = slice(j * tn, (j + 1) * tn)
        o_ref[:, sl] = jnp.dot(h, w_ref[:, sl], preferred_element_type=F32).astype(BF16)


def _inproj(x, g, w):
    m, n = x.shape[0], w.shape[1]
    tm = min(512, m)
    return pl.pallas_call(
        functools.partial(_inproj_kernel, tn=512),
        grid=(m // tm,),
        in_specs=[
            pl.BlockSpec((tm, D_MODEL), lambda i: (i, 0)),
            _resident((1, D_MODEL)),
            _resident((D_MODEL, n)),
        ],
        out_specs=pl.BlockSpec((tm, n), lambda i: (i, 0)),
        out_shape=jax.ShapeDtypeStruct((m, n), BF16),
        compiler_params=_cparams(("arbitrary",)),
        name="inproj",
    )(x, g.reshape(1, D_MODEL), w)


def _conv_taps(u, prev1, prev2, w_ref):
    return (w_ref[0:1, :] * prev2 + w_ref[1:2, :] * prev1) + w_ref[2:3, :] * u


def _conv_long_kernel(b_ref, c_ref, v_ref, w_ref, st0_ref, mix_ref, st_ref, carry, *, tiles_per_seq):
    i = pl.program_id(0)
    tt = c_ref.shape[0]

    @pl.when(i % tiles_per_seq == 0)
    def _():
        carry[6:8, :] = st0_ref[0]

    u = c_ref[...].astype(F32) * v_ref[...].astype(F32)
    pm2 = carry[6:7, :]
    pm1 = carry[7:8, :]
    row = lax.broadcasted_iota(I32, u.shape, 0)
    prev1 = jnp.where(row == 0, pm1, pltpu.roll(u, 1, 0))
    prev2 = jnp.where(row == 0, pm2, jnp.where(row == 1, pm1, pltpu.roll(u, 2, 0)))
    mix_ref[...] = (b_ref[...].astype(F32) * _conv_taps(u, prev1, prev2, w_ref)).astype(BF16)
    carry[...] = u[tt - 8:tt, :]
    st_ref[0] = carry[6:8, :]


def _conv_long(z, conv_w, state0, batch, seq):
    tt = min(512, seq)
    tps = seq // tt
    col = lambda j: pl.BlockSpec((tt, D_MODEL), lambda i, j=j: (i, j))
    return pl.pallas_call(
        functools.partial(_conv_long_kernel, tiles_per_seq=tps),
        grid=(batch * tps,),
        in_specs=[col(0), col(1), col(2), _resident((CONV_WIDTH, D_MODEL)),
                  pl.BlockSpec((1, 2, D_MODEL), lambda i: (i // tps, 0, 0))],
        out_specs=[pl.BlockSpec((tt, D_MODEL), lambda i: (i, 0)),
                   pl.BlockSpec((1, 2, D_MODEL), lambda i: (i // tps, 0, 0))],
        out_shape=[jax.ShapeDtypeStruct((batch * seq, D_MODEL), BF16),
                   jax.ShapeDtypeStruct((batch, 2, D_MODEL), F32)],
        scratch_shapes=[pltpu.VMEM((8, D_MODEL), F32)],
        compiler_params=_cparams(("arbitrary",)),
        name="conv_long",
    )(z, z, z, conv_w, state0)


def _conv_short_kernel(b_ref, c_ref, v_ref, w_ref, p1_ref, p2_ref, mix_ref, u_ref, *, seq):
    u = c_ref[...].astype(F32) * v_ref[...].astype(F32)
    t = lax.broadcasted_iota(I32, u.shape, 0) % seq
    prev1 = jnp.where(t == 0, p1_ref[...], pltpu.roll(u, 1, 0))
    prev2 = jnp.where(t <= 1, p2_ref[...], pltpu.roll(u, 2, 0))
    mix_ref[...] = (b_ref[...].astype(F32) * _conv_taps(u, prev1, prev2, w_ref)).astype(BF16)
    u_ref[...] = u


def _conv_short(z, conv_w, state0, batch, seq):
    m = batch * seq
    zeros = jnp.zeros((batch, seq - 1, D_MODEL), F32)
    p1 = jnp.concatenate([state0[:, 1:2], zeros], axis=1).reshape(m, D_MODEL)
    p2 = jnp.concatenate([state0, zeros[:, 1:]], axis=1).reshape(m, D_MODEL)
    col = lambda j: pl.BlockSpec((m, D_MODEL), lambda i, j=j: (0, j))
    full = pl.BlockSpec((m, D_MODEL), lambda i: (0, 0))
    mix, u = pl.pallas_call(
        functools.partial(_conv_short_kernel, seq=seq),
        grid=(1,),
        in_specs=[col(0), col(1), col(2), _resident((CONV_WIDTH, D_MODEL)), full, full],
        out_specs=[full, full],
        out_shape=[jax.ShapeDtypeStruct((m, D_MODEL), BF16), jax.ShapeDtypeStruct((m, D_MODEL), F32)],
        compiler_params=_cparams(("arbitrary",)),
        name="conv_short",
    )(z, z, z, conv_w, p1, p2)
    return mix, u.reshape(batch, seq, D_MODEL)[:, seq - 2:]


def _xattn_kernel(q_ref, k_hbm, v_hbm, o_ref, kbuf, vbuf, sem, *, nb, layer, n_bt):
    i = pl.program_id(0)
    t = pl.program_id(1)
    slot = i % 2

    def copies(bt, sl):
        out = []
        for b in range(nb):
            for h in range(XA_HEADS):
                out.append(pltpu.make_async_copy(k_hbm.at[layer, bt * nb + b, :, h, :], kbuf.at[sl, b, h], sem.at[0, sl]))
                out.append(pltpu.make_async_copy(v_hbm.at[layer, bt * nb + b, :, h, :], vbuf.at[sl, b, h], sem.at[1, sl]))
        return out

    @pl.when((i == 0) & (t == 0))
    def _():
        for c in copies(0, 0):
            c.start()

    @pl.when(t == 0)
    def _():
        @pl.when(i + 1 < n_bt)
        def _():
            for c in copies(i + 1, 1 - slot):
                c.start()

        for c in copies(i, slot):
            c.wait()

    scale = XA_DIM ** -0.5
    tq = q_ref.shape[1]
    rows = -(-tq // SHORT_ROWS) * SHORT_ROWS
    pairs = [(b, h) for b in range(nb) for h in range(XA_HEADS)]
    scores = []
    for b, h in pairs:
        q = q_ref[b, :, h * XA_DIM:(h + 1) * XA_DIM]
        if rows != tq:
            q = jnp.concatenate([q.astype(F32), jnp.zeros((rows - tq, XA_DIM), F32)], axis=0).astype(BF16)
        k = kbuf[slot, b, h].astype(BF16)
        scores.append(lax.dot_general(q, k, NT_DIMS, preferred_element_type=F32) * scale)
    s = jnp.concatenate(scores, axis=0)
    p = jnp.exp(s - jnp.max(s, axis=-1, keepdims=True))
    p = (p / jnp.sum(p, axis=-1, keepdims=True)).astype(BF16)
    for n, (b, h) in enumerate(pairs):
        v = vbuf[slot, b, h].astype(BF16)
        o = jnp.dot(p[n * rows:(n + 1) * rows], v, preferred_element_type=F32)
        o_ref[b, :, h * XA_DIM:(h + 1) * XA_DIM] = o[:tq].astype(BF16)


def _xattn(z3, qcol, mem_k, mem_v, layer, nb, tq):
    batch, seq = z3.shape[0], z3.shape[1]
    n_bt = batch // nb
    plane = pltpu.VMEM((2, nb, XA_HEADS, N_MEM, XA_DIM), F32)
    return pl.pallas_call(
        functools.partial(_xattn_kernel, nb=nb, layer=layer, n_bt=n_bt),
        grid=(n_bt, seq // tq),
        in_specs=[pl.BlockSpec((nb, tq, D_MODEL), lambda i, t: (i, t, qcol)),
                  pl.BlockSpec(memory_space=pl.ANY), pl.BlockSpec(memory_space=pl.ANY)],
        out_specs=pl.BlockSpec((nb, tq, D_MODEL), lambda i, t: (i, t, 0)),
        out_shape=jax.ShapeDtypeStruct((batch, seq, D_MODEL), BF16),
        scratch_shapes=[plane, plane, pltpu.SemaphoreType.DMA((2, 2))],
        compiler_params=_cparams(("arbitrary", "arbitrary")),
        name="xattn",
    )(z3, mem_k, mem_v)


def _rotary(x, cos, sin):
    half = x.shape[-1] // 2
    x1, x2 = x[:, :half], x[:, half:]
    return jnp.concatenate([x1 * cos - x2 * sin, x1 * sin + x2 * cos], axis=-1)


def _group_norm_gate(o, gate, gn):
    mu = jnp.mean(o, axis=-1, keepdims=True)
    var = jnp.mean(jnp.square(o - mu), axis=-1, keepdims=True)
    return _silu(gate) * ((o - mu) * lax.rsqrt(var + EPS) * gn)


def _ret_long_kernel(full_ref, q_ref, k_ref, v_ref, g_ref, cos_ref, sin_ref, dec_ref, inn_ref, tail_ref,
                     gn_ref, s0_ref, mix_ref, sfin_ref, s_acc, *, chunk):
    t = pl.program_id(1)
    heads = range(RET_HEADS)

    @pl.when(t == 0)
    def _():
        s_acc[...] = s0_ref[0]

    cos, sin = cos_ref[...], sin_ref[...]
    qs = [_rotary(q_ref[0, :, h * RET_DK:(h + 1) * RET_DK].astype(F32), cos, sin) for h in heads]
    ks = [_rotary(k_ref[0, :, h * RET_DK:(h + 1) * RET_DK].astype(F32), cos, sin) * (RET_DK ** -0.5) for h in heads]
    for c in range(q_ref.shape[1] // chunk):
        sl = slice(c * chunk, (c + 1) * chunk)
        qc = [qs[h][sl].astype(BF16) for h in heads]
        kc = [ks[h][sl] for h in heads]
        vc = [v_ref[0, sl, h * RET_DV:(h + 1) * RET_DV] for h in heads]
        s_old = [s_acc[h] for h in heads]
        scores = [lax.dot_general(qc[h], kc[h].astype(BF16), NT_DIMS, preferred_element_type=F32) * dec_ref[h]
                  for h in heads]
        cross = [jnp.dot(qc[h], s_old[h].astype(BF16), preferred_element_type=F32) * inn_ref[h][:, 0:1]
                 for h in heads]
        o = [jnp.dot(scores[h].astype(BF16), vc[h], preferred_element_type=F32) + cross[h] for h in heads]
        for h in heads:
            kt = (kc[h] * tail_ref[h][:, 0:1]).astype(BF16)
            s_acc[h] = full_ref[h] * s_old[h] + lax.dot_general(kt, vc[h], TN_DIMS, preferred_element_type=F32)
        og = jnp.concatenate(o, axis=0)
        mu = jnp.mean(og, axis=-1, keepdims=True)
        var = jnp.mean(jnp.square(og - mu), axis=-1, keepdims=True)
        on = (og - mu) * lax.rsqrt(var + EPS)
        for h in heads:
            vs = slice(h * RET_DV, (h + 1) * RET_DV)
            gate = g_ref[0, sl, vs].astype(F32)
            mix_ref[0, sl, vs] = (_silu(gate) * (on[h * chunk:(h + 1) * chunk] * gn_ref[:, vs])).astype(BF16)

    @pl.when(t == pl.num_programs(1) - 1)
    def _():
        sfin_ref[0] = s_acc[...]


def _ret_tables(pos, chunk):
    half = RET_DK // 2
    inv = ROPE_BASE ** (-jnp.arange(half, dtype=F32) / half)
    ang = pos.astype(F32)[:, None] * inv[None, :]
    log_g = jnp.log1p(-jnp.exp2(-5.0 - jnp.arange(RET_HEADS, dtype=F32)))
    idx = jnp.arange(chunk)
    diff = idx[:, None] - idx[None, :]
    decay = jnp.where(diff[None] >= 0,
                      jnp.exp(jnp.maximum(diff, 0)[None].astype(F32) * log_g[:, None, None]), 0.0)
    inner = jnp.exp((idx + 1).astype(F32)[None, :] * log_g[:, None])
    tail = jnp.exp((chunk - 1 - idx).astype(F32)[None, :] * log_g[:, None])
    full = jnp.exp(chunk * log_g)
    return jnp.cos(ang), jnp.sin(ang), decay, inner, tail, full


def _ret_long(z3, state0, gn, pos):
    batch, seq = z3.shape[0], z3.shape[1]
    chunk = RET_CHUNK
    tc = min(512, seq)
    cos, sin, decay, inner, tail, full = _ret_tables(pos, chunk)
    lane_b = lambda a: jnp.broadcast_to(a[:, :, None], (RET_HEADS, chunk, LANES))
    qw = RET_HEADS * RET_DK
    vw = RET_HEADS * RET_DV
    state_spec = pl.BlockSpec((1, RET_HEADS, RET_DK, RET_DV), lambda b, t: (b, 0, 0, 0))
    return pl.pallas_call(
        functools.partial(_ret_long_kernel, chunk=chunk),
        grid=(batch, seq // tc),
        in_specs=[
            pl.BlockSpec(memory_space=pltpu.SMEM),
            pl.BlockSpec((1, tc, qw), lambda b, t: (b, t, 0)),
            pl.BlockSpec((1, tc, qw), lambda b, t: (b, t, 1)),
            pl.BlockSpec((1, tc, vw), lambda b, t: (b, t, 1)),
            pl.BlockSpec((1, tc, vw), lambda b, t: (b, t, 2)),
            pl.BlockSpec((tc, RET_DK // 2), lambda b, t: (t, 0)),
            pl.BlockSpec((tc, RET_DK // 2), lambda b, t: (t, 0)),
            _resident((RET_HEADS, chunk, chunk)), _resident((RET_HEADS, chunk, LANES)),
            _resident((RET_HEADS, chunk, LANES)), _resident((1, vw)),
            state_spec,
        ],
        out_specs=[pl.BlockSpec((1, tc, vw), lambda b, t: (b, t, 0)), state_spec],
        out_shape=[jax.ShapeDtypeStruct((batch, seq, vw), BF16),
                   jax.ShapeDtypeStruct((batch, RET_HEADS, RET_DK, RET_DV), F32)],
        scratch_shapes=[pltpu.VMEM((RET_HEADS, RET_DK, RET_DV), F32)],
        compiler_params=_cparams(("arbitrary", "arbitrary")),
        name="ret_long",
    )(full, z3, z3, z3, z3, cos, sin, decay, lane_b(inner), lane_b(tail), gn.reshape(1, -1), state0)


def _ret_short_kernel(full_ref, q_ref, k_ref, v_ref, g_ref, cos_ref, sin_ref, dec_ref, inn_ref, tail_ref,
                      gn_ref, s0_ref, mix_ref, snew_ref, *, nb, seq):
    def pad_rows(a):
        return jnp.concatenate([a, jnp.zeros((SHORT_ROWS - seq, a.shape[1]), F32)], axis=0)

    cos, sin = cos_ref[...], sin_ref[...]
    for b in range(nb):
        for h in range(RET_HEADS):
            qs = slice(h * RET_DK, (h + 1) * RET_DK)
            vs = slice(h * RET_DV, (h + 1) * RET_DV)
            q = pad_rows(_rotary(q_ref[b, :, qs].astype(F32), cos, sin)).astype(BF16)
            k = pad_rows(_rotary(k_ref[b, :, qs].astype(F32), cos, sin) * (RET_DK ** -0.5))
            v = pad_rows(v_ref[b, :, vs].astype(F32)).astype(BF16)
            s_old = s0_ref[b, h]
            scores = lax.dot_general(q, k.astype(BF16), NT_DIMS, preferred_element_type=F32) * dec_ref[h]
            o = jnp.dot(scores.astype(BF16), v, preferred_element_type=F32)
            o = o + jnp.dot(q, s_old.astype(BF16), preferred_element_type=F32) * inn_ref[h]
            snew_ref[b, h] = full_ref[h] * s_old + lax.dot_general((k * tail_ref[h]).astype(BF16), v, TN_DIMS,
                                                                   preferred_element_type=F32)
            mix_ref[b, :, vs] = _group_norm_gate(o[:seq], g_ref[b, :, vs].astype(F32), gn_ref[:, vs]).astype(BF16)


def _ret_short(z3, state0, gn, pos, nb=2):
    batch, seq = z3.shape[0], z3.shape[1]
    cos, sin, decay, inner, tail, full = _ret_tables(pos, seq)
    qw = RET_HEADS * RET_DK
    vw = RET_HEADS * RET_DV
    small = lambda a: _resident(a.shape)
    rp = SHORT_ROWS - seq
    decay = jnp.pad(decay, ((0, 0), (0, rp), (0, rp)))
    inner3 = jnp.pad(inner, ((0, 0), (0, rp)))[:, :, None]
    tail3 = jnp.pad(tail, ((0, 0), (0, rp)))[:, :, None]
    return pl.pallas_call(
        functools.partial(_ret_short_kernel, nb=nb, seq=seq),
        grid=(batch // nb,),
        in_specs=[
            pl.BlockSpec(memory_space=pltpu.SMEM),
            pl.BlockSpec((nb, seq, qw), lambda i: (i, 0, 0)),
            pl.BlockSpec((nb, seq, qw), lambda i: (i, 0, 1)),
            pl.BlockSpec((nb, seq, vw), lambda i: (i, 0, 1)),
            pl.BlockSpec((nb, seq, vw), lambda i: (i, 0, 2)),
            small(cos), small(sin), small(decay), small(inner3), small(tail3),
            _resident((1, vw)),
            pl.BlockSpec((nb, RET_HEADS, RET_DK, RET_DV), lambda i: (i, 0, 0, 0)),
        ],
        out_specs=[
            pl.BlockSpec((nb, seq, vw), lambda i: (i, 0, 0)),
            pl.BlockSpec((nb, RET_HEADS, RET_DK, RET_DV), lambda i: (i, 0, 0, 0)),
        ],
        out_shape=[jax.ShapeDtypeStruct((batch, seq, vw), BF16),
                   jax.ShapeDtypeStruct((batch, RET_HEADS, RET_DK, RET_DV), F32)],
        compiler_params=_cparams(("arbitrary",)),
        name="ret_short",
    )(full, z3, z3, z3, z3, cos, sin, decay, inner3, tail3, gn.reshape(1, -1), state0)


def _route(logits):
    lane = lax.broadcasted_iota(I32, logits.shape, 1).astype(F32)
    neg = jnp.float32(-1e30)
    big = jnp.float32(LANES)
    is_g = lane < N_GROUPS
    lg = jnp.where(is_g, logits, neg)
    gmax = jnp.max(lg, axis=-1, keepdims=True)
    gidx = jnp.min(jnp.where(lg == gmax, lane, big), axis=-1, keepdims=True)
    pg_sel = 1.0 / jnp.sum(jnp.where(is_g, jnp.exp(lg - gmax), 0.0), axis=-1, keepdims=True)
    lo = N_GROUPS + EXPERTS_PER_GROUP * gidx
    in_grp = (lane >= lo) & (lane < lo + EXPERTS_PER_GROUP)
    sel = jnp.where(in_grp, logits, neg)
    e = jnp.where(in_grp, jnp.exp(sel - jnp.max(sel, axis=-1, keepdims=True)), 0.0)
    pe = jnp.where(in_grp, e / jnp.sum(e, axis=-1, keepdims=True), -1.0)
    t1 = jnp.max(pe, axis=-1, keepdims=True)
    i1 = jnp.min(jnp.where(pe == t1, lane, big), axis=-1, keepdims=True)
    pe2 = jnp.where(lane == i1, -1.0, pe)
    t2 = jnp.max(pe2, axis=-1, keepdims=True)
    i2 = jnp.min(jnp.where(pe2 == t2, lane, big), axis=-1, keepdims=True)
    den = t1 + t2
    gates = jnp.where(lane == i1, t1 / den * pg_sel, 0.0) + jnp.where(lane == i2, t2 / den * pg_sel, 0.0)
    out = jnp.zeros_like(gates)
    for g in range(N_GROUPS):
        shifted = pltpu.roll(gates, LANES - (N_GROUPS + EXPERTS_PER_GROUP * g), 1)
        out = jnp.where((gidx == g) & (lane < EXPERTS_PER_GROUP), shifted, out)
    return jnp.where(lane == GID_LANE, gidx, out)


def _outproj_kernel(x_ref, mix_ref, xa_ref, wa_ref, wb_ref, g_ref, wr_ref, o_ref, r_ref, *, sub):
    logits = []
    for s in range(x_ref.shape[0] // sub):
        rows = slice(s * sub, (s + 1) * sub)
        acc = jnp.dot(mix_ref[rows, :], wa_ref[...], preferred_element_type=F32)
        acc = acc + jnp.dot(xa_ref[rows, :], wb_ref[...], preferred_element_type=F32)
        xn = x_ref[rows, :] + acc
        o_ref[rows, :D_MODEL] = xn
        h2 = _rms(xn, g_ref[...])
        hi = h2.astype(BF16)
        lo = (h2 - hi.astype(F32)).astype(BF16)
        p_hi = jnp.dot(hi, wr_ref[...], preferred_element_type=F32)
        p_lo = jnp.dot(lo, wr_ref[:, :ROUTER_W], preferred_element_type=F32)
        logits.append((p_hi[:, :ROUTER_W] + p_lo) + p_hi[:, ROUTER_W:])
    route = _route(jnp.concatenate(logits, axis=0))
    o_ref[:, D_MODEL:] = route
    r_ref[...] = route


def _outproj_router(x, mix, xa, w_out, g_ffn, w_router2):
    m, cm = mix.shape
    tm = min(512, m)
    row = lambda w: pl.BlockSpec((tm, w), lambda i: (i, 0))
    return pl.pallas_call(
        functools.partial(_outproj_kernel, sub=min(128, tm)),
        grid=(m // tm,),
        in_specs=[row(D_MODEL), row(cm), row(D_MODEL), _resident((cm, D_MODEL)), _resident((D_MODEL, D_MODEL)),
                  _resident((1, D_MODEL)), _resident((D_MODEL, 2 * ROUTER_W))],
        out_specs=[row(AUG_W), row(LANES)],
        out_shape=[jax.ShapeDtypeStruct((m, AUG_W), F32), jax.ShapeDtypeStruct((m, LANES), F32)],
        compiler_params=_cparams(("arbitrary",)),
        name="outproj_router",
    )(x, mix, xa, w_out[:cm], w_out[cm:], g_ffn.reshape(1, D_MODEL), w_router2)


STEP_VALID, STEP_FIRST, STEP_LAST, STEP_NEWGROUP = 1, 2, 4, 8


def _moe_plan(gid, tm):
    m = gid.shape[0]
    ntile = m // tm
    nstep = ntile + N_GROUPS - 1
    counts = jnp.sum((gid[:, None] == jnp.arange(N_GROUPS, dtype=I32)[None, :]).astype(I32), axis=0)
    gend = jnp.cumsum(counts)
    gstart = gend - counts
    row_tok = jnp.argsort(gid, stable=True).astype(I32)
    lo = jnp.arange(ntile, dtype=I32)[:, None] * tm
    present = (gstart[None, :] < lo + tm) & (gend[None, :] > lo) & (counts[None, :] > 0)
    idx = jnp.nonzero(present.reshape(-1), size=nstep, fill_value=-1)[0].astype(I32)
    valid = idx >= 0
    idx = jnp.where(valid, idx, jnp.max(idx))
    st_tile = idx // N_GROUPS
    st_gid = idx % N_GROUPS
    prev_tile = jnp.concatenate([jnp.full((1,), -1, I32), st_tile[:-1]])
    next_tile = jnp.concatenate([st_tile[1:], jnp.full((1,), -1, I32)])
    next_valid = jnp.concatenate([valid[1:], jnp.zeros((1,), bool)])
    first = valid & (st_tile != prev_tile)
    last = valid & ((st_tile != next_tile) | ~next_valid)
    prev_gid = jnp.concatenate([jnp.full((1,), -1, I32), st_gid[:-1]])
    newgroup = valid & (st_gid != prev_gid)
    flags = valid * STEP_VALID + first * STEP_FIRST + last * STEP_LAST + newgroup * STEP_NEWGROUP
    return st_tile, st_gid, flags.astype(I32), row_tok


def _moe_kernel(tile_ref, gid_ref, flag_ref, tok_ref, xaug_hbm, g_ref, fin_ref, wgu_ref, wd_ref, out_hbm,
                xbuf, ybuf, wgu_bf, wd_bf, gsem, ssem, *, tm, ntile, final_norm):
    s = pl.program_id(0)
    tile = tile_ref[s]
    grp = gid_ref[s]
    flags = flag_ref[s]
    valid = (flags & STEP_VALID) > 0
    first = (flags & STEP_FIRST) > 0
    last = (flags & STEP_LAST) > 0
    slot = tile % 2

    @pl.when((flags & STEP_NEWGROUP) > 0)
    def _():
        for j in range(EXPERTS_PER_GROUP):
            wgu_bf[j] = wgu_ref[j].astype(BF16)
            wd_bf[j] = wd_ref[j].astype(BF16)

    def gather_start(t, sl):
        for r in range(tm):
            tok = tok_ref[t * tm + r]
            pltpu.make_async_copy(xaug_hbm.at[pl.ds(tok, 1)], xbuf.at[sl, pl.ds(r, 1)], gsem.at[sl]).start()

    def scatter_start(t, sl):
        for r in range(tm):
            tok = tok_ref[t * tm + r]
            pltpu.make_async_copy(ybuf.at[sl, pl.ds(r, 1)], out_hbm.at[pl.ds(tok, 1)], ssem.at[sl]).start()

    def gather_wait(sl):
        pltpu.make_async_copy(xaug_hbm.at[pl.ds(0, tm)], xbuf.at[sl], gsem.at[sl]).wait()

    def scatter_wait(sl):
        pltpu.make_async_copy(ybuf.at[sl], out_hbm.at[pl.ds(0, tm)], ssem.at[sl]).wait()

    @pl.when(s == 0)
    def _():
        gather_start(0, 0)

    @pl.when(first & (tile + 1 < ntile))
    def _():
        gather_start(tile + 1, 1 - slot)

    @pl.when(first)
    def _():
        gather_wait(slot)

        @pl.when(tile >= 2)
        def _():
            scatter_wait(slot)

        ybuf[slot] = xbuf[slot, :, 0:D_MODEL]

    @pl.when(valid)
    def _():
        xn = xbuf[slot, :, 0:D_MODEL]
        gl = xbuf[slot, :, GATE_LANE0:AUG_W]
        mine = gl[:, GID_LANE:GID_LANE + 1] == grp.astype(F32)
        h2 = _rms(xn, g_ref[...]).astype(BF16)
        acc = jnp.zeros((tm, D_MODEL), F32)
        for j in range(EXPERTS_PER_GROUP):
            gu = jnp.dot(h2, wgu_bf[j], preferred_element_type=F32)
            gate = jnp.where(mine, gl[:, j:j + 1], 0.0)
            act = _silu(gu[:, :EXPERT_FF]) * gu[:, EXPERT_FF:] * gate
            acc = acc + jnp.dot(act.astype(BF16), wd_bf[j], preferred_element_type=F32)
        ybuf[slot] = ybuf[slot] + acc

    @pl.when(last)
    def _():
        if final_norm:
            ybuf[slot] = _rms(ybuf[slot], fin_ref[...])
        scatter_start(tile, slot)

        @pl.when(tile == ntile - 1)
        def _():
            if ntile >= 2:
                scatter_wait(1 - slot)
            scatter_wait(slot)


def _moe(xaug, route, g_ffn, final_g, w_gu, w_down, layer, final_norm):
    m = xaug.shape[0]
    tm = min(MOE_TM, m)
    gid = route[:, GID_LANE].astype(I32)
    st_tile, st_gid, flags, row_tok = _moe_plan(gid, tm)
    epg = EXPERTS_PER_GROUP
    grid_spec = pltpu.PrefetchScalarGridSpec(
        num_scalar_prefetch=4,
        grid=(st_tile.shape[0],),
        in_specs=[
            pl.BlockSpec(memory_space=pl.ANY),
            pl.BlockSpec((1, D_MODEL), lambda s, *_: (0, 0)),
            pl.BlockSpec((1, D_MODEL), lambda s, *_: (0, 0)),
            pl.BlockSpec((None, epg, D_MODEL, 2 * EXPERT_FF), lambda s, tl, gd, fl, tok: (layer, gd[s], 0, 0),
                         pipeline_mode=pl.Buffered(1)),
            pl.BlockSpec((None, epg, EXPERT_FF, D_MODEL), lambda s, tl, gd, fl, tok: (layer, gd[s], 0, 0),
                         pipeline_mode=pl.Buffered(1)),
        ],
        out_specs=pl.BlockSpec(memory_space=pl.ANY),
        scratch_shapes=[
            pltpu.VMEM((2, tm, AUG_W), F32),
            pltpu.VMEM((2, tm, D_MODEL), F32),
            pltpu.VMEM((epg, D_MODEL, 2 * EXPERT_FF), BF16),
            pltpu.VMEM((epg, EXPERT_FF, D_MODEL), BF16),
            pltpu.SemaphoreType.DMA((2,)),
            pltpu.SemaphoreType.DMA((2,)),
        ],
    )
    return pl.pallas_call(
        functools.partial(_moe_kernel, tm=tm, ntile=m // tm, final_norm=final_norm),
        grid_spec=grid_spec,
        out_shape=jax.ShapeDtypeStruct((m, D_MODEL), F32),
        compiler_params=_cparams(("arbitrary",)),
        name="moe",
    )(st_tile, st_gid, flags, row_tok, xaug, g_ffn.reshape(1, D_MODEL), final_g.reshape(1, D_MODEL), w_gu, w_down)


def _trunk(x, pos, conv_state, ret_state, mem_k, mem_v, wts, long_seq):
    batch, seq = x.shape[0], x.shape[1]
    m = batch * seq
    x2 = x.reshape(m, D_MODEL)
    if long_seq:
        nb, tq = 1, min(512, seq)
    else:
        nb, tq = 4, seq
    new_conv = new_ret = None
    for i in range(DEPTH):
        if i % 2 == 0:
            z = _inproj(x2, wts["norm_mix"][i], wts["conv_w_in"])
            conv = _conv_long if long_seq else _conv_short
            mix, new_conv = conv(z, wts["conv_w"], conv_state, batch, seq)
            qcol = 3
            w_out = wts["conv_w_out"]
        else:
            z = _inproj(x2, wts["norm_mix"][i], wts["ret_w_in"])
            ret = _ret_long if long_seq else _ret_short
            mix, new_ret = ret(z.reshape(batch, seq, -1), ret_state, wts["ret_gn"], pos)
            mix = mix.reshape(m, -1)
            qcol = 6
            w_out = wts["ret_w_out"]
        xa = _xattn(z.reshape(batch, seq, -1), qcol, mem_k, mem_v, i, nb, tq).reshape(m, D_MODEL)
        xaug, route = _outproj_router(x2, mix, xa, w_out, wts["norm_ffn"][i], wts["w_router"][i])
        x2 = _moe(xaug, route, wts["norm_ffn"][i], wts["final_norm"], wts["moe_w_gate_up"], wts["moe_w_down"], i,
                  final_norm=(i == DEPTH - 1))
    return x2.reshape(batch, seq, D_MODEL), new_conv, new_ret


def kernel(x_prompt, x_sample, state_conv, state_ret, cache_mem_k, cache_mem_v, mem_prompt, norm_mix, norm_ffn,
           mem_norm, w_mem_kv, conv_w_in, conv_w, conv_w_out, ret_w_in, ret_gn, ret_w_out, moe_w_group,
           moe_w_router, moe_w_gate_up, moe_w_down, final_norm):
    batch, seq = x_prompt.shape[0], x_prompt.shape[1]
    dec_batch, dec_seq = x_sample.shape[0], x_sample.shape[1]
    w_router = jnp.concatenate(
        [moe_w_group, moe_w_router, jnp.zeros((DEPTH, D_MODEL, ROUTER_W - N_GROUPS - N_EXPERTS), F32)], axis=-1)
    w_router_hi = w_router.astype(BF16)
    w_router = jnp.concatenate([w_router_hi, (w_router - w_router_hi.astype(F32)).astype(BF16)], axis=-1)
    wts = dict(
        norm_mix=norm_mix, norm_ffn=norm_ffn, final_norm=final_norm, w_router=w_router,
        conv_w_in=conv_w_in[0].astype(BF16), conv_w=conv_w[0], conv_w_out=conv_w_out[0].astype(BF16),
        ret_w_in=ret_w_in[0].astype(BF16), ret_gn=ret_gn[0], ret_w_out=ret_w_out[0].astype(BF16),
        moe_w_gate_up=moe_w_gate_up, moe_w_down=moe_w_down,
    )
    mem_k_prompt, mem_v_prompt = _mem_kv(mem_prompt.reshape(batch * N_MEM, D_MODEL), mem_norm, w_mem_kv.astype(BF16))

    y_prompt, conv_p, ret_p = _trunk(
        x_prompt, jnp.arange(seq, dtype=I32),
        jnp.zeros((batch, CONV_WIDTH - 1, D_MODEL), F32), jnp.zeros((batch, RET_HEADS, RET_DK, RET_DV), F32),
        mem_k_prompt, mem_v_prompt, wts, True)
    y_sample, conv_s, ret_s = _trunk(
        x_sample, PAST_LEN + jnp.arange(dec_seq, dtype=I32), state_conv[0], state_ret[0],
        cache_mem_k, cache_mem_v, wts, False)
    return (y_prompt, y_sample, conv_p[None], conv_s[None], ret_p[None], ret_s[None], mem_k_prompt, mem_v_prompt)
```

```python
import functools

import jax
import jax.numpy as jnp
from jax import lax
from jax.experimental import pallas as pl
from jax.experimental.pallas import tpu as pltpu

F32 = jnp.float32
BF16 = jnp.bfloat16
I32 = jnp.int32

D_MODEL = 1024
DEPTH = 2
CONV_WIDTH = 3
RET_HEADS = 4
RET_DK = 256
RET_DV = 512
RET_CHUNK = 128
ROPE_BASE = 10000.0
N_MEM = 256
XA_HEADS = 4
XA_DIM = 256
N_GROUPS = 4
EXPERTS_PER_GROUP = 8
N_EXPERTS = N_GROUPS * EXPERTS_PER_GROUP
EXPERT_FF = 256
EPS = 1e-6

LANES = 128
AUG_W = D_MODEL + LANES
GATE_LANE0 = D_MODEL
GID_LANE = EXPERTS_PER_GROUP
ROUTER_ROWS = 64
VMEM_LIMIT = 56 * 1024 * 1024
MOE_TM = 256
SHORT_ROWS = 8
PAST_LEN = 16384
NT_DIMS = (((1,), (1,)), ((), ()))
TN_DIMS = (((0,), (0,)), ((), ()))


def _cparams(sem):
    return pltpu.CompilerParams(dimension_semantics=sem, vmem_limit_bytes=VMEM_LIMIT)


def _rms(x, g):
    return x * lax.rsqrt(jnp.mean(x * x, axis=-1, keepdims=True) + EPS) * g


def _silu(x):
    return x * jax.nn.sigmoid(x)


def _resident(shape):
    nd = len(shape)
    return pl.BlockSpec(shape, lambda *_: (0,) * nd, pipeline_mode=pl.Buffered(1))


def _memkv_kernel(x_ref, g_ref, w_ref, k_ref, v_ref, *, nb):
    h = _rms(x_ref[...], g_ref[...]).astype(BF16)
    kv = jnp.dot(h, w_ref[...], preferred_element_type=F32)
    for b in range(nb):
        rows = slice(b * N_MEM, (b + 1) * N_MEM)
        for hd in range(XA_HEADS):
            k_ref[b, :, hd, :] = kv[rows, hd * XA_DIM:(hd + 1) * XA_DIM]
            v_ref[b, :, hd, :] = kv[rows, D_MODEL + hd * XA_DIM:D_MODEL + (hd + 1) * XA_DIM]


def _mem_kv(mem2d, mem_norm, w_kv):
    batch = mem2d.shape[0] // N_MEM
    nb = min(2, batch)
    tm = nb * N_MEM
    out = jax.ShapeDtypeStruct((DEPTH, batch, N_MEM, XA_HEADS, XA_DIM), F32)
    out_spec = pl.BlockSpec((None, nb, N_MEM, XA_HEADS, XA_DIM), lambda l, j: (l, j, 0, 0, 0))
    return pl.pallas_call(
        functools.partial(_memkv_kernel, nb=nb),
        grid=(DEPTH, batch // nb),
        in_specs=[
            pl.BlockSpec((tm, D_MODEL), lambda l, j: (j, 0)),
            pl.BlockSpec((None, 1, D_MODEL), lambda l, j: (l, 0, 0)),
            pl.BlockSpec((None, D_MODEL, 2 * D_MODEL), lambda l, j: (l, 0, 0)),
        ],
        out_specs=[out_spec, out_spec],
        out_shape=[out, out],
        compiler_params=_cparams(("arbitrary", "arbitrary")),
        name="mem_kv",
    )(mem2d, mem_norm.reshape(DEPTH, 1, D_MODEL), w_kv)


def _inproj_kernel(x_ref, g_ref, w_ref, o_ref, *, tn):
    h = _rms(x_ref[...], g_ref[...]).astype(BF16)
    for j in range(o_ref.shape[1] // tn):
        sl = slice(j * tn, (j + 1) * tn)
        o_ref[:, sl] = jnp.dot(h, w_ref[:, sl], preferred_element_type=F32).astype(BF16)


def _inproj(x, g, w):
    m, n = x.shape[0], w.shape[1]
    tm = min(512, m)
    return pl.pallas_call(
        functools.partial(_inproj_kernel, tn=512),
        grid=(m // tm,),
        in_specs=[
            pl.BlockSpec((tm, D_MODEL), lambda i: (i, 0)),
            _resident((1, D_MODEL)),
            _resident((D_MODEL, n)),
        ],
        out_specs=pl.BlockSpec((tm, n), lambda i: (i, 0)),
        out_shape=jax.ShapeDtypeStruct((m, n), BF16),
        compiler_params=_cparams(("arbitrary",)),
        name="inproj",
    )(x, g.reshape(1, D_MODEL), w)


def _conv_taps(u, prev1, prev2, w_ref):
    return (w_ref[0:1, :] * prev2 + w_ref[1:2, :] * prev1) + w_ref[2:3, :] * u


def _conv_long_kernel(b_ref, c_ref, v_ref, w_ref, st0_ref, mix_ref, st_ref, carry, *, tiles_per_seq):
    i = pl.program_id(0)
    tt = c_ref.shape[0]

    @pl.when(i % tiles_per_seq == 0)
    def _():
        carry[6:8, :] = st0_ref[0]

    u = c_ref[...].astype(F32) * v_ref[...].astype(F32)
    pm2 = carry[6:7, :]
    pm1 = carry[7:8, :]
    row = lax.broadcasted_iota(I32, u.shape, 0)
    prev1 = jnp.where(row == 0, pm1, pltpu.roll(u, 1, 0))
    prev2 = jnp.where(row == 0, pm2, jnp.where(row == 1, pm1, pltpu.roll(u, 2, 0)))
    mix_ref[...] = (b_ref[...].astype(F32) * _conv_taps(u, prev1, prev2, w_ref)).astype(BF16)
    carry[...] = u[tt - 8:tt, :]
    st_ref[0] = carry[6:8, :]


def _conv_long(z, conv_w, state0, batch, seq):
    tt = min(512, seq)
    tps = seq // tt
    col = lambda j: pl.BlockSpec((tt, D_MODEL), lambda i, j=j: (i, j))
    return pl.pallas_call(
        functools.partial(_conv_long_kernel, tiles_per_seq=tps),
        grid=(batch * tps,),
        in_specs=[col(0), col(1), col(2), _resident((CONV_WIDTH, D_MODEL)),
                  pl.BlockSpec((1, 2, D_MODEL), lambda i: (i // tps, 0, 0))],
        out_specs=[pl.BlockSpec((tt, D_MODEL), lambda i: (i, 0)),
                   pl.BlockSpec((1, 2, D_MODEL), lambda i: (i // tps, 0, 0))],
        out_shape=[jax.ShapeDtypeStruct((batch * seq, D_MODEL), BF16),
                   jax.ShapeDtypeStruct((batch, 2, D_MODEL), F32)],
        scratch_shapes=[pltpu.VMEM((8, D_MODEL), F32)],
        compiler_params=_cparams(("arbitrary",)),
        name="conv_long",
    )(z, z, z, conv_w, state0)


def _conv_short_kernel(b_ref, c_ref, v_ref, w_ref, p1_ref, p2_ref, mix_ref, u_ref, *, seq):
    u = c_ref[...].astype(F32) * v_ref[...].astype(F32)
    t = lax.broadcasted_iota(I32, u.shape, 0) % seq
    prev1 = jnp.where(t == 0, p1_ref[...], pltpu.roll(u, 1, 0))
    prev2 = jnp.where(t <= 1, p2_ref[...], pltpu.roll(u, 2, 0))
    mix_ref[...] = (b_ref[...].astype(F32) * _conv_taps(u, prev1, prev2, w_ref)).astype(BF16)
    u_ref[...] = u


def _conv_short(z, conv_w, state0, batch, seq):
    m = batch * seq
    zeros = jnp.zeros((batch, seq - 1, D_MODEL), F32)
    p1 = jnp.concatenate([state0[:, 1:2], zeros], axis=1).reshape(m, D_MODEL)
    p2 = jnp.concatenate([state0, zeros[:, 1:]], axis=1).reshape(m, D_MODEL)
    col = lambda j: pl.BlockSpec((m, D_MODEL), lambda i, j=j: (0, j))
    full = pl.BlockSpec((m, D_MODEL), lambda i: (0, 0))
    mix, u = pl.pallas_call(
        functools.partial(_conv_short_kernel, seq=seq),
        grid=(1,),
        in_specs=[col(0), col(1), col(2), _resident((CONV_WIDTH, D_MODEL)), full, full],
        out_specs=[full, full],
        out_shape=[jax.ShapeDtypeStruct((m, D_MODEL), BF16), jax.ShapeDtypeStruct((m, D_MODEL), F32)],
        compiler_params=_cparams(("arbitrary",)),
        name="conv_short",
    )(z, z, z, conv_w, p1, p2)
    return mix, u.reshape(batch, seq, D_MODEL)[:, seq - 2:]


def _xattn_kernel(q_ref, k_hbm, v_hbm, o_ref, kbuf, vbuf, sem, *, nb, layer, n_bt):
    i = pl.program_id(0)
    t = pl.program_id(1)
    slot = i % 2

    def copies(bt, sl):
        out = []
        for b in range(nb):
            for h in range(XA_HEADS):
                out.append(pltpu.make_async_copy(k_hbm.at[layer, bt * nb + b, :, h, :], kbuf.at[sl, b, h], sem.at[0, sl]))
                out.append(pltpu.make_async_copy(v_hbm.at[layer, bt * nb + b, :, h, :], vbuf.at[sl, b, h], sem.at[1, sl]))
        return out

    @pl.when((i == 0) & (t == 0))
    def _():
        for c in copies(0, 0):
            c.start()

    @pl.when(t == 0)
    def _():
        @pl.when(i + 1 < n_bt)
        def _():
            for c in copies(i + 1, 1 - slot):
                c.start()

        for c in copies(i, slot):
            c.wait()

    scale = XA_DIM ** -0.5
    tq = q_ref.shape[1]
    rows = -(-tq // SHORT_ROWS) * SHORT_ROWS
    pairs = [(b, h) for b in range(nb) for h in range(XA_HEADS)]
    scores = []
    for b, h in pairs:
        q = q_ref[b, :, h * XA_DIM:(h + 1) * XA_DIM]
        if rows != tq:
            q = jnp.concatenate([q.astype(F32), jnp.zeros((rows - tq, XA_DIM), F32)], axis=0).astype(BF16)
        k = kbuf[slot, b, h].astype(BF16)
        scores.append(lax.dot_general(q, k, NT_DIMS, preferred_element_type=F32) * scale)
    s = jnp.concatenate(scores, axis=0)
    p = jnp.exp(s - jnp.max(s, axis=-1, keepdims=True))
    p = (p / jnp.sum(p, axis=-1, keepdims=True)).astype(BF16)
    for n, (b, h) in enumerate(pairs):
        v = vbuf[slot, b, h].astype(BF16)
        o = jnp.dot(p[n * rows:(n + 1) * rows], v, preferred_element_type=F32)
        o_ref[b, :, h * XA_DIM:(h + 1) * XA_DIM] = o[:tq].astype(BF16)


def _xattn(z3, qcol, mem_k, mem_v, layer, nb, tq):
    batch, seq = z3.shape[0], z3.shape[1]
    n_bt = batch // nb
    plane = pltpu.VMEM((2, nb, XA_HEADS, N_MEM, XA_DIM), F32)
    return pl.pallas_call(
        functools.partial(_xattn_kernel, nb=nb, layer=layer, n_bt=n_bt),
        grid=(n_bt, seq // tq),
        in_specs=[pl.BlockSpec((nb, tq, D_MODEL), lambda i, t: (i, t, qcol)),
                  pl.BlockSpec(memory_space=pl.ANY), pl.BlockSpec(memory_space=pl.ANY)],
        out_specs=pl.BlockSpec((nb, tq, D_MODEL), lambda i, t: (i, t, 0)),
        out_shape=jax.ShapeDtypeStruct((batch, seq, D_MODEL), BF16),
        scratch_shapes=[plane, plane, pltpu.SemaphoreType.DMA((2, 2))],
        compiler_params=_cparams(("arbitrary", "arbitrary")),
        name="xattn",
    )(z3, mem_k, mem_v)


def _rotary(x, cos, sin):
    half = x.shape[-1] // 2
    x1, x2 = x[:, :half], x[:, half:]
    return jnp.concatenate([x1 * cos - x2 * sin, x1 * sin + x2 * cos], axis=-1)


def _group_norm_gate(o, gate, gn):
    mu = jnp.mean(o, axis=-1, keepdims=True)
    var = jnp.mean(jnp.square(o - mu), axis=-1, keepdims=True)
    return _silu(gate) * ((o - mu) * lax.rsqrt(var + EPS) * gn)


def _ret_long_kernel(full_ref, q_ref, k_ref, v_ref, g_ref, cos_ref, sin_ref, dec_ref, inn_ref, tail_ref,
                     gn_ref, s0_ref, mix_ref, sfin_ref, s_acc, *, chunk):
    t = pl.program_id(1)
    heads = range(RET_HEADS)

    @pl.when(t == 0)
    def _():
        s_acc[...] = s0_ref[0]

    cos, sin = cos_ref[...], sin_ref[...]
    qs = [_rotary(q_ref[0, :, h * RET_DK:(h + 1) * RET_DK].astype(F32), cos, sin) for h in heads]
    ks = [_rotary(k_ref[0, :, h * RET_DK:(h + 1) * RET_DK].astype(F32), cos, sin) * (RET_DK ** -0.5) for h in heads]
    for c in range(q_ref.shape[1] // chunk):
        sl = slice(c * chunk, (c + 1) * chunk)
        qc = [qs[h][sl].astype(BF16) for h in heads]
        kc = [ks[h][sl] for h in heads]
        vc = [v_ref[0, sl, h * RET_DV:(h + 1) * RET_DV] for h in heads]
        s_old = [s_acc[h] for h in heads]
        scores = [lax.dot_general(qc[h], kc[h].astype(BF16), NT_DIMS, preferred_element_type=F32) * dec_ref[h]
                  for h in heads]
        cross = [jnp.dot(qc[h], s_old[h].astype(BF16), preferred_element_type=F32) * inn_ref[h][:, 0:1]
                 for h in heads]
        o = [jnp.dot(scores[h].astype(BF16), vc[h], preferred_element_type=F32) + cross[h] for h in heads]
        for h in heads:
            kt = (kc[h] * tail_ref[h][:, 0:1]).astype(BF16)
            s_acc[h] = full_ref[h] * s_old[h] + lax.dot_general(kt, vc[h], TN_DIMS, preferred_element_type=F32)
        og = jnp.concatenate(o, axis=0)
        mu = jnp.mean(og, axis=-1, keepdims=True)
        var = jnp.mean(jnp.square(og - mu), axis=-1, keepdims=True)
        on = (og - mu) * lax.rsqrt(var + EPS)
        for h in heads:
            vs = slice(h * RET_DV, (h + 1) * RET_DV)
            gate = g_ref[0, sl, vs].astype(F32)
            mix_ref[0, sl, vs] = (_silu(gate) * (on[h * chunk:(h + 1) * chunk] * gn_ref[:, vs])).astype(BF16)

    @pl.when(t == pl.num_programs(1) - 1)
    def _():
        sfin_ref[0] = s_acc[...]


def _ret_tables(pos, chunk):
    half = RET_DK // 2
    inv = ROPE_BASE ** (-jnp.arange(half, dtype=F32) / half)
    ang = pos.astype(F32)[:, None] * inv[None, :]
    log_g = jnp.log1p(-jnp.exp2(-5.0 - jnp.arange(RET_HEADS, dtype=F32)))
    idx = jnp.arange(chunk)
    diff = idx[:, None] - idx[None, :]
    decay = jnp.where(diff[None] >= 0,
                      jnp.exp(jnp.maximum(diff, 0)[None].astype(F32) * log_g[:, None, None]), 0.0)
    inner = jnp.exp((idx + 1).astype(F32)[None, :] * log_g[:, None])
    tail = jnp.exp((chunk - 1 - idx).astype(F32)[None, :] * log_g[:, None])
    full = jnp.exp(chunk * log_g)
    return jnp.cos(ang), jnp.sin(ang), decay, inner, tail, full


def _ret_long(z3, state0, gn, pos):
    batch, seq = z3.shape[0], z3.shape[1]
    chunk = RET_CHUNK
    tc = min(512, seq)
    cos, sin, decay, inner, tail, full = _ret_tables(pos, chunk)
    lane_b = lambda a: jnp.broadcast_to(a[:, :, None], (RET_HEADS, chunk, LANES))
    qw = RET_HEADS * RET_DK
    vw = RET_HEADS * RET_DV
    state_spec = pl.BlockSpec((1, RET_HEADS, RET_DK, RET_DV), lambda b, t: (b, 0, 0, 0))
    return pl.pallas_call(
        functools.partial(_ret_long_kernel, chunk=chunk),
        grid=(batch, seq // tc),
        in_specs=[
            pl.BlockSpec(memory_space=pltpu.SMEM),
            pl.BlockSpec((1, tc, qw), lambda b, t: (b, t, 0)),
            pl.BlockSpec((1, tc, qw), lambda b, t: (b, t, 1)),
            pl.BlockSpec((1, tc, vw), lambda b, t: (b, t, 1)),
            pl.BlockSpec((1, tc, vw), lambda b, t: (b, t, 2)),
            pl.BlockSpec((tc, RET_DK // 2), lambda b, t: (t, 0)),
            pl.BlockSpec((tc, RET_DK // 2), lambda b, t: (t, 0)),
            _resident((RET_HEADS, chunk, chunk)), _resident((RET_HEADS, chunk, LANES)),
            _resident((RET_HEADS, chunk, LANES)), _resident((1, vw)),
            state_spec,
        ],
        out_specs=[pl.BlockSpec((1, tc, vw), lambda b, t: (b, t, 0)), state_spec],
        out_shape=[jax.ShapeDtypeStruct((batch, seq, vw), BF16),
                   jax.ShapeDtypeStruct((batch, RET_HEADS, RET_DK, RET_DV), F32)],
        scratch_shapes=[pltpu.VMEM((RET_HEADS, RET_DK, RET_DV), F32)],
        compiler_params=_cparams(("arbitrary", "arbitrary")),
        name="ret_long",
    )(full, z3, z3, z3, z3, cos, sin, decay, lane_b(inner), lane_b(tail), gn.reshape(1, -1), state0)


def _ret_short_kernel(full_ref, q_ref, k_ref, v_ref, g_ref, cos_ref, sin_ref, dec_ref, inn_ref, tail_ref,
                      gn_ref, s0_ref, mix_ref, snew_ref, *, nb, seq):
    def pad_rows(a):
        return jnp.concatenate([a, jnp.zeros((SHORT_ROWS - seq, a.shape[1]), F32)], axis=0)

    cos, sin = cos_ref[...], sin_ref[...]
    for b in range(nb):
        for h in range(RET_HEADS):
            qs = slice(h * RET_DK, (h + 1) * RET_DK)
            vs = slice(h * RET_DV, (h + 1) * RET_DV)
            q = pad_rows(_rotary(q_ref[b, :, qs].astype(F32), cos, sin)).astype(BF16)
            k = pad_rows(_rotary(k_ref[b, :, qs].astype(F32), cos, sin) * (RET_DK ** -0.5))
            v = pad_rows(v_ref[b, :, vs].astype(F32)).astype(BF16)
            s_old = s0_ref[b, h]
            scores = lax.dot_general(q, k.astype(BF16), NT_DIMS, preferred_element_type=F32) * dec_ref[h]
            o = jnp.dot(scores.astype(BF16), v, preferred_element_type=F32)
            o = o + jnp.dot(q, s_old.astype(BF16), preferred_element_type=F32) * inn_ref[h]
            snew_ref[b, h] = full_ref[h] * s_old + lax.dot_general((k * tail_ref[h]).astype(BF16), v, TN_DIMS,
                                                                   preferred_element_type=F32)
            mix_ref[b, :, vs] = _group_norm_gate(o[:seq], g_ref[b, :, vs].astype(F32), gn_ref[:, vs]).astype(BF16)


def _ret_short(z3, state0, gn, pos, nb=2):
    batch, seq = z3.shape[0], z3.shape[1]
    cos, sin, decay, inner, tail, full = _ret_tables(pos, seq)
    qw = RET_HEADS * RET_DK
    vw = RET_HEADS * RET_DV
    small = lambda a: _resident(a.shape)
    rp = SHORT_ROWS - seq
    decay = jnp.pad(decay, ((0, 0), (0, rp), (0, rp)))
    inner3 = jnp.pad(inner, ((0, 0), (0, rp)))[:, :, None]
    tail3 = jnp.pad(tail, ((0, 0), (0, rp)))[:, :, None]
    return pl.pallas_call(
        functools.partial(_ret_short_kernel, nb=nb, seq=seq),
        grid=(batch // nb,),
        in_specs=[
            pl.BlockSpec(memory_space=pltpu.SMEM),
            pl.BlockSpec((nb, seq, qw), lambda i: (i, 0, 0)),
            pl.BlockSpec((nb, seq, qw), lambda i: (i, 0, 1)),
            pl.BlockSpec((nb, seq, vw), lambda i: (i, 0, 1)),
            pl.BlockSpec((nb, seq, vw), lambda i: (i, 0, 2)),
            small(cos), small(sin), small(decay), small(inner3), small(tail3),
            _resident((1, vw)),
            pl.BlockSpec((nb, RET_HEADS, RET_DK, RET_DV), lambda i: (i, 0, 0, 0)),
        ],
        out_specs=[
            pl.BlockSpec((nb, seq, vw), lambda i: (i, 0, 0)),
            pl.BlockSpec((nb, RET_HEADS, RET_DK, RET_DV), lambda i: (i, 0, 0, 0)),
        ],
        out_shape=[jax.ShapeDtypeStruct((batch, seq, vw), BF16),
                   jax.ShapeDtypeStruct((batch, RET_HEADS, RET_DK, RET_DV), F32)],
        compiler_params=_cparams(("arbitrary",)),
        name="ret_short",
    )(full, z3, z3, z3, z3, cos, sin, decay, inner3, tail3, gn.reshape(1, -1), state0)


def _route(logits_t):
    n = logits_t.shape[1]
    epg = EXPERTS_PER_GROUP
    row = lax.broadcasted_iota(I32, (epg, n), 0).astype(F32)
    neg = jnp.float32(-1e30)
    big = jnp.float32(epg)
    is_g = row < N_GROUPS
    lg = jnp.where(is_g, logits_t[0:epg], neg)
    gmax = jnp.max(lg, axis=0, keepdims=True)
    gidx = jnp.min(jnp.where(lg == gmax, row, big), axis=0, keepdims=True)
    pg_sel = 1.0 / jnp.sum(jnp.where(is_g, jnp.exp(lg - gmax), 0.0), axis=0, keepdims=True)
    sel = logits_t[epg:2 * epg]
    for g in range(1, N_GROUPS):
        sel = jnp.where(gidx == g, logits_t[epg * (g + 1):epg * (g + 2)], sel)
    e = jnp.exp(sel - jnp.max(sel, axis=0, keepdims=True))
    pe = e / jnp.sum(e, axis=0, keepdims=True)
    t1 = jnp.max(pe, axis=0, keepdims=True)
    i1 = jnp.min(jnp.where(pe == t1, row, big), axis=0, keepdims=True)
    pe2 = jnp.where(row == i1, -1.0, pe)
    t2 = jnp.max(pe2, axis=0, keepdims=True)
    i2 = jnp.min(jnp.where(pe2 == t2, row, big), axis=0, keepdims=True)
    den = t1 + t2
    gates = jnp.where(row == i1, t1 / den * pg_sel, 0.0) + jnp.where(row == i2, t2 / den * pg_sel, 0.0)
    ids = jnp.where(row == 0, gidx, 0.0)
    block = jnp.concatenate([gates, ids, jnp.zeros((LANES - 2 * epg, n), F32)], axis=0)
    return block.T


def _outproj_kernel(x_ref, mix_ref, xa_ref, wa_ref, wb_ref, g_ref, wr_ref, o_ref, r_ref, *, sub):
    logits = []
    for s in range(x_ref.shape[0] // sub):
        rows = slice(s * sub, (s + 1) * sub)
        acc = jnp.dot(mix_ref[rows, :], wa_ref[...], preferred_element_type=F32)
        acc = acc + jnp.dot(xa_ref[rows, :], wb_ref[...], preferred_element_type=F32)
        xn = x_ref[rows, :] + acc
        o_ref[rows, :D_MODEL] = xn
        h2 = _rms(xn, g_ref[...])
        hi = h2.astype(BF16)
        lo = (h2 - hi.astype(F32)).astype(BF16)
        p_hi = lax.dot_general(wr_ref[...], hi, NT_DIMS, preferred_element_type=F32)
        p_lo = lax.dot_general(wr_ref[0:ROUTER_ROWS, :], lo, NT_DIMS, preferred_element_type=F32)
        logits.append((p_hi[:ROUTER_ROWS] + p_lo) + p_hi[ROUTER_ROWS:])
    route = _route(jnp.concatenate(logits, axis=1))
    o_ref[:, D_MODEL:] = route
    r_ref[...] = route


def _router_weights(w_group, w_router):
    epg = EXPERTS_PER_GROUP
    zeros = lambda r: jnp.zeros((DEPTH, r, D_MODEL), F32)
    wt = jnp.concatenate([jnp.swapaxes(w_group, 1, 2), zeros(epg - N_GROUPS), jnp.swapaxes(w_router, 1, 2),
                          zeros(ROUTER_ROWS - epg - N_EXPERTS)], axis=1)
    hi = wt.astype(BF16)
    return jnp.concatenate([hi, (wt - hi.astype(F32)).astype(BF16)], axis=1)


def _outproj_router(x, mix, xa, w_out, g_ffn, w_router2):
    m, cm = mix.shape
    tm = min(512, m)
    row = lambda w: pl.BlockSpec((tm, w), lambda i: (i, 0))
    return pl.pallas_call(
        functools.partial(_outproj_kernel, sub=min(128, tm)),
        grid=(m // tm,),
        in_specs=[row(D_MODEL), row(cm), row(D_MODEL), _resident((cm, D_MODEL)), _resident((D_MODEL, D_MODEL)),
                  _resident((1, D_MODEL)), _resident((2 * ROUTER_ROWS, D_MODEL))],
        out_specs=[row(AUG_W), row(LANES)],
        out_shape=[jax.ShapeDtypeStruct((m, AUG_W), F32), jax.ShapeDtypeStruct((m, LANES), F32)],
        compiler_params=_cparams(("arbitrary",)),
        name="outproj_router",
    )(x, mix, xa, w_out[:cm], w_out[cm:], g_ffn.reshape(1, D_MODEL), w_router2)


STEP_VALID, STEP_FIRST, STEP_LAST, STEP_NEWGROUP = 1, 2, 4, 8


def _moe_plan(gid, tm):
    m = gid.shape[0]
    ntile = m // tm
    nstep = ntile + N_GROUPS - 1
    counts = jnp.sum((gid[:, None] == jnp.arange(N_GROUPS, dtype=I32)[None, :]).astype(I32), axis=0)
    gend = jnp.cumsum(counts)
    gstart = gend - counts
    row_tok = jnp.argsort(gid, stable=True).astype(I32)
    lo = jnp.arange(ntile, dtype=I32)[:, None] * tm
    present = (gstart[None, :] < lo + tm) & (gend[None, :] > lo) & (counts[None, :] > 0)
    idx = jnp.nonzero(present.reshape(-1), size=nstep, fill_value=-1)[0].astype(I32)
    valid = idx >= 0
    idx = jnp.where(valid, idx, jnp.max(idx))
    st_tile = idx // N_GROUPS
    st_gid = idx % N_GROUPS
    prev_tile = jnp.concatenate([jnp.full((1,), -1, I32), st_tile[:-1]])
    next_tile = jnp.concatenate([st_tile[1:], jnp.full((1,), -1, I32)])
    next_valid = jnp.concatenate([valid[1:], jnp.zeros((1,), bool)])
    first = valid & (st_tile != prev_tile)
    last = valid & ((st_tile != next_tile) | ~next_valid)
    prev_gid = jnp.concatenate([jnp.full((1,), -1, I32), st_gid[:-1]])
    newgroup = valid & (st_gid != prev_gid)
    flags = valid * STEP_VALID + first * STEP_FIRST + last * STEP_LAST + newgroup * STEP_NEWGROUP
    return st_tile, st_gid, flags.astype(I32), row_tok


def _moe_kernel(tile_ref, gid_ref, flag_ref, tok_ref, xaug_hbm, g_ref, fin_ref, wgu_ref, wd_ref, out_hbm,
                xbuf, ybuf, wgu_bf, wd_bf, gsem, ssem, *, tm, ntile, final_norm):
    s = pl.program_id(0)
    tile = tile_ref[s]
    grp = gid_ref[s]
    flags = flag_ref[s]
    valid = (flags & STEP_VALID) > 0
    first = (flags & STEP_FIRST) > 0
    last = (flags & STEP_LAST) > 0
    slot = tile % 2

    @pl.when((flags & STEP_NEWGROUP) > 0)
    def _():
        for j in range(EXPERTS_PER_GROUP):
            wgu_bf[j] = wgu_ref[j].astype(BF16)
            wd_bf[j] = wd_ref[j].astype(BF16)

    def gather_start(t, sl):
        for r in range(tm):
            tok = tok_ref[t * tm + r]
            pltpu.make_async_copy(xaug_hbm.at[pl.ds(tok, 1)], xbuf.at[sl, pl.ds(r, 1)],
                                  gsem.at[sl]).start(priority=r % 2)

    def scatter_start(t, sl):
        for r in range(tm):
            tok = tok_ref[t * tm + r]
            pltpu.make_async_copy(ybuf.at[sl, pl.ds(r, 1)], out_hbm.at[pl.ds(tok, 1)],
                                  ssem.at[sl]).start(priority=r % 2)

    def gather_wait(sl):
        pltpu.make_async_copy(xaug_hbm.at[pl.ds(0, tm)], xbuf.at[sl], gsem.at[sl]).wait()

    def scatter_wait(sl):
        pltpu.make_async_copy(ybuf.at[sl], out_hbm.at[pl.ds(0, tm)], ssem.at[sl]).wait()

    @pl.when(s == 0)
    def _():
        gather_start(0, 0)

    @pl.when(first & (tile + 1 < ntile))
    def _():
        gather_start(tile + 1, 1 - slot)

    @pl.when(first)
    def _():
        gather_wait(slot)

        @pl.when(tile >= 2)
        def _():
            scatter_wait(slot)

        ybuf[slot] = xbuf[slot, :, 0:D_MODEL]

    @pl.when(valid)
    def _():
        xn = xbuf[slot, :, 0:D_MODEL]
        gl = xbuf[slot, :, GATE_LANE0:AUG_W]
        mine = gl[:, GID_LANE:GID_LANE + 1] == grp.astype(F32)
        h2 = _rms(xn, g_ref[...]).astype(BF16)
        acc = jnp.zeros((tm, D_MODEL), F32)
        for j in range(EXPERTS_PER_GROUP):
            gu = jnp.dot(h2, wgu_bf[j], preferred_element_type=F32)
            gate = jnp.where(mine, gl[:, j:j + 1], 0.0)
            act = _silu(gu[:, :EXPERT_FF]) * gu[:, EXPERT_FF:] * gate
            acc = acc + jnp.dot(act.astype(BF16), wd_bf[j], preferred_element_type=F32)
        ybuf[slot] = ybuf[slot] + acc

    @pl.when(last)
    def _():
        if final_norm:
            ybuf[slot] = _rms(ybuf[slot], fin_ref[...])
        scatter_start(tile, slot)

        @pl.when(tile == ntile - 1)
        def _():
            if ntile >= 2:
                scatter_wait(1 - slot)
            scatter_wait(slot)


def _moe(xaug, route, g_ffn, final_g, w_gu, w_down, layer, final_norm):
    m = xaug.shape[0]
    tm = min(MOE_TM, m)
    gid = route[:, GID_LANE].astype(I32)
    st_tile, st_gid, flags, row_tok = _moe_plan(gid, tm)
    epg = EXPERTS_PER_GROUP
    grid_spec = pltpu.PrefetchScalarGridSpec(
        num_scalar_prefetch=4,
        grid=(st_tile.shape[0],),
        in_specs=[
            pl.BlockSpec(memory_space=pl.ANY),
            pl.BlockSpec((1, D_MODEL), lambda s, *_: (0, 0)),
            pl.BlockSpec((1, D_MODEL), lambda s, *_: (0, 0)),
            pl.BlockSpec((None, epg, D_MODEL, 2 * EXPERT_FF), lambda s, tl, gd, fl, tok: (layer, gd[s], 0, 0),
                         pipeline_mode=pl.Buffered(1)),
            pl.BlockSpec((None, epg, EXPERT_FF, D_MODEL), lambda s, tl, gd, fl, tok: (layer, gd[s], 0, 0),
                         pipeline_mode=pl.Buffered(1)),
        ],
        out_specs=pl.BlockSpec(memory_space=pl.ANY),
        scratch_shapes=[
            pltpu.VMEM((2, tm, AUG_W), F32),
            pltpu.VMEM((2, tm, D_MODEL), F32),
            pltpu.VMEM((epg, D_MODEL, 2 * EXPERT_FF), BF16),
            pltpu.VMEM((epg, EXPERT_FF, D_MODEL), BF16),
            pltpu.SemaphoreType.DMA((2,)),
            pltpu.SemaphoreType.DMA((2,)),
        ],
    )
    return pl.pallas_call(
        functools.partial(_moe_kernel, tm=tm, ntile=m // tm, final_norm=final_norm),
        grid_spec=grid_spec,
        out_shape=jax.ShapeDtypeStruct((m, D_MODEL), F32),
        compiler_params=_cparams(("arbitrary",)),
        name="moe",
    )(st_tile, st_gid, flags, row_tok, xaug, g_ffn.reshape(1, D_MODEL), final_g.reshape(1, D_MODEL), w_gu, w_down)


def _trunk(x, pos, conv_state, ret_state, mem_k, mem_v, wts, long_seq):
    batch, seq = x.shape[0], x.shape[1]
    m = batch * seq
    x2 = x.reshape(m, D_MODEL)
    if long_seq:
        nb, tq = 1, min(512, seq)
    else:
        nb, tq = 4, seq
    new_conv = new_ret = None
    for i in range(DEPTH):
        if i % 2 == 0:
            z = _inproj(x2, wts["norm_mix"][i], wts["conv_w_in"])
            conv = _conv_long if long_seq else _conv_short
            mix, new_conv = conv(z, wts["conv_w"], conv_state, batch, seq)
            qcol = 3
            w_out = wts["conv_w_out"]
        else:
            z = _inproj(x2, wts["norm_mix"][i], wts["ret_w_in"])
            ret = _ret_long if long_seq else _ret_short
            mix, new_ret = ret(z.reshape(batch, seq, -1), ret_state, wts["ret_gn"], pos)
            mix = mix.reshape(m, -1)
            qcol = 6
            w_out = wts["ret_w_out"]
        xa = _xattn(z.reshape(batch, seq, -1), qcol, mem_k, mem_v, i, nb, tq).reshape(m, D_MODEL)
        xaug, route = _outproj_router(x2, mix, xa, w_out, wts["norm_ffn"][i], wts["w_router"][i])
        x2 = _moe(xaug, route, wts["norm_ffn"][i], wts["final_norm"], wts["moe_w_gate_up"], wts["moe_w_down"], i,
                  final_norm=(i == DEPTH - 1))
    return x2.reshape(batch, seq, D_MODEL), new_conv, new_ret


def kernel(x_prompt, x_sample, state_conv, state_ret, cache_mem_k, cache_mem_v, mem_prompt, norm_mix, norm_ffn,
           mem_norm, w_mem_kv, conv_w_in, conv_w, conv_w_out, ret_w_in, ret_gn, ret_w_out, moe_w_group,
           moe_w_router, moe_w_gate_up, moe_w_down, final_norm):
    batch, seq = x_prompt.shape[0], x_prompt.shape[1]
    dec_batch, dec_seq = x_sample.shape[0], x_sample.shape[1]
    w_router = _router_weights(moe_w_group, moe_w_router)
    wts = dict(
        norm_mix=norm_mix, norm_ffn=norm_ffn, final_norm=final_norm, w_router=w_router,
        conv_w_in=conv_w_in[0].astype(BF16), conv_w=conv_w[0], conv_w_out=conv_w_out[0].astype(BF16),
        ret_w_in=ret_w_in[0].astype(BF16), ret_gn=ret_gn[0], ret_w_out=ret_w_out[0].astype(BF16),
        moe_w_gate_up=moe_w_gate_up, moe_w_down=moe_w_down,
    )
    mem_k_prompt, mem_v_prompt = _mem_kv(mem_prompt.reshape(batch * N_MEM, D_MODEL), mem_norm, w_mem_kv.astype(BF16))

    y_prompt, conv_p, ret_p = _trunk(
        x_prompt, jnp.arange(seq, dtype=I32),
        jnp.zeros((batch, CONV_WIDTH - 1, D_MODEL), F32), jnp.zeros((batch, RET_HEADS, RET_DK, RET_DV), F32),
        mem_k_prompt, mem_v_prompt, wts, True)
    y_sample, conv_s, ret_s = _trunk(
        x_sample, PAST_LEN + jnp.arange(dec_seq, dtype=I32), state_conv[0], state_ret[0],
        cache_mem_k, cache_mem_v, wts, False)
    return (y_prompt, y_sample, conv_p[None], conv_s[None], ret_p[None], ret_s[None], mem_k_prompt, mem_v_prompt)
```

```python
import functools

import jax
import jax.numpy as jnp
from jax import lax
from jax.experimental import pallas as pl
from jax.experimental.pallas import tpu as pltpu

F32 = jnp.float32
BF16 = jnp.bfloat16
I32 = jnp.int32

D_MODEL = 1024
DEPTH = 2
CONV_WIDTH = 3
RET_HEADS = 4
RET_DK = 256
RET_DV = 512
RET_CHUNK = 128
ROPE_BASE = 10000.0
N_MEM = 256
XA_HEADS = 4
XA_DIM = 256
N_GROUPS = 4
EXPERTS_PER_GROUP = 8
N_EXPERTS = N_GROUPS * EXPERTS_PER_GROUP
EXPERT_FF = 256
EPS = 1e-6

LANES = 128
AUG_W = D_MODEL + LANES
GATE_LANE0 = D_MODEL
GID_LANE = EXPERTS_PER_GROUP
ROUTER_ROWS = 64
VMEM_LIMIT = 56 * 1024 * 1024
MOE_TM = 512
SHORT_ROWS = 8
PAST_LEN = 16384
NT_DIMS = (((1,), (1,)), ((), ()))
TN_DIMS = (((0,), (0,)), ((), ()))


def _cparams(sem):
    return pltpu.CompilerParams(dimension_semantics=sem, vmem_limit_bytes=VMEM_LIMIT)


def _rms(x, g):
    return x * lax.rsqrt(jnp.mean(x * x, axis=-1, keepdims=True) + EPS) * g


def _silu(x):
    return x * jax.nn.sigmoid(x)


def _resident(shape):
    nd = len(shape)
    return pl.BlockSpec(shape, lambda *_: (0,) * nd, pipeline_mode=pl.Buffered(1))


def _memkv_kernel(x_ref, g_ref, w_ref, k_ref, v_ref, *, nb):
    h = _rms(x_ref[...], g_ref[...]).astype(BF16)
    kv = jnp.dot(h, w_ref[...], preferred_element_type=F32)
    for b in range(nb):
        rows = slice(b * N_MEM, (b + 1) * N_MEM)
        for hd in range(XA_HEADS):
            k_ref[b, :, hd, :] = kv[rows, hd * XA_DIM:(hd + 1) * XA_DIM]
            v_ref[b, :, hd, :] = kv[rows, D_MODEL + hd * XA_DIM:D_MODEL + (hd + 1) * XA_DIM]


def _mem_kv(mem2d, mem_norm, w_kv):
    batch = mem2d.shape[0] // N_MEM
    nb = min(2, batch)
    tm = nb * N_MEM
    out = jax.ShapeDtypeStruct((DEPTH, batch, N_MEM, XA_HEADS, XA_DIM), F32)
    out_spec = pl.BlockSpec((None, nb, N_MEM, XA_HEADS, XA_DIM), lambda l, j: (l, j, 0, 0, 0))
    return pl.pallas_call(
        functools.partial(_memkv_kernel, nb=nb),
        grid=(DEPTH, batch // nb),
        in_specs=[
            pl.BlockSpec((tm, D_MODEL), lambda l, j: (j, 0)),
            pl.BlockSpec((None, 1, D_MODEL), lambda l, j: (l, 0, 0)),
            pl.BlockSpec((None, D_MODEL, 2 * D_MODEL), lambda l, j: (l, 0, 0)),
        ],
        out_specs=[out_spec, out_spec],
        out_shape=[out, out],
        compiler_params=_cparams(("arbitrary", "arbitrary")),
        name="mem_kv",
    )(mem2d, mem_norm.reshape(DEPTH, 1, D_MODEL), w_kv)


def _inproj_kernel(x_ref, g_ref, w_ref, o_ref, *, tn):
    h = _rms(x_ref[...], g_ref[...]).astype(BF16)
    for j in range(o_ref.shape[1] // tn):
        sl = slice(j * tn, (j + 1) * tn)
        o_ref[:, sl] = jnp.dot(h, w_ref[:, sl], preferred_element_type=F32).astype(BF16)


def _inproj(x, g, w):
    m, n = x.shape[0], w.shape[1]
    tm = min(512, m)
    return pl.pallas_call(
        functools.partial(_inproj_kernel, tn=512),
        grid=(m // tm,),
        in_specs=[
            pl.BlockSpec((tm, D_MODEL), lambda i: (i, 0)),
            _resident((1, D_MODEL)),
            _resident((D_MODEL, n)),
        ],
        out_specs=pl.BlockSpec((tm, n), lambda i: (i, 0)),
        out_shape=jax.ShapeDtypeStruct((m, n), BF16),
        compiler_params=_cparams(("arbitrary",)),
        name="inproj",
    )(x, g.reshape(1, D_MODEL), w)


def _conv_taps(u, prev1, prev2, w_ref):
    return (w_ref[0:1, :] * prev2 + w_ref[1:2, :] * prev1) + w_ref[2:3, :] * u


def _conv_long_kernel(b_ref, c_ref, v_ref, w_ref, st0_ref, mix_ref, st_ref, carry, *, tiles_per_seq):
    i = pl.program_id(0)
    tt = c_ref.shape[0]

    @pl.when(i % tiles_per_seq == 0)
    def _():
        carry[6:8, :] = st0_ref[0]

    u = c_ref[...].astype(F32) * v_ref[...].astype(F32)
    pm2 = carry[6:7, :]
    pm1 = carry[7:8, :]
    row = lax.broadcasted_iota(I32, u.shape, 0)
    prev1 = jnp.where(row == 0, pm1, pltpu.roll(u, 1, 0))
    prev2 = jnp.where(row == 0, pm2, jnp.where(row == 1, pm1, pltpu.roll(u, 2, 0)))
    mix_ref[...] = (b_ref[...].astype(F32) * _conv_taps(u, prev1, prev2, w_ref)).astype(BF16)
    carry[...] = u[tt - 8:tt, :]
    st_ref[0] = carry[6:8, :]


def _conv_long(z, conv_w, state0, batch, seq):
    tt = min(512, seq)
    tps = seq // tt
    col = lambda j: pl.BlockSpec((tt, D_MODEL), lambda i, j=j: (i, j))
    return pl.pallas_call(
        functools.partial(_conv_long_kernel, tiles_per_seq=tps),
        grid=(batch * tps,),
        in_specs=[col(0), col(1), col(2), _resident((CONV_WIDTH, D_MODEL)),
                  pl.BlockSpec((1, 2, D_MODEL), lambda i: (i // tps, 0, 0))],
        out_specs=[pl.BlockSpec((tt, D_MODEL), lambda i: (i, 0)),
                   pl.BlockSpec((1, 2, D_MODEL), lambda i: (i // tps, 0, 0))],
        out_shape=[jax.ShapeDtypeStruct((batch * seq, D_MODEL), BF16),
                   jax.ShapeDtypeStruct((batch, 2, D_MODEL), F32)],
        scratch_shapes=[pltpu.VMEM((8, D_MODEL), F32)],
        compiler_params=_cparams(("arbitrary",)),
        name="conv_long",
    )(z, z, z, conv_w, state0)


def _conv_short_kernel(b_ref, c_ref, v_ref, w_ref, p1_ref, p2_ref, mix_ref, u_ref, *, seq):
    u = c_ref[...].astype(F32) * v_ref[...].astype(F32)
    t = lax.broadcasted_iota(I32, u.shape, 0) % seq
    prev1 = jnp.where(t == 0, p1_ref[...], pltpu.roll(u, 1, 0))
    prev2 = jnp.where(t <= 1, p2_ref[...], pltpu.roll(u, 2, 0))
    mix_ref[...] = (b_ref[...].astype(F32) * _conv_taps(u, prev1, prev2, w_ref)).astype(BF16)
    u_ref[...] = u


def _conv_short(z, conv_w, state0, batch, seq):
    m = batch * seq
    zeros = jnp.zeros((batch, seq - 1, D_MODEL), F32)
    p1 = jnp.concatenate([state0[:, 1:2], zeros], axis=1).reshape(m, D_MODEL)
    p2 = jnp.concatenate([state0, zeros[:, 1:]], axis=1).reshape(m, D_MODEL)
    col = lambda j: pl.BlockSpec((m, D_MODEL), lambda i, j=j: (0, j))
    full = pl.BlockSpec((m, D_MODEL), lambda i: (0, 0))
    mix, u = pl.pallas_call(
        functools.partial(_conv_short_kernel, seq=seq),
        grid=(1,),
        in_specs=[col(0), col(1), col(2), _resident((CONV_WIDTH, D_MODEL)), full, full],
        out_specs=[full, full],
        out_shape=[jax.ShapeDtypeStruct((m, D_MODEL), BF16), jax.ShapeDtypeStruct((m, D_MODEL), F32)],
        compiler_params=_cparams(("arbitrary",)),
        name="conv_short",
    )(z, z, z, conv_w, p1, p2)
    return mix, u.reshape(batch, seq, D_MODEL)[:, seq - 2:]


def _xattn_kernel(q_ref, k_hbm, v_hbm, o_ref, kbuf, vbuf, sem, *, nb, layer, n_bt):
    i = pl.program_id(0)
    t = pl.program_id(1)
    slot = i % 2

    def copies(bt, sl):
        out = []
        for b in range(nb):
            for h in range(XA_HEADS):
                out.append(pltpu.make_async_copy(k_hbm.at[layer, bt * nb + b, :, h, :], kbuf.at[sl, b, h], sem.at[0, sl]))
                out.append(pltpu.make_async_copy(v_hbm.at[layer, bt * nb + b, :, h, :], vbuf.at[sl, b, h], sem.at[1, sl]))
        return out

    @pl.when((i == 0) & (t == 0))
    def _():
        for c in copies(0, 0):
            c.start()

    @pl.when(t == 0)
    def _():
        @pl.when(i + 1 < n_bt)
        def _():
            for c in copies(i + 1, 1 - slot):
                c.start()

        for c in copies(i, slot):
            c.wait()

    scale = XA_DIM ** -0.5
    tq = q_ref.shape[1]
    rows = -(-tq // SHORT_ROWS) * SHORT_ROWS
    pairs = [(b, h) for b in range(nb) for h in range(XA_HEADS)]
    scores = []
    for b, h in pairs:
        q = q_ref[b, :, h * XA_DIM:(h + 1) * XA_DIM]
        if rows != tq:
            q = jnp.concatenate([q.astype(F32), jnp.zeros((rows - tq, XA_DIM), F32)], axis=0).astype(BF16)
        k = kbuf[slot, b, h].astype(BF16)
        scores.append(lax.dot_general(q, k, NT_DIMS, preferred_element_type=F32) * scale)
    s = jnp.concatenate(scores, axis=0)
    p = jnp.exp(s - jnp.max(s, axis=-1, keepdims=True))
    p = (p / jnp.sum(p, axis=-1, keepdims=True)).astype(BF16)
    for n, (b, h) in enumerate(pairs):
        v = vbuf[slot, b, h].astype(BF16)
        o = jnp.dot(p[n * rows:(n + 1) * rows], v, preferred_element_type=F32)
        o_ref[b, :, h * XA_DIM:(h + 1) * XA_DIM] = o[:tq].astype(BF16)


def _xattn(z3, qcol, mem_k, mem_v, layer, nb, tq):
    batch, seq = z3.shape[0], z3.shape[1]
    n_bt = batch // nb
    plane = pltpu.VMEM((2, nb, XA_HEADS, N_MEM, XA_DIM), F32)
    return pl.pallas_call(
        functools.partial(_xattn_kernel, nb=nb, layer=layer, n_bt=n_bt),
        grid=(n_bt, seq // tq),
        in_specs=[pl.BlockSpec((nb, tq, D_MODEL), lambda i, t: (i, t, qcol)),
                  pl.BlockSpec(memory_space=pl.ANY), pl.BlockSpec(memory_space=pl.ANY)],
        out_specs=pl.BlockSpec((nb, tq, D_MODEL), lambda i, t: (i, t, 0)),
        out_shape=jax.ShapeDtypeStruct((batch, seq, D_MODEL), BF16),
        scratch_shapes=[plane, plane, pltpu.SemaphoreType.DMA((2, 2))],
        compiler_params=_cparams(("arbitrary", "arbitrary")),
        name="xattn",
    )(z3, mem_k, mem_v)


def _rotary(x, cos, sin):
    half = x.shape[-1] // 2
    x1, x2 = x[:, :half], x[:, half:]
    return jnp.concatenate([x1 * cos - x2 * sin, x1 * sin + x2 * cos], axis=-1)


def _group_norm_gate(o, gate, gn):
    mu = jnp.mean(o, axis=-1, keepdims=True)
    var = jnp.mean(jnp.square(o - mu), axis=-1, keepdims=True)
    return _silu(gate) * ((o - mu) * lax.rsqrt(var + EPS) * gn)


def _ret_long_kernel(full_ref, q_ref, k_ref, v_ref, g_ref, cos_ref, sin_ref, dec_ref, inn_ref, tail_ref,
                     gn_ref, mix_ref, sfin_ref, s_acc, *, chunk):
    t = pl.program_id(1)
    heads = range(RET_HEADS)

    @pl.when(t == 0)
    def _():
        s_acc[...] = jnp.zeros_like(s_acc)

    cos, sin = cos_ref[...], sin_ref[...]
    qs = [_rotary(q_ref[0, :, h * RET_DK:(h + 1) * RET_DK].astype(F32), cos, sin) for h in heads]
    ks = [_rotary(k_ref[0, :, h * RET_DK:(h + 1) * RET_DK].astype(F32), cos, sin) * (RET_DK ** -0.5) for h in heads]
    for c in range(q_ref.shape[1] // chunk):
        sl = slice(c * chunk, (c + 1) * chunk)
        qc = [qs[h][sl].astype(BF16) for h in heads]
        kc = [ks[h][sl] for h in heads]
        vc = [v_ref[0, sl, h * RET_DV:(h + 1) * RET_DV] for h in heads]
        s_old = [s_acc[h] for h in heads]
        scores = [lax.dot_general(qc[h], kc[h].astype(BF16), NT_DIMS, preferred_element_type=F32) * dec_ref[h]
                  for h in heads]
        cross = [jnp.dot(qc[h], s_old[h].astype(BF16), preferred_element_type=F32) * inn_ref[h][:, 0:1]
                 for h in heads]
        o = [jnp.dot(scores[h].astype(BF16), vc[h], preferred_element_type=F32) + cross[h] for h in heads]
        for h in heads:
            kt = (kc[h] * tail_ref[h][:, 0:1]).astype(BF16)
            s_acc[h] = full_ref[h] * s_old[h] + lax.dot_general(kt, vc[h], TN_DIMS, preferred_element_type=F32)
        og = jnp.concatenate(o, axis=0)
        mu = jnp.mean(og, axis=-1, keepdims=True)
        var = jnp.mean(jnp.square(og - mu), axis=-1, keepdims=True)
        on = (og - mu) * lax.rsqrt(var + EPS)
        for h in heads:
            vs = slice(h * RET_DV, (h + 1) * RET_DV)
            gate = g_ref[0, sl, vs]
            mix_ref[0, sl, vs] = _silu(gate) * (on[h * chunk:(h + 1) * chunk] * gn_ref[:, vs]).astype(BF16)

    @pl.when(t == pl.num_programs(1) - 1)
    def _():
        sfin_ref[0] = s_acc[...]


def _ret_tables(pos, chunk):
    half = RET_DK // 2
    inv = ROPE_BASE ** (-jnp.arange(half, dtype=F32) / half)
    ang = pos.astype(F32)[:, None] * inv[None, :]
    log_g = jnp.log1p(-jnp.exp2(-5.0 - jnp.arange(RET_HEADS, dtype=F32)))
    idx = jnp.arange(chunk)
    diff = idx[:, None] - idx[None, :]
    decay = jnp.where(diff[None] >= 0,
                      jnp.exp(jnp.maximum(diff, 0)[None].astype(F32) * log_g[:, None, None]), 0.0)
    inner = jnp.exp((idx + 1).astype(F32)[None, :] * log_g[:, None])
    tail = jnp.exp((chunk - 1 - idx).astype(F32)[None, :] * log_g[:, None])
    full = jnp.exp(chunk * log_g)
    return jnp.cos(ang), jnp.sin(ang), decay, inner, tail, full


def _ret_long(z3, state0, gn, pos):
    assert state0 is None, "the long-sequence path starts from the zero state"
    batch, seq = z3.shape[0], z3.shape[1]
    tc = min(512, seq)
    chunk = 2 * RET_CHUNK if tc % (2 * RET_CHUNK) == 0 else RET_CHUNK
    cos, sin, decay, inner, tail, full = _ret_tables(pos, chunk)
    lane_b = lambda a: jnp.broadcast_to(a[:, :, None], (RET_HEADS, chunk, LANES))
    qw = RET_HEADS * RET_DK
    vw = RET_HEADS * RET_DV
    state_spec = pl.BlockSpec((1, RET_HEADS, RET_DK, RET_DV), lambda b, t: (b, 0, 0, 0))
    return pl.pallas_call(
        functools.partial(_ret_long_kernel, chunk=chunk),
        grid=(batch, seq // tc),
        in_specs=[
            pl.BlockSpec(memory_space=pltpu.SMEM),
            pl.BlockSpec((1, tc, qw), lambda b, t: (b, t, 0)),
            pl.BlockSpec((1, tc, qw), lambda b, t: (b, t, 1)),
            pl.BlockSpec((1, tc, vw), lambda b, t: (b, t, 1)),
            pl.BlockSpec((1, tc, vw), lambda b, t: (b, t, 2)),
            pl.BlockSpec((tc, RET_DK // 2), lambda b, t: (t, 0)),
            pl.BlockSpec((tc, RET_DK // 2), lambda b, t: (t, 0)),
            _resident((RET_HEADS, chunk, chunk)), _resident((RET_HEADS, chunk, LANES)),
            _resident((RET_HEADS, chunk, LANES)), _resident((1, vw)),
        ],
        out_specs=[pl.BlockSpec((1, tc, vw), lambda b, t: (b, t, 0)), state_spec],
        out_shape=[jax.ShapeDtypeStruct((batch, seq, vw), BF16),
                   jax.ShapeDtypeStruct((batch, RET_HEADS, RET_DK, RET_DV), F32)],
        scratch_shapes=[pltpu.VMEM((RET_HEADS, RET_DK, RET_DV), F32)],
        compiler_params=_cparams(("arbitrary", "arbitrary")),
        name="ret_long",
    )(full, z3, z3, z3, z3, cos, sin, decay, lane_b(inner), lane_b(tail), gn.reshape(1, -1))


def _ret_short_kernel(full_ref, q_ref, k_ref, v_ref, g_ref, cos_ref, sin_ref, dec_ref, inn_ref, tail_ref,
                      gn_ref, s0_ref, mix_ref, snew_ref, *, nb, seq):
    def pad_rows(a):
        return jnp.concatenate([a, jnp.zeros((SHORT_ROWS - seq, a.shape[1]), F32)], axis=0)

    cos, sin = cos_ref[...], sin_ref[...]
    for b in range(nb):
        for h in range(RET_HEADS):
            qs = slice(h * RET_DK, (h + 1) * RET_DK)
            vs = slice(h * RET_DV, (h + 1) * RET_DV)
            q = pad_rows(_rotary(q_ref[b, :, qs].astype(F32), cos, sin)).astype(BF16)
            k = pad_rows(_rotary(k_ref[b, :, qs].astype(F32), cos, sin) * (RET_DK ** -0.5))
            v = pad_rows(v_ref[b, :, vs].astype(F32)).astype(BF16)
            s_old = s0_ref[b, h]
            scores = lax.dot_general(q, k.astype(BF16), NT_DIMS, preferred_element_type=F32) * dec_ref[h]
            o = jnp.dot(scores.astype(BF16), v, preferred_element_type=F32)
            o = o + jnp.dot(q, s_old.astype(BF16), preferred_element_type=F32) * inn_ref[h]
            snew_ref[b, h] = full_ref[h] * s_old + lax.dot_general((k * tail_ref[h]).astype(BF16), v, TN_DIMS,
                                                                   preferred_element_type=F32)
            mix_ref[b, :, vs] = _group_norm_gate(o[:seq], g_ref[b, :, vs].astype(F32), gn_ref[:, vs]).astype(BF16)


def _ret_short(z3, state0, gn, pos, nb=2):
    batch, seq = z3.shape[0], z3.shape[1]
    cos, sin, decay, inner, tail, full = _ret_tables(pos, seq)
    qw = RET_HEADS * RET_DK
    vw = RET_HEADS * RET_DV
    small = lambda a: _resident(a.shape)
    rp = SHORT_ROWS - seq
    decay = jnp.pad(decay, ((0, 0), (0, rp), (0, rp)))
    inner3 = jnp.pad(inner, ((0, 0), (0, rp)))[:, :, None]
    tail3 = jnp.pad(tail, ((0, 0), (0, rp)))[:, :, None]
    return pl.pallas_call(
        functools.partial(_ret_short_kernel, nb=nb, seq=seq),
        grid=(batch // nb,),
        in_specs=[
            pl.BlockSpec(memory_space=pltpu.SMEM),
            pl.BlockSpec((nb, seq, qw), lambda i: (i, 0, 0)),
            pl.BlockSpec((nb, seq, qw), lambda i: (i, 0, 1)),
            pl.BlockSpec((nb, seq, vw), lambda i: (i, 0, 1)),
            pl.BlockSpec((nb, seq, vw), lambda i: (i, 0, 2)),
            small(cos), small(sin), small(decay), small(inner3), small(tail3),
            _resident((1, vw)),
            pl.BlockSpec((nb, RET_HEADS, RET_DK, RET_DV), lambda i: (i, 0, 0, 0)),
        ],
        out_specs=[
            pl.BlockSpec((nb, seq, vw), lambda i: (i, 0, 0)),
            pl.BlockSpec((nb, RET_HEADS, RET_DK, RET_DV), lambda i: (i, 0, 0, 0)),
        ],
        out_shape=[jax.ShapeDtypeStruct((batch, seq, vw), BF16),
                   jax.ShapeDtypeStruct((batch, RET_HEADS, RET_DK, RET_DV), F32)],
        compiler_params=_cparams(("arbitrary",)),
        name="ret_short",
    )(full, z3, z3, z3, z3, cos, sin, decay, inner3, tail3, gn.reshape(1, -1), state0)


def _route(logits_t):
    n = logits_t.shape[1]
    epg = EXPERTS_PER_GROUP
    row = lax.broadcasted_iota(I32, (epg, n), 0).astype(F32)
    neg = jnp.float32(-1e30)
    big = jnp.float32(epg)
    is_g = row < N_GROUPS
    lg = jnp.where(is_g, logits_t[0:epg], neg)
    gmax = jnp.max(lg, axis=0, keepdims=True)
    gidx = jnp.min(jnp.where(lg == gmax, row, big), axis=0, keepdims=True)
    pg_sel = 1.0 / jnp.sum(jnp.where(is_g, jnp.exp(lg - gmax), 0.0), axis=0, keepdims=True)
    sel = logits_t[epg:2 * epg]
    for g in range(1, N_GROUPS):
        sel = jnp.where(gidx == g, logits_t[epg * (g + 1):epg * (g + 2)], sel)
    e = jnp.exp(sel - jnp.max(sel, axis=0, keepdims=True))
    pe = e / jnp.sum(e, axis=0, keepdims=True)
    t1 = jnp.max(pe, axis=0, keepdims=True)
    i1 = jnp.min(jnp.where(pe == t1, row, big), axis=0, keepdims=True)
    pe2 = jnp.where(row == i1, -1.0, pe)
    t2 = jnp.max(pe2, axis=0, keepdims=True)
    i2 = jnp.min(jnp.where(pe2 == t2, row, big), axis=0, keepdims=True)
    den = t1 + t2
    gates = jnp.where(row == i1, t1 / den * pg_sel, 0.0) + jnp.where(row == i2, t2 / den * pg_sel, 0.0)
    ids = jnp.where(row == 0, gidx, 0.0)
    block = jnp.concatenate([gates, ids, jnp.zeros((LANES - 2 * epg, n), F32)], axis=0)
    return block.T


def _outproj_kernel(x_ref, mix_ref, xa_ref, wa_ref, wb_ref, g_ref, wr_ref, o_ref, r_ref, *, sub):
    acc = jnp.dot(mix_ref[...], wa_ref[...], preferred_element_type=F32)
    acc = acc + jnp.dot(xa_ref[...], wb_ref[...], preferred_element_type=F32)
    o_ref[:, :D_MODEL] = x_ref[...] + acc
    logits = []
    for s in range(x_ref.shape[0] // sub):
        rows = slice(s * sub, (s + 1) * sub)
        h2 = _rms(o_ref[rows, :D_MODEL], g_ref[...])
        hi = h2.astype(BF16)
        lo = (h2 - hi.astype(F32)).astype(BF16)
        p_hi = lax.dot_general(wr_ref[...], hi, NT_DIMS, preferred_element_type=F32)
        p_lo = lax.dot_general(wr_ref[0:ROUTER_ROWS, :], lo, NT_DIMS, preferred_element_type=F32)
        logits.append((p_hi[:ROUTER_ROWS] + p_lo) + p_hi[ROUTER_ROWS:])
    route = _route(jnp.concatenate(logits, axis=1))
    o_ref[:, D_MODEL:] = route
    r_ref[...] = route


def _router_weights(w_group, w_router):
    epg = EXPERTS_PER_GROUP
    zeros = lambda r: jnp.zeros((DEPTH, r, D_MODEL), F32)
    wt = jnp.concatenate([jnp.swapaxes(w_group, 1, 2), zeros(epg - N_GROUPS), jnp.swapaxes(w_router, 1, 2),
                          zeros(ROUTER_ROWS - epg - N_EXPERTS)], axis=1)
    hi = wt.astype(BF16)
    return jnp.concatenate([hi, (wt - hi.astype(F32)).astype(BF16)], axis=1)


def _outproj_router(x, mix, xa, w_out, g_ffn, w_router2):
    m, cm = mix.shape
    tm = min(512, m)
    row = lambda w: pl.BlockSpec((tm, w), lambda i: (i, 0))
    return pl.pallas_call(
        functools.partial(_outproj_kernel, sub=min(128, tm)),
        grid=(m // tm,),
        in_specs=[row(D_MODEL), row(cm), row(D_MODEL), _resident((cm, D_MODEL)), _resident((D_MODEL, D_MODEL)),
                  _resident((1, D_MODEL)), _resident((2 * ROUTER_ROWS, D_MODEL))],
        out_specs=[row(AUG_W), row(LANES)],
        out_shape=[jax.ShapeDtypeStruct((m, AUG_W), F32), jax.ShapeDtypeStruct((m, LANES), F32)],
        compiler_params=_cparams(("arbitrary",)),
        name="outproj_router",
    )(x, mix, xa, w_out[:cm], w_out[cm:], g_ffn.reshape(1, D_MODEL), w_router2)


STEP_VALID, STEP_FIRST, STEP_LAST, STEP_NEWGROUP = 1, 2, 4, 8


def _moe_plan(gid, tm):
    m = gid.shape[0]
    ntile = m // tm
    nstep = ntile + N_GROUPS - 1
    counts = jnp.sum((gid[:, None] == jnp.arange(N_GROUPS, dtype=I32)[None, :]).astype(I32), axis=0)
    gend = jnp.cumsum(counts)
    gstart = gend - counts
    row_tok = jnp.argsort(gid, stable=True).astype(I32)
    lo = jnp.arange(ntile, dtype=I32)[:, None] * tm
    present = (gstart[None, :] < lo + tm) & (gend[None, :] > lo) & (counts[None, :] > 0)
    idx = jnp.nonzero(present.reshape(-1), size=nstep, fill_value=-1)[0].astype(I32)
    valid = idx >= 0
    idx = jnp.where(valid, idx, jnp.max(idx))
    st_tile = idx // N_GROUPS
    st_gid = idx % N_GROUPS
    prev_tile = jnp.concatenate([jnp.full((1,), -1, I32), st_tile[:-1]])
    next_tile = jnp.concatenate([st_tile[1:], jnp.full((1,), -1, I32)])
    next_valid = jnp.concatenate([valid[1:], jnp.zeros((1,), bool)])
    first = valid & (st_tile != prev_tile)
    last = valid & ((st_tile != next_tile) | ~next_valid)
    prev_gid = jnp.concatenate([jnp.full((1,), -1, I32), st_gid[:-1]])
    newgroup = valid & (st_gid != prev_gid)
    flags = valid * STEP_VALID + first * STEP_FIRST + last * STEP_LAST + newgroup * STEP_NEWGROUP
    return st_tile, st_gid, flags.astype(I32), row_tok


def _moe_kernel(tile_ref, gid_ref, flag_ref, tok_ref, xaug_hbm, g_ref, fin_ref, wgu_ref, wd_ref, out_hbm,
                xbuf, ybuf, wgu_bf, wd_bf, gsem, ssem, *, tm, ntile, final_norm):
    s = pl.program_id(0)
    tile = tile_ref[s]
    grp = gid_ref[s]
    flags = flag_ref[s]
    valid = (flags & STEP_VALID) > 0
    first = (flags & STEP_FIRST) > 0
    last = (flags & STEP_LAST) > 0
    slot = tile % 2

    @pl.when((flags & STEP_NEWGROUP) > 0)
    def _():
        for j in range(EXPERTS_PER_GROUP):
            wgu_bf[j] = wgu_ref[j].astype(BF16)
            wd_bf[j] = wd_ref[j].astype(BF16)

    def gather_start(t, sl):
        for r in range(tm):
            tok = tok_ref[t * tm + r]
            pltpu.make_async_copy(xaug_hbm.at[pl.ds(tok, 1)], xbuf.at[sl, pl.ds(r, 1)],
                                  gsem.at[sl]).start(priority=r % 2)

    def scatter_start(t, sl):
        for r in range(tm):
            tok = tok_ref[t * tm + r]
            pltpu.make_async_copy(ybuf.at[sl, pl.ds(r, 1)], out_hbm.at[pl.ds(tok, 1)],
                                  ssem.at[sl]).start(priority=r % 2)

    def gather_wait(sl):
        pltpu.make_async_copy(xaug_hbm.at[pl.ds(0, tm)], xbuf.at[sl], gsem.at[sl]).wait()

    def scatter_wait(sl):
        pltpu.make_async_copy(ybuf.at[sl], out_hbm.at[pl.ds(0, tm)], ssem.at[sl]).wait()

    @pl.when(s == 0)
    def _():
        gather_start(0, 0)

    @pl.when(first & (tile + 1 < ntile))
    def _():
        gather_start(tile + 1, 1 - slot)

    @pl.when(first)
    def _():
        gather_wait(slot)

        @pl.when(tile >= 2)
        def _():
            scatter_wait(slot)

        ybuf[slot] = xbuf[slot, :, 0:D_MODEL]

    @pl.when(valid)
    def _():
        xn = xbuf[slot, :, 0:D_MODEL]
        gl = xbuf[slot, :, GATE_LANE0:AUG_W]
        mine = gl[:, GID_LANE:GID_LANE + 1] == grp.astype(F32)
        h2 = _rms(xn, g_ref[...]).astype(BF16)
        acc = jnp.zeros((tm, D_MODEL), F32)
        for j in range(EXPERTS_PER_GROUP):
            gu = jnp.dot(h2, wgu_bf[j], preferred_element_type=F32)
            gate = jnp.where(mine, gl[:, j:j + 1], 0.0)
            act = _silu(gu[:, :EXPERT_FF]) * gu[:, EXPERT_FF:] * gate
            acc = acc + jnp.dot(act.astype(BF16), wd_bf[j], preferred_element_type=F32)
        ybuf[slot] = ybuf[slot] + acc

    @pl.when(last)
    def _():
        if final_norm:
            ybuf[slot] = _rms(ybuf[slot], fin_ref[...])
        scatter_start(tile, slot)

        @pl.when(tile == ntile - 1)
        def _():
            if ntile >= 2:
                scatter_wait(1 - slot)
            scatter_wait(slot)


def _moe(xaug, route, g_ffn, final_g, w_gu, w_down, layer, final_norm):
    m = xaug.shape[0]
    tm = min(MOE_TM, m)
    gid = route[:, GID_LANE].astype(I32)
    st_tile, st_gid, flags, row_tok = _moe_plan(gid, tm)
    epg = EXPERTS_PER_GROUP
    grid_spec = pltpu.PrefetchScalarGridSpec(
        num_scalar_prefetch=4,
        grid=(st_tile.shape[0],),
        in_specs=[
            pl.BlockSpec(memory_space=pl.ANY),
            pl.BlockSpec((1, D_MODEL), lambda s, *_: (0, 0)),
            pl.BlockSpec((1, D_MODEL), lambda s, *_: (0, 0)),
            pl.BlockSpec((None, epg, D_MODEL, 2 * EXPERT_FF), lambda s, tl, gd, fl, tok: (layer, gd[s], 0, 0),
                         pipeline_mode=pl.Buffered(1)),
            pl.BlockSpec((None, epg, EXPERT_FF, D_MODEL), lambda s, tl, gd, fl, tok: (layer, gd[s], 0, 0),
                         pipeline_mode=pl.Buffered(1)),
        ],
        out_specs=pl.BlockSpec(memory_space=pl.ANY),
        scratch_shapes=[
            pltpu.VMEM((2, tm, AUG_W), F32),
            pltpu.VMEM((2, tm, D_MODEL), F32),
            pltpu.VMEM((epg, D_MODEL, 2 * EXPERT_FF), BF16),
            pltpu.VMEM((epg, EXPERT_FF, D_MODEL), BF16),
            pltpu.SemaphoreType.DMA((2,)),
            pltpu.SemaphoreType.DMA((2,)),
        ],
    )
    return pl.pallas_call(
        functools.partial(_moe_kernel, tm=tm, ntile=m // tm, final_norm=final_norm),
        grid_spec=grid_spec,
        out_shape=jax.ShapeDtypeStruct((m, D_MODEL), F32),
        compiler_params=_cparams(("arbitrary",)),
        name="moe",
    )(st_tile, st_gid, flags, row_tok, xaug, g_ffn.reshape(1, D_MODEL), final_g.reshape(1, D_MODEL), w_gu, w_down)


def _trunk(x, pos, conv_state, ret_state, mem_k, mem_v, wts, long_seq):
    batch, seq = x.shape[0], x.shape[1]
    m = batch * seq
    x2 = x.reshape(m, D_MODEL)
    if long_seq:
        nb, tq = 1, min(512, seq)
    else:
        nb, tq = 4, seq
    new_conv = new_ret = None
    for i in range(DEPTH):
        if i % 2 == 0:
            z = _inproj(x2, wts["norm_mix"][i], wts["conv_w_in"])
            conv = _conv_long if long_seq else _conv_short
            mix, new_conv = conv(z, wts["conv_w"], conv_state, batch, seq)
            qcol = 3
            w_out = wts["conv_w_out"]
        else:
            z = _inproj(x2, wts["norm_mix"][i], wts["ret_w_in"])
            ret = _ret_long if long_seq else _ret_short
            mix, new_ret = ret(z.reshape(batch, seq, -1), ret_state, wts["ret_gn"], pos)
            mix = mix.reshape(m, -1)
            qcol = 6
            w_out = wts["ret_w_out"]
        xa = _xattn(z.reshape(batch, seq, -1), qcol, mem_k, mem_v, i, nb, tq).reshape(m, D_MODEL)
        xaug, route = _outproj_router(x2, mix, xa, w_out, wts["norm_ffn"][i], wts["w_router"][i])
        x2 = _moe(xaug, route, wts["norm_ffn"][i], wts["final_norm"], wts["moe_w_gate_up"], wts["moe_w_down"], i,
                  final_norm=(i == DEPTH - 1))
    return x2.reshape(batch, seq, D_MODEL), new_conv, new_ret


def kernel(x_prompt, x_sample, state_conv, state_ret, cache_mem_k, cache_mem_v, mem_prompt, norm_mix, norm_ffn,
           mem_norm, w_mem_kv, conv_w_in, conv_w, conv_w_out, ret_w_in, ret_gn, ret_w_out, moe_w_group,
           moe_w_router, moe_w_gate_up, moe_w_down, final_norm):
    batch, seq = x_prompt.shape[0], x_prompt.shape[1]
    dec_batch, dec_seq = x_sample.shape[0], x_sample.shape[1]
    w_router = _router_weights(moe_w_group, moe_w_router)
    wts = dict(
        norm_mix=norm_mix, norm_ffn=norm_ffn, final_norm=final_norm, w_router=w_router,
        conv_w_in=conv_w_in[0].astype(BF16), conv_w=conv_w[0], conv_w_out=conv_w_out[0].astype(BF16),
        ret_w_in=ret_w_in[0].astype(BF16), ret_gn=ret_gn[0], ret_w_out=ret_w_out[0].astype(BF16),
        moe_w_gate_up=moe_w_gate_up, moe_w_down=moe_w_down,
    )
    mem_k_prompt, mem_v_prompt = _mem_kv(mem_prompt.reshape(batch * N_MEM, D_MODEL), mem_norm, w_mem_kv.astype(BF16))

    y_prompt, conv_p, ret_p = _trunk(
        x_prompt, jnp.arange(seq, dtype=I32),
        jnp.zeros((batch, CONV_WIDTH - 1, D_MODEL), F32), None,
        mem_k_prompt, mem_v_prompt, wts, True)
    y_sample, conv_s, ret_s = _trunk(
        x_sample, PAST_LEN + jnp.arange(dec_seq, dtype=I32), state_conv[0], state_ret[0],
        cache_mem_k, cache_mem_v, wts, False)
    return (y_prompt, y_sample, conv_p[None], conv_s[None], ret_p[None], ret_s[None], mem_k_prompt, mem_v_prompt)
```

```python
import functools

import jax
import jax.numpy as jnp
from jax import lax
from jax.experimental import pallas as pl
from jax.experimental.pallas import tpu as pltpu

F32 = jnp.float32
BF16 = jnp.bfloat16
I32 = jnp.int32

D_MODEL = 1024
DEPTH = 2
CONV_WIDTH = 3
RET_HEADS = 4
RET_DK = 256
RET_DV = 512
RET_CHUNK = 128
ROPE_BASE = 10000.0
N_MEM = 256
XA_HEADS = 4
XA_DIM = 256
N_GROUPS = 4
EXPERTS_PER_GROUP = 8
N_EXPERTS = N_GROUPS * EXPERTS_PER_GROUP
EXPERT_FF = 256
EPS = 1e-6

LANES = 128
AUG_W = D_MODEL + LANES
GATE_LANE0 = D_MODEL
GID_LANE = EXPERTS_PER_GROUP
ROUTER_ROWS = 64
VMEM_LIMIT = 56 * 1024 * 1024
MOE_TM = 512
SHORT_ROWS = 8
PAST_LEN = 16384
NT_DIMS = (((1,), (1,)), ((), ()))
TN_DIMS = (((0,), (0,)), ((), ()))


def _cparams(sem):
    return pltpu.CompilerParams(dimension_semantics=sem, vmem_limit_bytes=VMEM_LIMIT)


def _rms(x, g):
    return x * lax.rsqrt(jnp.mean(x * x, axis=-1, keepdims=True) + EPS) * g


def _silu(x):
    return x * jax.nn.sigmoid(x)


def _resident(shape):
    nd = len(shape)
    return pl.BlockSpec(shape, lambda *_: (0,) * nd, pipeline_mode=pl.Buffered(1))


def _memkv_kernel(x_ref, g_ref, w_ref, k_ref, v_ref, *, nb):
    h = _rms(x_ref[...], g_ref[...]).astype(BF16)
    kv = jnp.dot(h, w_ref[...], preferred_element_type=F32)
    for b in range(nb):
        rows = slice(b * N_MEM, (b + 1) * N_MEM)
        for hd in range(XA_HEADS):
            k_ref[b, :, hd, :] = kv[rows, hd * XA_DIM:(hd + 1) * XA_DIM]
            v_ref[b, :, hd, :] = kv[rows, D_MODEL + hd * XA_DIM:D_MODEL + (hd + 1) * XA_DIM]


def _mem_kv(mem2d, mem_norm, w_kv):
    batch = mem2d.shape[0] // N_MEM
    nb = min(2, batch)
    tm = nb * N_MEM
    out = jax.ShapeDtypeStruct((DEPTH, batch, N_MEM, XA_HEADS, XA_DIM), F32)
    out_spec = pl.BlockSpec((None, nb, N_MEM, XA_HEADS, XA_DIM), lambda l, j: (l, j, 0, 0, 0))
    return pl.pallas_call(
        functools.partial(_memkv_kernel, nb=nb),
        grid=(DEPTH, batch // nb),
        in_specs=[
            pl.BlockSpec((tm, D_MODEL), lambda l, j: (j, 0)),
            pl.BlockSpec((None, 1, D_MODEL), lambda l, j: (l, 0, 0)),
            pl.BlockSpec((None, D_MODEL, 2 * D_MODEL), lambda l, j: (l, 0, 0)),
        ],
        out_specs=[out_spec, out_spec],
        out_shape=[out, out],
        compiler_params=_cparams(("arbitrary", "arbitrary")),
        name="mem_kv",
    )(mem2d, mem_norm.reshape(DEPTH, 1, D_MODEL), w_kv)


def _inproj_kernel(x_ref, g_ref, w_ref, o_ref, *, tn):
    h = _rms(x_ref[...], g_ref[...]).astype(BF16)
    for j in range(o_ref.shape[1] // tn):
        sl = slice(j * tn, (j + 1) * tn)
        o_ref[:, sl] = jnp.dot(h, w_ref[:, sl], preferred_element_type=F32).astype(BF16)


def _inproj(x, g, w):
    m, n = x.shape[0], w.shape[1]
    tm = min(512, m)
    return pl.pallas_call(
        functools.partial(_inproj_kernel, tn=512),
        grid=(m // tm,),
        in_specs=[
            pl.BlockSpec((tm, D_MODEL), lambda i: (i, 0)),
            _resident((1, D_MODEL)),
            _resident((D_MODEL, n)),
        ],
        out_specs=pl.BlockSpec((tm, n), lambda i: (i, 0)),
        out_shape=jax.ShapeDtypeStruct((m, n), BF16),
        compiler_params=_cparams(("arbitrary",)),
        name="inproj",
    )(x, g.reshape(1, D_MODEL), w)


def _conv_taps(u, prev1, prev2, w_ref):
    return (w_ref[0:1, :] * prev2 + w_ref[1:2, :] * prev1) + w_ref[2:3, :] * u


def _conv_long_kernel(b_ref, c_ref, v_ref, w_ref, st0_ref, mix_ref, st_ref, carry, *, tiles_per_seq):
    i = pl.program_id(0)
    tt = c_ref.shape[0]

    @pl.when(i % tiles_per_seq == 0)
    def _():
        carry[6:8, :] = st0_ref[0]

    u = c_ref[...].astype(F32) * v_ref[...].astype(F32)
    pm2 = carry[6:7, :]
    pm1 = carry[7:8, :]
    row = lax.broadcasted_iota(I32, u.shape, 0)
    prev1 = jnp.where(row == 0, pm1, pltpu.roll(u, 1, 0))
    prev2 = jnp.where(row == 0, pm2, jnp.where(row == 1, pm1, pltpu.roll(u, 2, 0)))
    mix_ref[...] = (b_ref[...].astype(F32) * _conv_taps(u, prev1, prev2, w_ref)).astype(BF16)
    carry[...] = u[tt - 8:tt, :]
    st_ref[0] = carry[6:8, :]


def _conv_long(z, conv_w, state0, batch, seq):
    tt = min(512, seq)
    tps = seq // tt
    col = lambda j: pl.BlockSpec((tt, D_MODEL), lambda i, j=j: (i, j))
    return pl.pallas_call(
        functools.partial(_conv_long_kernel, tiles_per_seq=tps),
        grid=(batch * tps,),
        in_specs=[col(0), col(1), col(2), _resident((CONV_WIDTH, D_MODEL)),
                  pl.BlockSpec((1, 2, D_MODEL), lambda i: (i // tps, 0, 0))],
        out_specs=[pl.BlockSpec((tt, D_MODEL), lambda i: (i, 0)),
                   pl.BlockSpec((1, 2, D_MODEL), lambda i: (i // tps, 0, 0))],
        out_shape=[jax.ShapeDtypeStruct((batch * seq, D_MODEL), BF16),
                   jax.ShapeDtypeStruct((batch, 2, D_MODEL), F32)],
        scratch_shapes=[pltpu.VMEM((8, D_MODEL), F32)],
        compiler_params=_cparams(("arbitrary",)),
        name="conv_long",
    )(z, z, z, conv_w, state0)


def _conv_short_kernel(b_ref, c_ref, v_ref, w_ref, p1_ref, p2_ref, mix_ref, u_ref, *, seq):
    u = c_ref[...].astype(F32) * v_ref[...].astype(F32)
    t = lax.broadcasted_iota(I32, u.shape, 0) % seq
    prev1 = jnp.where(t == 0, p1_ref[...], pltpu.roll(u, 1, 0))
    prev2 = jnp.where(t <= 1, p2_ref[...], pltpu.roll(u, 2, 0))
    mix_ref[...] = (b_ref[...].astype(F32) * _conv_taps(u, prev1, prev2, w_ref)).astype(BF16)
    u_ref[...] = u


def _conv_short(z, conv_w, state0, batch, seq):
    m = batch * seq
    zeros = jnp.zeros((batch, seq - 1, D_MODEL), F32)
    p1 = jnp.concatenate([state0[:, 1:2], zeros], axis=1).reshape(m, D_MODEL)
    p2 = jnp.concatenate([state0, zeros[:, 1:]], axis=1).reshape(m, D_MODEL)
    col = lambda j: pl.BlockSpec((m, D_MODEL), lambda i, j=j: (0, j))
    full = pl.BlockSpec((m, D_MODEL), lambda i: (0, 0))
    mix, u = pl.pallas_call(
        functools.partial(_conv_short_kernel, seq=seq),
        grid=(1,),
        in_specs=[col(0), col(1), col(2), _resident((CONV_WIDTH, D_MODEL)), full, full],
        out_specs=[full, full],
        out_shape=[jax.ShapeDtypeStruct((m, D_MODEL), BF16), jax.ShapeDtypeStruct((m, D_MODEL), F32)],
        compiler_params=_cparams(("arbitrary",)),
        name="conv_short",
    )(z, z, z, conv_w, p1, p2)
    return mix, u.reshape(batch, seq, D_MODEL)[:, seq - 2:]


def _xattn_kernel(q_ref, k_hbm, v_hbm, o_ref, kbuf, vbuf, sem, *, nb, layer, n_bt):
    i = pl.program_id(0)
    t = pl.program_id(1)
    slot = i % 2

    def copies(bt, sl):
        out = []
        for b in range(nb):
            for h in range(XA_HEADS):
                out.append(pltpu.make_async_copy(k_hbm.at[layer, bt * nb + b, :, h, :], kbuf.at[sl, b, h], sem.at[0, sl]))
                out.append(pltpu.make_async_copy(v_hbm.at[layer, bt * nb + b, :, h, :], vbuf.at[sl, b, h], sem.at[1, sl]))
        return out

    @pl.when((i == 0) & (t == 0))
    def _():
        for c in copies(0, 0):
            c.start()

    @pl.when(t == 0)
    def _():
        @pl.when(i + 1 < n_bt)
        def _():
            for c in copies(i + 1, 1 - slot):
                c.start()

        for c in copies(i, slot):
            c.wait()

    scale = XA_DIM ** -0.5
    tq = q_ref.shape[1]
    rows = -(-tq // SHORT_ROWS) * SHORT_ROWS
    pairs = [(b, h) for b in range(nb) for h in range(XA_HEADS)]
    scores = []
    for b, h in pairs:
        q = q_ref[b, :, h * XA_DIM:(h + 1) * XA_DIM]
        if rows != tq:
            q = jnp.concatenate([q.astype(F32), jnp.zeros((rows - tq, XA_DIM), F32)], axis=0).astype(BF16)
        k = kbuf[slot, b, h].astype(BF16)
        scores.append(lax.dot_general(q, k, NT_DIMS, preferred_element_type=F32) * scale)
    s = jnp.concatenate(scores, axis=0)
    p = jnp.exp(s - jnp.max(s, axis=-1, keepdims=True))
    p = (p / jnp.sum(p, axis=-1, keepdims=True)).astype(BF16)
    for n, (b, h) in enumerate(pairs):
        v = vbuf[slot, b, h].astype(BF16)
        o = jnp.dot(p[n * rows:(n + 1) * rows], v, preferred_element_type=F32)
        o_ref[b, :, h * XA_DIM:(h + 1) * XA_DIM] = o[:tq].astype(BF16)


def _xattn(z, batch, seq, qcol, mem_k, mem_v, layer, nb, tq):
    n_bt = batch // nb
    plane = pltpu.VMEM((2, nb, XA_HEADS, N_MEM, XA_DIM), F32)
    if nb == 1:
        q_arr = z.reshape(1, z.shape[0], z.shape[1])
        nt = seq // tq
        q_spec = pl.BlockSpec((1, tq, D_MODEL), lambda i, t: (0, i * nt + t, qcol))
        o_spec = pl.BlockSpec((1, tq, D_MODEL), lambda i, t: (0, i * nt + t, 0))
        o_shape = (1, batch * seq, D_MODEL)
    else:
        q_arr = z.reshape(batch, seq, z.shape[1])
        q_spec = pl.BlockSpec((nb, tq, D_MODEL), lambda i, t: (i, t, qcol))
        o_spec = pl.BlockSpec((nb, tq, D_MODEL), lambda i, t: (i, t, 0))
        o_shape = (batch, seq, D_MODEL)
    return pl.pallas_call(
        functools.partial(_xattn_kernel, nb=nb, layer=layer, n_bt=n_bt),
        grid=(n_bt, seq // tq),
        in_specs=[q_spec, pl.BlockSpec(memory_space=pl.ANY), pl.BlockSpec(memory_space=pl.ANY)],
        out_specs=o_spec,
        out_shape=jax.ShapeDtypeStruct(o_shape, BF16),
        scratch_shapes=[plane, plane, pltpu.SemaphoreType.DMA((2, 2))],
        compiler_params=_cparams(("arbitrary", "arbitrary")),
        name="xattn",
    )(q_arr, mem_k, mem_v).reshape(batch * seq, D_MODEL)


def _rotary(x, cos, sin):
    half = x.shape[-1] // 2
    x1, x2 = x[:, :half], x[:, half:]
    return jnp.concatenate([x1 * cos - x2 * sin, x1 * sin + x2 * cos], axis=-1)


def _group_norm_gate(o, gate, gn):
    mu = jnp.mean(o, axis=-1, keepdims=True)
    var = jnp.mean(jnp.square(o - mu), axis=-1, keepdims=True)
    return _silu(gate) * ((o - mu) * lax.rsqrt(var + EPS) * gn)


def _ret_long_kernel(full_ref, q_ref, k_ref, v_ref, g_ref, cos_ref, sin_ref, dec_ref, inn_ref, tail_ref,
                     gn_ref, mix_ref, sfin_ref, s_acc, *, chunk):
    t = pl.program_id(1)
    heads = range(RET_HEADS)

    @pl.when(t == 0)
    def _():
        s_acc[...] = jnp.zeros_like(s_acc)

    cos, sin = cos_ref[...], sin_ref[...]
    qs = [_rotary(q_ref[0, :, h * RET_DK:(h + 1) * RET_DK].astype(F32), cos, sin) for h in heads]
    ks = [_rotary(k_ref[0, :, h * RET_DK:(h + 1) * RET_DK].astype(F32), cos, sin) * (RET_DK ** -0.5) for h in heads]
    for c in range(q_ref.shape[1] // chunk):
        sl = slice(c * chunk, (c + 1) * chunk)
        qc = [qs[h][sl].astype(BF16) for h in heads]
        kc = [ks[h][sl] for h in heads]
        vc = [v_ref[0, sl, h * RET_DV:(h + 1) * RET_DV] for h in heads]
        s_old = [s_acc[h] for h in heads]
        scores = [lax.dot_general(qc[h], kc[h].astype(BF16), NT_DIMS, preferred_element_type=F32) * dec_ref[h]
                  for h in heads]
        cross = [jnp.dot(qc[h], s_old[h].astype(BF16), preferred_element_type=F32) * inn_ref[h][:, 0:1]
                 for h in heads]
        o = [jnp.dot(scores[h].astype(BF16), vc[h], preferred_element_type=F32) + cross[h] for h in heads]
        for h in heads:
            kt = (kc[h] * tail_ref[h][:, 0:1]).astype(BF16)
            s_acc[h] = full_ref[h] * s_old[h] + lax.dot_general(kt, vc[h], TN_DIMS, preferred_element_type=F32)
        og = jnp.concatenate(o, axis=0)
        mu = jnp.mean(og, axis=-1, keepdims=True)
        var = jnp.mean(jnp.square(og - mu), axis=-1, keepdims=True)
        on = (og - mu) * lax.rsqrt(var + EPS)
        for h in heads:
            vs = slice(h * RET_DV, (h + 1) * RET_DV)
            gate = g_ref[0, sl, vs]
            mix_ref[0, sl, vs] = _silu(gate) * (on[h * chunk:(h + 1) * chunk] * gn_ref[:, vs]).astype(BF16)

    @pl.when(t == pl.num_programs(1) - 1)
    def _():
        sfin_ref[0] = s_acc[...]


def _ret_tables(pos, chunk):
    half = RET_DK // 2
    inv = ROPE_BASE ** (-jnp.arange(half, dtype=F32) / half)
    ang = pos.astype(F32)[:, None] * inv[None, :]
    log_g = jnp.log1p(-jnp.exp2(-5.0 - jnp.arange(RET_HEADS, dtype=F32)))
    idx = jnp.arange(chunk)
    diff = idx[:, None] - idx[None, :]
    decay = jnp.where(diff[None] >= 0,
                      jnp.exp(jnp.maximum(diff, 0)[None].astype(F32) * log_g[:, None, None]), 0.0)
    inner = jnp.exp((idx + 1).astype(F32)[None, :] * log_g[:, None])
    tail = jnp.exp((chunk - 1 - idx).astype(F32)[None, :] * log_g[:, None])
    full = jnp.exp(chunk * log_g)
    return jnp.cos(ang), jnp.sin(ang), decay, inner, tail, full


def _ret_long(z, batch, seq, gn, pos):
    z3 = z.reshape(1, z.shape[0], z.shape[1])
    tc = min(512, seq)
    nt = seq // tc
    chunk = 2 * RET_CHUNK if tc % (2 * RET_CHUNK) == 0 else RET_CHUNK
    cos, sin, decay, inner, tail, full = _ret_tables(pos, chunk)
    lane_b = lambda a: jnp.broadcast_to(a[:, :, None], (RET_HEADS, chunk, LANES))
    qw = RET_HEADS * RET_DK
    vw = RET_HEADS * RET_DV
    state_spec = pl.BlockSpec((1, RET_HEADS, RET_DK, RET_DV), lambda b, t: (b, 0, 0, 0))
    mix, state = pl.pallas_call(
        functools.partial(_ret_long_kernel, chunk=chunk),
        grid=(batch, seq // tc),
        in_specs=[
            pl.BlockSpec(memory_space=pltpu.SMEM),
            pl.BlockSpec((1, tc, qw), lambda b, t: (0, b * nt + t, 0)),
            pl.BlockSpec((1, tc, qw), lambda b, t: (0, b * nt + t, 1)),
            pl.BlockSpec((1, tc, vw), lambda b, t: (0, b * nt + t, 1)),
            pl.BlockSpec((1, tc, vw), lambda b, t: (0, b * nt + t, 2)),
            pl.BlockSpec((tc, RET_DK // 2), lambda b, t: (t, 0)),
            pl.BlockSpec((tc, RET_DK // 2), lambda b, t: (t, 0)),
            _resident((RET_HEADS, chunk, chunk)), _resident((RET_HEADS, chunk, LANES)),
            _resident((RET_HEADS, chunk, LANES)), _resident((1, vw)),
        ],
        out_specs=[pl.BlockSpec((1, tc, vw), lambda b, t: (0, b * nt + t, 0)), state_spec],
        out_shape=[jax.ShapeDtypeStruct((1, batch * seq, vw), BF16),
                   jax.ShapeDtypeStruct((batch, RET_HEADS, RET_DK, RET_DV), F32)],
        scratch_shapes=[pltpu.VMEM((RET_HEADS, RET_DK, RET_DV), F32)],
        compiler_params=_cparams(("arbitrary", "arbitrary")),
        name="ret_long",
    )(full, z3, z3, z3, z3, cos, sin, decay, lane_b(inner), lane_b(tail), gn.reshape(1, -1))
    return mix.reshape(batch * seq, vw), state


def _ret_short_kernel(full_ref, q_ref, k_ref, v_ref, g_ref, cos_ref, sin_ref, dec_ref, inn_ref, tail_ref,
                      gn_ref, s0_ref, mix_ref, snew_ref, *, nb, seq):
    def pad_rows(a):
        return jnp.concatenate([a, jnp.zeros((SHORT_ROWS - seq, a.shape[1]), F32)], axis=0)

    cos, sin = cos_ref[...], sin_ref[...]
    for b in range(nb):
        for h in range(RET_HEADS):
            qs = slice(h * RET_DK, (h + 1) * RET_DK)
            vs = slice(h * RET_DV, (h + 1) * RET_DV)
            q = pad_rows(_rotary(q_ref[b, :, qs].astype(F32), cos, sin)).astype(BF16)
            k = pad_rows(_rotary(k_ref[b, :, qs].astype(F32), cos, sin) * (RET_DK ** -0.5))
            v = pad_rows(v_ref[b, :, vs].astype(F32)).astype(BF16)
            s_old = s0_ref[b, h]
            scores = lax.dot_general(q, k.astype(BF16), NT_DIMS, preferred_element_type=F32) * dec_ref[h]
            o = jnp.dot(scores.astype(BF16), v, preferred_element_type=F32)
            o = o + jnp.dot(q, s_old.astype(BF16), preferred_element_type=F32) * inn_ref[h]
            snew_ref[b, h] = full_ref[h] * s_old + lax.dot_general((k * tail_ref[h]).astype(BF16), v, TN_DIMS,
                                                                   preferred_element_type=F32)
            mix_ref[b, :, vs] = _group_norm_gate(o[:seq], g_ref[b, :, vs].astype(F32), gn_ref[:, vs]).astype(BF16)


def _ret_short(z, batch, seq, state0, gn, pos, nb=4):
    z3 = z.reshape(batch, seq, z.shape[1])
    cos, sin, decay, inner, tail, full = _ret_tables(pos, seq)
    qw = RET_HEADS * RET_DK
    vw = RET_HEADS * RET_DV
    small = lambda a: _resident(a.shape)
    rp = SHORT_ROWS - seq
    decay = jnp.pad(decay, ((0, 0), (0, rp), (0, rp)))
    inner3 = jnp.pad(inner, ((0, 0), (0, rp)))[:, :, None]
    tail3 = jnp.pad(tail, ((0, 0), (0, rp)))[:, :, None]
    mix, state = pl.pallas_call(
        functools.partial(_ret_short_kernel, nb=nb, seq=seq),
        grid=(batch // nb,),
        in_specs=[
            pl.BlockSpec(memory_space=pltpu.SMEM),
            pl.BlockSpec((nb, seq, qw), lambda i: (i, 0, 0)),
            pl.BlockSpec((nb, seq, qw), lambda i: (i, 0, 1)),
            pl.BlockSpec((nb, seq, vw), lambda i: (i, 0, 1)),
            pl.BlockSpec((nb, seq, vw), lambda i: (i, 0, 2)),
            small(cos), small(sin), small(decay), small(inner3), small(tail3),
            _resident((1, vw)),
            pl.BlockSpec((nb, RET_HEADS, RET_DK, RET_DV), lambda i: (i, 0, 0, 0)),
        ],
        out_specs=[
            pl.BlockSpec((nb, seq, vw), lambda i: (i, 0, 0)),
            pl.BlockSpec((nb, RET_HEADS, RET_DK, RET_DV), lambda i: (i, 0, 0, 0)),
        ],
        out_shape=[jax.ShapeDtypeStruct((batch, seq, vw), BF16),
                   jax.ShapeDtypeStruct((batch, RET_HEADS, RET_DK, RET_DV), F32)],
        compiler_params=_cparams(("arbitrary",)),
        name="ret_short",
    )(full, z3, z3, z3, z3, cos, sin, decay, inner3, tail3, gn.reshape(1, -1), state0)
    return mix.reshape(batch * seq, vw), state


def _route(logits_t):
    n = logits_t.shape[1]
    epg = EXPERTS_PER_GROUP
    row = lax.broadcasted_iota(I32, (epg, n), 0).astype(F32)
    neg = jnp.float32(-1e30)
    big = jnp.float32(epg)
    is_g = row < N_GROUPS
    lg = jnp.where(is_g, logits_t[0:epg], neg)
    gmax = jnp.max(lg, axis=0, keepdims=True)
    gidx = jnp.min(jnp.where(lg == gmax, row, big), axis=0, keepdims=True)
    pg_sel = 1.0 / jnp.sum(jnp.where(is_g, jnp.exp(lg - gmax), 0.0), axis=0, keepdims=True)
    sel = logits_t[epg:2 * epg]
    for g in range(1, N_GROUPS):
        sel = jnp.where(gidx == g, logits_t[epg * (g + 1):epg * (g + 2)], sel)
    e = jnp.exp(sel - jnp.max(sel, axis=0, keepdims=True))
    pe = e / jnp.sum(e, axis=0, keepdims=True)
    t1 = jnp.max(pe, axis=0, keepdims=True)
    i1 = jnp.min(jnp.where(pe == t1, row, big), axis=0, keepdims=True)
    pe2 = jnp.where(row == i1, -1.0, pe)
    t2 = jnp.max(pe2, axis=0, keepdims=True)
    i2 = jnp.min(jnp.where(pe2 == t2, row, big), axis=0, keepdims=True)
    den = t1 + t2
    gates = jnp.where(row == i1, t1 / den * pg_sel, 0.0) + jnp.where(row == i2, t2 / den * pg_sel, 0.0)
    ids = jnp.where(row == 0, gidx, 0.0)
    block = jnp.concatenate([gates, ids, jnp.zeros((LANES - 2 * epg, n), F32)], axis=0)
    return block.T


def _outproj_pair_kernel(xl_ref, mixl_ref, xal_ref, xs_ref, mixs_ref, xas_ref, wa_ref, wb_ref, g_ref, wr_ref,
                         o_ref, r_ref, *, sub, n_long):
    i = pl.program_id(0)

    @pl.when(i < n_long)
    def _():
        _outproj_kernel(xl_ref, mixl_ref, xal_ref, wa_ref, wb_ref, g_ref, wr_ref, o_ref, r_ref, sub=sub)

    @pl.when(i >= n_long)
    def _():
        _outproj_kernel(xs_ref, mixs_ref, xas_ref, wa_ref, wb_ref, g_ref, wr_ref, o_ref, r_ref, sub=sub)


def _outproj_kernel(x_ref, mix_ref, xa_ref, wa_ref, wb_ref, g_ref, wr_ref, o_ref, r_ref, *, sub):
    acc = jnp.dot(mix_ref[...], wa_ref[...], preferred_element_type=F32)
    acc = acc + jnp.dot(xa_ref[...], wb_ref[...], preferred_element_type=F32)
    o_ref[:, :D_MODEL] = x_ref[...] + acc
    logits = []
    for s in range(x_ref.shape[0] // sub):
        rows = slice(s * sub, (s + 1) * sub)
        h2 = _rms(o_ref[rows, :D_MODEL], g_ref[...])
        hi = h2.astype(BF16)
        lo = (h2 - hi.astype(F32)).astype(BF16)
        p_hi = lax.dot_general(wr_ref[...], hi, NT_DIMS, preferred_element_type=F32)
        p_lo = lax.dot_general(wr_ref[0:ROUTER_ROWS, :], lo, NT_DIMS, preferred_element_type=F32)
        logits.append((p_hi[:ROUTER_ROWS] + p_lo) + p_hi[ROUTER_ROWS:])
    route = _route(jnp.concatenate(logits, axis=1))
    o_ref[:, D_MODEL:] = route
    r_ref[...] = route


def _router_weights(w_group, w_router):
    epg = EXPERTS_PER_GROUP
    zeros = lambda r: jnp.zeros((DEPTH, r, D_MODEL), F32)
    wt = jnp.concatenate([jnp.swapaxes(w_group, 1, 2), zeros(epg - N_GROUPS), jnp.swapaxes(w_router, 1, 2),
                          zeros(ROUTER_ROWS - epg - N_EXPERTS)], axis=1)
    hi = wt.astype(BF16)
    return jnp.concatenate([hi, (wt - hi.astype(F32)).astype(BF16)], axis=1)


def _outproj_router(x, x_off, mix, xa, w_out, g_ffn, w_router2):
    m, cm = mix.shape
    tm = min(512, m)
    xb = x_off // tm
    row = lambda w, off=0: pl.BlockSpec((tm, w), lambda i: (i + off, 0))
    return pl.pallas_call(
        functools.partial(_outproj_kernel, sub=min(128, tm)),
        grid=(m // tm,),
        in_specs=[row(D_MODEL, xb), row(cm), row(D_MODEL), _resident((cm, D_MODEL)), _resident((D_MODEL, D_MODEL)),
                  _resident((1, D_MODEL)), _resident((2 * ROUTER_ROWS, D_MODEL))],
        out_specs=[row(AUG_W), row(LANES)],
        out_shape=[jax.ShapeDtypeStruct((m, AUG_W), F32), jax.ShapeDtypeStruct((m, LANES), F32)],
        compiler_params=_cparams(("arbitrary",)),
        name="outproj_router",
    )(x, mix, xa, w_out[:cm], w_out[cm:], g_ffn.reshape(1, D_MODEL), w_router2)


def _outproj_router_pair(long, short, w_out, g_ffn, w_router2):
    tm = 512
    cm = long[1].shape[1]
    nl, ns = long[0].shape[0] // tm, short[0].shape[0] // tm
    first = lambda w: pl.BlockSpec((tm, w), lambda i: (jnp.minimum(i, nl - 1), 0))
    second = lambda w: pl.BlockSpec((tm, w), lambda i: (jnp.maximum(i - nl, 0), 0))
    row = lambda w: pl.BlockSpec((tm, w), lambda i: (i, 0))
    m = (nl + ns) * tm
    return pl.pallas_call(
        functools.partial(_outproj_pair_kernel, sub=128, n_long=nl),
        grid=(nl + ns,),
        in_specs=[first(D_MODEL), first(cm), first(D_MODEL), second(D_MODEL), second(cm), second(D_MODEL),
                  _resident((cm, D_MODEL)), _resident((D_MODEL, D_MODEL)),
                  _resident((1, D_MODEL)), _resident((2 * ROUTER_ROWS, D_MODEL))],
        out_specs=[row(AUG_W), row(LANES)],
        out_shape=[jax.ShapeDtypeStruct((m, AUG_W), F32), jax.ShapeDtypeStruct((m, LANES), F32)],
        compiler_params=_cparams(("arbitrary",)),
        name="outproj_router_pair",
    )(*long, *short, w_out[:cm], w_out[cm:], g_ffn.reshape(1, D_MODEL), w_router2)


STEP_VALID, STEP_FIRST, STEP_LAST, STEP_NEWGROUP = 1, 2, 4, 8


def _moe_plan(gid, tm):
    m = gid.shape[0]
    ntile = m // tm
    nstep = ntile + N_GROUPS - 1
    counts = jnp.sum((gid[:, None] == jnp.arange(N_GROUPS, dtype=I32)[None, :]).astype(I32), axis=0)
    gend = jnp.cumsum(counts)
    gstart = gend - counts
    row_tok = jnp.argsort(gid, stable=True).astype(I32)
    inner_ok = (counts[1:] > 0) & (gstart[1:] % tm != 0)
    cuts = jnp.concatenate([jnp.arange(ntile, dtype=I32) * tm, jnp.where(inner_ok, gstart[1:], m)])
    cut_ok = jnp.concatenate([jnp.ones((ntile,), bool), inner_ok])
    rank = jnp.sum(((cuts[None, :] < cuts[:, None]) & cut_ok[None, :]).astype(I32), axis=1)
    sel = jnp.where(cut_ok, rank, nstep)[None, :] == jnp.arange(nstep, dtype=I32)[:, None]
    valid = jnp.any(sel, axis=1)
    start = jnp.sum(jnp.where(sel, cuts[None, :], 0), axis=1)
    start = jnp.where(valid, start, jnp.max(start))
    st_tile = start // tm
    st_gid = jnp.sum((gend[None, :] <= start[:, None]).astype(I32), axis=1)
    first = valid & (start % tm == 0)
    next_first = jnp.concatenate([first[1:], jnp.ones((1,), bool)])
    next_valid = jnp.concatenate([valid[1:], jnp.zeros((1,), bool)])
    last = valid & (next_first | ~next_valid)
    prev_gid = jnp.concatenate([jnp.full((1,), -1, I32), st_gid[:-1]])
    newgroup = valid & (st_gid != prev_gid)
    flags = valid * STEP_VALID + first * STEP_FIRST + last * STEP_LAST + newgroup * STEP_NEWGROUP
    return st_tile, st_gid, flags.astype(I32), row_tok


def _moe_kernel(tile_ref, gid_ref, flag_ref, tok_ref, xaug_hbm, g_ref, fin_ref, wgu_ref, wd_ref, out_hbm,
                xbuf, ybuf, wgu_bf, wd_bf, gsem, ssem, *, tm, ntile, final_norm):
    s = pl.program_id(0)
    tile = tile_ref[s]
    grp = gid_ref[s]
    flags = flag_ref[s]
    valid = (flags & STEP_VALID) > 0
    first = (flags & STEP_FIRST) > 0
    last = (flags & STEP_LAST) > 0
    slot = tile % 2

    @pl.when((flags & STEP_NEWGROUP) > 0)
    def _():
        for j in range(EXPERTS_PER_GROUP):
            wgu_bf[j] = wgu_ref[j].astype(BF16)
            wd_bf[j] = wd_ref[j].astype(BF16)

    def gather_start(t, sl):
        for r in range(tm):
            tok = tok_ref[t * tm + r]
            pltpu.make_async_copy(xaug_hbm.at[pl.ds(tok, 1)], xbuf.at[sl, pl.ds(r, 1)],
                                  gsem.at[sl]).start(priority=r % 2)

    def scatter_start(t, sl):
        for r in range(tm):
            tok = tok_ref[t * tm + r]
            pltpu.make_async_copy(ybuf.at[sl, pl.ds(r, 1)], out_hbm.at[pl.ds(tok, 1)],
                                  ssem.at[sl]).start(priority=r % 2)

    def gather_wait(sl):
        pltpu.make_async_copy(xaug_hbm.at[pl.ds(0, tm)], xbuf.at[sl], gsem.at[sl]).wait()

    def scatter_wait(sl):
        pltpu.make_async_copy(ybuf.at[sl], out_hbm.at[pl.ds(0, tm)], ssem.at[sl]).wait()

    @pl.when(s == 0)
    def _():
        gather_start(0, 0)

    @pl.when(first & (tile + 1 < ntile))
    def _():
        gather_start(tile + 1, 1 - slot)

    @pl.when(first)
    def _():
        gather_wait(slot)

        @pl.when(tile >= 2)
        def _():
            scatter_wait(slot)

        ybuf[slot] = xbuf[slot, :, 0:D_MODEL]

    @pl.when(valid)
    def _():
        xn = xbuf[slot, :, 0:D_MODEL]
        gl = xbuf[slot, :, GATE_LANE0:AUG_W]
        mine = gl[:, GID_LANE:GID_LANE + 1] == grp.astype(F32)
        h2 = _rms(xn, g_ref[...]).astype(BF16)
        acc = jnp.zeros((tm, D_MODEL), F32)
        for j in range(EXPERTS_PER_GROUP):
            gu = jnp.dot(h2, wgu_bf[j], preferred_element_type=F32)
            gate = jnp.where(mine, gl[:, j:j + 1], 0.0)
            act = _silu(gu[:, :EXPERT_FF]) * gu[:, EXPERT_FF:] * gate
            acc = acc + jnp.dot(act.astype(BF16), wd_bf[j], preferred_element_type=F32)
        ybuf[slot] = ybuf[slot] + acc

    @pl.when(last)
    def _():
        if final_norm:
            ybuf[slot] = _rms(ybuf[slot], fin_ref[...])
        scatter_start(tile, slot)

        @pl.when(tile == ntile - 1)
        def _():
            if ntile >= 2:
                scatter_wait(1 - slot)
            scatter_wait(slot)


def _moe(xaug, route, g_ffn, final_g, w_gu, w_down, layer, final_norm):
    m = xaug.shape[0]
    tm = min(MOE_TM, m)
    gid = route[:, GID_LANE].astype(I32)
    st_tile, st_gid, flags, row_tok = _moe_plan(gid, tm)
    epg = EXPERTS_PER_GROUP
    grid_spec = pltpu.PrefetchScalarGridSpec(
        num_scalar_prefetch=4,
        grid=(st_tile.shape[0],),
        in_specs=[
            pl.BlockSpec(memory_space=pl.ANY),
            pl.BlockSpec((1, D_MODEL), lambda s, *_: (0, 0)),
            pl.BlockSpec((1, D_MODEL), lambda s, *_: (0, 0)),
            pl.BlockSpec((None, epg, D_MODEL, 2 * EXPERT_FF), lambda s, tl, gd, fl, tok: (layer, gd[s], 0, 0),
                         pipeline_mode=pl.Buffered(1)),
            pl.BlockSpec((None, epg, EXPERT_FF, D_MODEL), lambda s, tl, gd, fl, tok: (layer, gd[s], 0, 0),
                         pipeline_mode=pl.Buffered(1)),
        ],
        out_specs=pl.BlockSpec(memory_space=pl.ANY),
        scratch_shapes=[
            pltpu.VMEM((2, tm, AUG_W), F32),
            pltpu.VMEM((2, tm, D_MODEL), F32),
            pltpu.VMEM((epg, D_MODEL, 2 * EXPERT_FF), BF16),
            pltpu.VMEM((epg, EXPERT_FF, D_MODEL), BF16),
            pltpu.SemaphoreType.DMA((2,)),
            pltpu.SemaphoreType.DMA((2,)),
        ],
    )
    return pl.pallas_call(
        functools.partial(_moe_kernel, tm=tm, ntile=m // tm, final_norm=final_norm),
        grid_spec=grid_spec,
        out_shape=jax.ShapeDtypeStruct((m, D_MODEL), F32),
        compiler_params=_cparams(("arbitrary",)),
        name="moe",
    )(st_tile, st_gid, flags, row_tok, xaug, g_ffn.reshape(1, D_MODEL), final_g.reshape(1, D_MODEL), w_gu, w_down)


SHORT_NB = 4


def _forward(x_long, x_short, pos_long, pos_short, conv_state, ret_state, mem_long, mem_short, wts):
    assert DEPTH == 2
    (bl, tl), (bs, ts) = x_long.shape[:2], x_short.shape[:2]
    ml, ms = bl * tl, bs * ts
    xl, xs = x_long.reshape(ml, D_MODEL), x_short.reshape(ms, D_MODEL)
    tq = min(512, tl)
    moe = lambda rows, route, i: _moe(rows, route, wts["norm_ffn"][i], wts["final_norm"], wts["moe_w_gate_up"],
                                      wts["moe_w_down"], i, final_norm=(i == DEPTH - 1))

    zl = _inproj(xl, wts["norm_mix"][0], wts["conv_w_in"])
    zs = _inproj(xs, wts["norm_mix"][0], wts["conv_w_in"])
    mix_l, conv_l = _conv_long(zl, wts["conv_w"], jnp.zeros((bl, CONV_WIDTH - 1, D_MODEL), F32), bl, tl)
    mix_s, conv_s = _conv_short(zs, wts["conv_w"], conv_state, bs, ts)
    xa_l = _xattn(zl, bl, tl, 3, *mem_long, 0, 1, tq)
    xa_s = _xattn(zs, bs, ts, 3, *mem_short, 0, SHORT_NB, ts)
    routed = _outproj_router_pair((xl, mix_l, xa_l), (xs, mix_s, xa_s), wts["conv_w_out"], wts["norm_ffn"][0],
                                  wts["w_router"][0])
    x1 = moe(*routed, 0)

    z = _inproj(x1, wts["norm_mix"][1], wts["ret_w_in"])
    zs = z[ml:]
    mix_l, ret_l = _ret_long(z, bl, tl, wts["ret_gn"], pos_long)
    mix_s, ret_s = _ret_short(zs, bs, ts, ret_state, wts["ret_gn"], pos_short)
    xa_l = _xattn(z, bl, tl, 6, *mem_long, 1, 1, tq)
    xa_s = _xattn(zs, bs, ts, 6, *mem_short, 1, SHORT_NB, ts)
    y_l = moe(*_outproj_router(x1, 0, mix_l, xa_l, wts["ret_w_out"], wts["norm_ffn"][1], wts["w_router"][1]), 1)
    y_s = moe(*_outproj_router(x1, ml, mix_s, xa_s, wts["ret_w_out"], wts["norm_ffn"][1], wts["w_router"][1]), 1)
    return (y_l.reshape(bl, tl, D_MODEL), y_s.reshape(bs, ts, D_MODEL), conv_l, conv_s, ret_l, ret_s)


def kernel(x_prompt, x_sample, state_conv, state_ret, cache_mem_k, cache_mem_v, mem_prompt, norm_mix, norm_ffn,
           mem_norm, w_mem_kv, conv_w_in, conv_w, conv_w_out, ret_w_in, ret_gn, ret_w_out, moe_w_group,
           moe_w_router, moe_w_gate_up, moe_w_down, final_norm):
    batch, seq = x_prompt.shape[0], x_prompt.shape[1]
    dec_batch, dec_seq = x_sample.shape[0], x_sample.shape[1]
    w_router = _router_weights(moe_w_group, moe_w_router)
    wts = dict(
        norm_mix=norm_mix, norm_ffn=norm_ffn, final_norm=final_norm, w_router=w_router,
        conv_w_in=conv_w_in[0].astype(BF16), conv_w=conv_w[0], conv_w_out=conv_w_out[0].astype(BF16),
        ret_w_in=ret_w_in[0].astype(BF16), ret_gn=ret_gn[0], ret_w_out=ret_w_out[0].astype(BF16),
        moe_w_gate_up=moe_w_gate_up, moe_w_down=moe_w_down,
    )
    mem_k_prompt, mem_v_prompt = _mem_kv(mem_prompt.reshape(batch * N_MEM, D_MODEL), mem_norm, w_mem_kv.astype(BF16))

    y_prompt, y_sample, conv_p, conv_s, ret_p, ret_s = _forward(
        x_prompt, x_sample, jnp.arange(seq, dtype=I32), PAST_LEN + jnp.arange(dec_seq, dtype=I32),
        state_conv[0], state_ret[0], (mem_k_prompt, mem_v_prompt), (cache_mem_k, cache_mem_v), wts)
    return (y_prompt, y_sample, conv_p[None], conv_s[None], ret_p[None], ret_s[None], mem_k_prompt, mem_v_prompt)
```

```python
import functools

import jax
import jax.numpy as jnp
from jax import lax
from jax.experimental import pallas as pl
from jax.experimental.pallas import tpu as pltpu

F32 = jnp.float32
BF16 = jnp.bfloat16
I32 = jnp.int32

D_MODEL = 1024
DEPTH = 2
CONV_WIDTH = 3
RET_HEADS = 4
RET_DK = 256
RET_DV = 512
RET_CHUNK = 128
ROPE_BASE = 10000.0
N_MEM = 256
XA_HEADS = 4
XA_DIM = 256
N_GROUPS = 4
EXPERTS_PER_GROUP = 8
N_EXPERTS = N_GROUPS * EXPERTS_PER_GROUP
EXPERT_FF = 256
EPS = 1e-6

LANES = 128
AUG_W = D_MODEL + LANES
GATE_LANE0 = D_MODEL
GID_LANE = EXPERTS_PER_GROUP
ROUTER_ROWS = 64
VMEM_LIMIT = 56 * 1024 * 1024
MOE_TM = 512
SHORT_ROWS = 8
PAST_LEN = 16384
NT_DIMS = (((1,), (1,)), ((), ()))
TN_DIMS = (((0,), (0,)), ((), ()))


def _cparams(sem):
    return pltpu.CompilerParams(dimension_semantics=sem, vmem_limit_bytes=VMEM_LIMIT)


def _rms(x, g):
    return x * lax.rsqrt(jnp.mean(x * x, axis=-1, keepdims=True) + EPS) * g


def _silu(x):
    return x * jax.nn.sigmoid(x)


def _resident(shape):
    nd = len(shape)
    return pl.BlockSpec(shape, lambda *_: (0,) * nd, pipeline_mode=pl.Buffered(1))


def _memkv_kernel(x_ref, g_ref, w_ref, k_ref, v_ref, *, nb):
    h = _rms(x_ref[...], g_ref[...]).astype(BF16)
    kv = jnp.dot(h, w_ref[...], preferred_element_type=F32)
    for b in range(nb):
        rows = slice(b * N_MEM, (b + 1) * N_MEM)
        for hd in range(XA_HEADS):
            k_ref[b, :, hd, :] = kv[rows, hd * XA_DIM:(hd + 1) * XA_DIM]
            v_ref[b, :, hd, :] = kv[rows, D_MODEL + hd * XA_DIM:D_MODEL + (hd + 1) * XA_DIM]


def _mem_kv(mem2d, mem_norm, w_kv):
    batch = mem2d.shape[0] // N_MEM
    nb = min(2, batch)
    tm = nb * N_MEM
    out = jax.ShapeDtypeStruct((DEPTH, batch, N_MEM, XA_HEADS, XA_DIM), F32)
    out_spec = pl.BlockSpec((None, nb, N_MEM, XA_HEADS, XA_DIM), lambda l, j: (l, j, 0, 0, 0))
    return pl.pallas_call(
        functools.partial(_memkv_kernel, nb=nb),
        grid=(DEPTH, batch // nb),
        in_specs=[
            pl.BlockSpec((tm, D_MODEL), lambda l, j: (j, 0)),
            pl.BlockSpec((None, 1, D_MODEL), lambda l, j: (l, 0, 0)),
            pl.BlockSpec((None, D_MODEL, 2 * D_MODEL), lambda l, j: (l, 0, 0)),
        ],
        out_specs=[out_spec, out_spec],
        out_shape=[out, out],
        compiler_params=_cparams(("arbitrary", "arbitrary")),
        name="mem_kv",
    )(mem2d, mem_norm.reshape(DEPTH, 1, D_MODEL), w_kv)


def _inproj_kernel(x_ref, g_ref, w_ref, o_ref, *, tn):
    h = _rms(x_ref[...], g_ref[...]).astype(BF16)
    for j in range(o_ref.shape[1] // tn):
        sl = slice(j * tn, (j + 1) * tn)
        o_ref[:, sl] = jnp.dot(h, w_ref[:, sl], preferred_element_type=F32).astype(BF16)


def _inproj(x, g, w):
    m, n = x.shape[0], w.shape[1]
    tm = min(512, m)
    return pl.pallas_call(
        functools.partial(_inproj_kernel, tn=512),
        grid=(m // tm,),
        in_specs=[
            pl.BlockSpec((tm, D_MODEL), lambda i: (i, 0)),
            _resident((1, D_MODEL)),
            _resident((D_MODEL, n)),
        ],
        out_specs=pl.BlockSpec((tm, n), lambda i: (i, 0)),
        out_shape=jax.ShapeDtypeStruct((m, n), BF16),
        compiler_params=_cparams(("arbitrary",)),
        name="inproj",
    )(x, g.reshape(1, D_MODEL), w)


def _conv_taps(u, prev1, prev2, w_ref):
    return (w_ref[0:1, :] * prev2 + w_ref[1:2, :] * prev1) + w_ref[2:3, :] * u


def _conv_long_kernel(b_ref, c_ref, v_ref, w_ref, st0_ref, mix_ref, st_ref, carry, *, tiles_per_seq):
    i = pl.program_id(0)
    tt = c_ref.shape[0]

    @pl.when(i % tiles_per_seq == 0)
    def _():
        carry[6:8, :] = st0_ref[0]

    u = c_ref[...].astype(F32) * v_ref[...].astype(F32)
    pm2 = carry[6:7, :]
    pm1 = carry[7:8, :]
    row = lax.broadcasted_iota(I32, u.shape, 0)
    prev1 = jnp.where(row == 0, pm1, pltpu.roll(u, 1, 0))
    prev2 = jnp.where(row == 0, pm2, jnp.where(row == 1, pm1, pltpu.roll(u, 2, 0)))
    mix_ref[...] = (b_ref[...].astype(F32) * _conv_taps(u, prev1, prev2, w_ref)).astype(BF16)
    carry[...] = u[tt - 8:tt, :]
    st_ref[0] = carry[6:8, :]


def _conv_long(z, conv_w, state0, batch, seq):
    tt = min(1024, seq)
    tps = seq // tt
    col = lambda j: pl.BlockSpec((tt, D_MODEL), lambda i, j=j: (i, j))
    return pl.pallas_call(
        functools.partial(_conv_long_kernel, tiles_per_seq=tps),
        grid=(batch * tps,),
        in_specs=[col(0), col(1), col(2), _resident((CONV_WIDTH, D_MODEL)),
                  pl.BlockSpec((1, 2, D_MODEL), lambda i: (i // tps, 0, 0))],
        out_specs=[pl.BlockSpec((tt, D_MODEL), lambda i: (i, 0)),
                   pl.BlockSpec((1, 2, D_MODEL), lambda i: (i // tps, 0, 0))],
        out_shape=[jax.ShapeDtypeStruct((batch * seq, D_MODEL), BF16),
                   jax.ShapeDtypeStruct((batch, 2, D_MODEL), F32)],
        scratch_shapes=[pltpu.VMEM((8, D_MODEL), F32)],
        compiler_params=_cparams(("arbitrary",)),
        name="conv_long",
    )(z, z, z, conv_w, state0)


def _conv_short_kernel(b_ref, c_ref, v_ref, w_ref, p1_ref, p2_ref, mix_ref, u_ref, *, seq):
    u = c_ref[...].astype(F32) * v_ref[...].astype(F32)
    t = lax.broadcasted_iota(I32, u.shape, 0) % seq
    prev1 = jnp.where(t == 0, p1_ref[...], pltpu.roll(u, 1, 0))
    prev2 = jnp.where(t <= 1, p2_ref[...], pltpu.roll(u, 2, 0))
    mix_ref[...] = (b_ref[...].astype(F32) * _conv_taps(u, prev1, prev2, w_ref)).astype(BF16)
    u_ref[...] = u


def _conv_short(z, conv_w, state0, batch, seq):
    m = batch * seq
    zeros = jnp.zeros((batch, seq - 1, D_MODEL), F32)
    p1 = jnp.concatenate([state0[:, 1:2], zeros], axis=1).reshape(m, D_MODEL)
    p2 = jnp.concatenate([state0, zeros[:, 1:]], axis=1).reshape(m, D_MODEL)
    col = lambda j: pl.BlockSpec((m, D_MODEL), lambda i, j=j: (0, j))
    full = pl.BlockSpec((m, D_MODEL), lambda i: (0, 0))
    mix, u = pl.pallas_call(
        functools.partial(_conv_short_kernel, seq=seq),
        grid=(1,),
        in_specs=[col(0), col(1), col(2), _resident((CONV_WIDTH, D_MODEL)), full, full],
        out_specs=[full, full],
        out_shape=[jax.ShapeDtypeStruct((m, D_MODEL), BF16), jax.ShapeDtypeStruct((m, D_MODEL), F32)],
        compiler_params=_cparams(("arbitrary",)),
        name="conv_short",
    )(z, z, z, conv_w, p1, p2)
    return mix, u.reshape(batch, seq, D_MODEL)[:, seq - 2:]


def _xattn_kernel(q_ref, k_hbm, v_hbm, o_ref, kbuf, vbuf, sem, *, nb, layer, n_bt):
    i = pl.program_id(0)
    t = pl.program_id(1)
    slot = i % 2

    def copies(bt, sl):
        out = []
        for b in range(nb):
            for h in range(XA_HEADS):
                out.append(pltpu.make_async_copy(k_hbm.at[layer, bt * nb + b, :, h, :], kbuf.at[sl, b, h], sem.at[0, sl]))
                out.append(pltpu.make_async_copy(v_hbm.at[layer, bt * nb + b, :, h, :], vbuf.at[sl, b, h], sem.at[1, sl]))
        return out

    @pl.when((i == 0) & (t == 0))
    def _():
        for c in copies(0, 0):
            c.start()

    @pl.when(t == 0)
    def _():
        @pl.when(i + 1 < n_bt)
        def _():
            for c in copies(i + 1, 1 - slot):
                c.start()

        for c in copies(i, slot):
            c.wait()

    scale = XA_DIM ** -0.5
    tq = q_ref.shape[1]
    rows = -(-tq // SHORT_ROWS) * SHORT_ROWS
    pairs = [(b, h) for b in range(nb) for h in range(XA_HEADS)]
    scores = []
    for b, h in pairs:
        q = q_ref[b, :, h * XA_DIM:(h + 1) * XA_DIM]
        if rows != tq:
            q = jnp.concatenate([q.astype(F32), jnp.zeros((rows - tq, XA_DIM), F32)], axis=0).astype(BF16)
        k = kbuf[slot, b, h].astype(BF16)
        scores.append(lax.dot_general(q, k, NT_DIMS, preferred_element_type=F32) * scale)
    s = jnp.concatenate(scores, axis=0)
    p = jnp.exp(s - jnp.max(s, axis=-1, keepdims=True))
    p = (p / jnp.sum(p, axis=-1, keepdims=True)).astype(BF16)
    for n, (b, h) in enumerate(pairs):
        v = vbuf[slot, b, h].astype(BF16)
        o = jnp.dot(p[n * rows:(n + 1) * rows], v, preferred_element_type=F32)
        o_ref[b, :, h * XA_DIM:(h + 1) * XA_DIM] = o[:tq].astype(BF16)


def _xattn(z, batch, seq, qcol, mem_k, mem_v, layer, nb, tq):
    n_bt = batch // nb
    plane = pltpu.VMEM((2, nb, XA_HEADS, N_MEM, XA_DIM), F32)
    if nb == 1:
        q_arr = z.reshape(1, z.shape[0], z.shape[1])
        nt = seq // tq
        q_spec = pl.BlockSpec((1, tq, D_MODEL), lambda i, t: (0, i * nt + t, qcol))
        o_spec = pl.BlockSpec((1, tq, D_MODEL), lambda i, t: (0, i * nt + t, 0))
        o_shape = (1, batch * seq, D_MODEL)
    else:
        q_arr = z.reshape(batch, seq, z.shape[1])
        q_spec = pl.BlockSpec((nb, tq, D_MODEL), lambda i, t: (i, t, qcol))
        o_spec = pl.BlockSpec((nb, tq, D_MODEL), lambda i, t: (i, t, 0))
        o_shape = (batch, seq, D_MODEL)
    return pl.pallas_call(
        functools.partial(_xattn_kernel, nb=nb, layer=layer, n_bt=n_bt),
        grid=(n_bt, seq // tq),
        in_specs=[q_spec, pl.BlockSpec(memory_space=pl.ANY), pl.BlockSpec(memory_space=pl.ANY)],
        out_specs=o_spec,
        out_shape=jax.ShapeDtypeStruct(o_shape, BF16),
        scratch_shapes=[plane, plane, pltpu.SemaphoreType.DMA((2, 2))],
        compiler_params=_cparams(("arbitrary", "arbitrary")),
        name="xattn",
    )(q_arr, mem_k, mem_v).reshape(batch * seq, D_MODEL)


def _rotary(x, cos, sin):
    half = x.shape[-1] // 2
    x1, x2 = x[:, :half], x[:, half:]
    return jnp.concatenate([x1 * cos - x2 * sin, x1 * sin + x2 * cos], axis=-1)


def _group_norm_gate(o, gate, gn):
    mu = jnp.mean(o, axis=-1, keepdims=True)
    var = jnp.mean(jnp.square(o - mu), axis=-1, keepdims=True)
    return _silu(gate) * ((o - mu) * lax.rsqrt(var + EPS) * gn)


def _ret_long_kernel(full_ref, q_ref, k_ref, v_ref, g_ref, cos_ref, sin_ref, dec_ref, inn_ref, tail_ref,
                     gn_ref, mix_ref, sfin_ref, s_acc, *, chunk):
    t = pl.program_id(1)
    heads = range(RET_HEADS)

    @pl.when(t == 0)
    def _():
        s_acc[...] = jnp.zeros_like(s_acc)

    cos, sin = cos_ref[...], sin_ref[...]
    qs = [_rotary(q_ref[0, :, h * RET_DK:(h + 1) * RET_DK].astype(F32), cos, sin) for h in heads]
    ks = [_rotary(k_ref[0, :, h * RET_DK:(h + 1) * RET_DK].astype(F32), cos, sin) * (RET_DK ** -0.5) for h in heads]
    for c in range(q_ref.shape[1] // chunk):
        sl = slice(c * chunk, (c + 1) * chunk)
        qc = [qs[h][sl].astype(BF16) for h in heads]
        kc = [ks[h][sl] for h in heads]
        vc = [v_ref[0, sl, h * RET_DV:(h + 1) * RET_DV] for h in heads]
        s_old = [s_acc[h] for h in heads]
        scores = [lax.dot_general(qc[h], kc[h].astype(BF16), NT_DIMS, preferred_element_type=F32) * dec_ref[h]
                  for h in heads]
        cross = [jnp.dot(qc[h], s_old[h].astype(BF16), preferred_element_type=F32) * inn_ref[h][:, 0:1]
                 for h in heads]
        o = [jnp.dot(scores[h].astype(BF16), vc[h], preferred_element_type=F32) + cross[h] for h in heads]
        for h in heads:
            kt = (kc[h] * tail_ref[h][:, 0:1]).astype(BF16)
            s_acc[h] = full_ref[h] * s_old[h] + lax.dot_general(kt, vc[h], TN_DIMS, preferred_element_type=F32)
        og = jnp.concatenate(o, axis=0)
        mu = jnp.mean(og, axis=-1, keepdims=True)
        var = jnp.mean(jnp.square(og - mu), axis=-1, keepdims=True)
        on = (og - mu) * lax.rsqrt(var + EPS)
        for h in heads:
            vs = slice(h * RET_DV, (h + 1) * RET_DV)
            gate = g_ref[0, sl, vs]
            mix_ref[0, sl, vs] = _silu(gate) * (on[h * chunk:(h + 1) * chunk] * gn_ref[:, vs]).astype(BF16)

    @pl.when(t == pl.num_programs(1) - 1)
    def _():
        sfin_ref[0] = s_acc[...]


def _ret_tables(pos, chunk):
    half = RET_DK // 2
    inv = ROPE_BASE ** (-jnp.arange(half, dtype=F32) / half)
    ang = pos.astype(F32)[:, None] * inv[None, :]
    log_g = jnp.log1p(-jnp.exp2(-5.0 - jnp.arange(RET_HEADS, dtype=F32)))
    idx = jnp.arange(chunk)
    diff = idx[:, None] - idx[None, :]
    decay = jnp.where(diff[None] >= 0,
                      jnp.exp(jnp.maximum(diff, 0)[None].astype(F32) * log_g[:, None, None]), 0.0)
    inner = jnp.exp((idx + 1).astype(F32)[None, :] * log_g[:, None])
    tail = jnp.exp((chunk - 1 - idx).astype(F32)[None, :] * log_g[:, None])
    full = jnp.exp(chunk * log_g)
    return jnp.cos(ang), jnp.sin(ang), decay, inner, tail, full


def _ret_long(z, batch, seq, gn, pos):
    z3 = z.reshape(1, z.shape[0], z.shape[1])
    tc = min(512, seq)
    nt = seq // tc
    chunk = 2 * RET_CHUNK if tc % (2 * RET_CHUNK) == 0 else RET_CHUNK
    cos, sin, decay, inner, tail, full = _ret_tables(pos, chunk)
    lane_b = lambda a: jnp.broadcast_to(a[:, :, None], (RET_HEADS, chunk, LANES))
    qw = RET_HEADS * RET_DK
    vw = RET_HEADS * RET_DV
    state_spec = pl.BlockSpec((1, RET_HEADS, RET_DK, RET_DV), lambda b, t: (b, 0, 0, 0))
    mix, state = pl.pallas_call(
        functools.partial(_ret_long_kernel, chunk=chunk),
        grid=(batch, seq // tc),
        in_specs=[
            pl.BlockSpec(memory_space=pltpu.SMEM),
            pl.BlockSpec((1, tc, qw), lambda b, t: (0, b * nt + t, 0)),
            pl.BlockSpec((1, tc, qw), lambda b, t: (0, b * nt + t, 1)),
            pl.BlockSpec((1, tc, vw), lambda b, t: (0, b * nt + t, 1)),
            pl.BlockSpec((1, tc, vw), lambda b, t: (0, b * nt + t, 2)),
            pl.BlockSpec((tc, RET_DK // 2), lambda b, t: (t, 0)),
            pl.BlockSpec((tc, RET_DK // 2), lambda b, t: (t, 0)),
            _resident((RET_HEADS, chunk, chunk)), _resident((RET_HEADS, chunk, LANES)),
            _resident((RET_HEADS, chunk, LANES)), _resident((1, vw)),
        ],
        out_specs=[pl.BlockSpec((1, tc, vw), lambda b, t: (0, b * nt + t, 0)), state_spec],
        out_shape=[jax.ShapeDtypeStruct((1, batch * seq, vw), BF16),
                   jax.ShapeDtypeStruct((batch, RET_HEADS, RET_DK, RET_DV), F32)],
        scratch_shapes=[pltpu.VMEM((RET_HEADS, RET_DK, RET_DV), F32)],
        compiler_params=_cparams(("arbitrary", "arbitrary")),
        name="ret_long",
    )(full, z3, z3, z3, z3, cos, sin, decay, lane_b(inner), lane_b(tail), gn.reshape(1, -1))
    return mix.reshape(batch * seq, vw), state


def _ret_short_kernel(full_ref, q_ref, k_ref, v_ref, g_ref, cos_ref, sin_ref, dec_ref, inn_ref, tail_ref,
                      gn_ref, s0_ref, mix_ref, snew_ref, *, nb, seq):
    def pad_rows(a):
        return jnp.concatenate([a, jnp.zeros((SHORT_ROWS - seq, a.shape[1]), F32)], axis=0)

    cos, sin = cos_ref[...], sin_ref[...]
    for b in range(nb):
        for h in range(RET_HEADS):
            qs = slice(h * RET_DK, (h + 1) * RET_DK)
            vs = slice(h * RET_DV, (h + 1) * RET_DV)
            q = pad_rows(_rotary(q_ref[b, :, qs].astype(F32), cos, sin)).astype(BF16)
            k = pad_rows(_rotary(k_ref[b, :, qs].astype(F32), cos, sin) * (RET_DK ** -0.5))
            v = pad_rows(v_ref[b, :, vs].astype(F32)).astype(BF16)
            s_old = s0_ref[b, h]
            scores = lax.dot_general(q, k.astype(BF16), NT_DIMS, preferred_element_type=F32) * dec_ref[h]
            o = jnp.dot(scores.astype(BF16), v, preferred_element_type=F32)
            o = o + jnp.dot(q, s_old.astype(BF16), preferred_element_type=F32) * inn_ref[h]
            snew_ref[b, h] = full_ref[h] * s_old + lax.dot_general((k * tail_ref[h]).astype(BF16), v, TN_DIMS,
                                                                   preferred_element_type=F32)
            mix_ref[b, :, vs] = _group_norm_gate(o[:seq], g_ref[b, :, vs].astype(F32), gn_ref[:, vs]).astype(BF16)


def _ret_short(z, batch, seq, state0, gn, pos, nb=4):
    z3 = z.reshape(batch, seq, z.shape[1])
    cos, sin, decay, inner, tail, full = _ret_tables(pos, seq)
    qw = RET_HEADS * RET_DK
    vw = RET_HEADS * RET_DV
    small = lambda a: _resident(a.shape)
    rp = SHORT_ROWS - seq
    decay = jnp.pad(decay, ((0, 0), (0, rp), (0, rp)))
    inner3 = jnp.pad(inner, ((0, 0), (0, rp)))[:, :, None]
    tail3 = jnp.pad(tail, ((0, 0), (0, rp)))[:, :, None]
    mix, state = pl.pallas_call(
        functools.partial(_ret_short_kernel, nb=nb, seq=seq),
        grid=(batch // nb,),
        in_specs=[
            pl.BlockSpec(memory_space=pltpu.SMEM),
            pl.BlockSpec((nb, seq, qw), lambda i: (i, 0, 0)),
            pl.BlockSpec((nb, seq, qw), lambda i: (i, 0, 1)),
            pl.BlockSpec((nb, seq, vw), lambda i: (i, 0, 1)),
            pl.BlockSpec((nb, seq, vw), lambda i: (i, 0, 2)),
            small(cos), small(sin), small(decay), small(inner3), small(tail3),
            _resident((1, vw)),
            pl.BlockSpec((nb, RET_HEADS, RET_DK, RET_DV), lambda i: (i, 0, 0, 0)),
        ],
        out_specs=[
            pl.BlockSpec((nb, seq, vw), lambda i: (i, 0, 0)),
            pl.BlockSpec((nb, RET_HEADS, RET_DK, RET_DV), lambda i: (i, 0, 0, 0)),
        ],
        out_shape=[jax.ShapeDtypeStruct((batch, seq, vw), BF16),
                   jax.ShapeDtypeStruct((batch, RET_HEADS, RET_DK, RET_DV), F32)],
        compiler_params=_cparams(("arbitrary",)),
        name="ret_short",
    )(full, z3, z3, z3, z3, cos, sin, decay, inner3, tail3, gn.reshape(1, -1), state0)
    return mix.reshape(batch * seq, vw), state


def _route(logits_t):
    n = logits_t.shape[1]
    epg = EXPERTS_PER_GROUP
    row = lax.broadcasted_iota(I32, (epg, n), 0).astype(F32)
    neg = jnp.float32(-1e30)
    big = jnp.float32(epg)
    is_g = row < N_GROUPS
    lg = jnp.where(is_g, logits_t[0:epg], neg)
    gmax = jnp.max(lg, axis=0, keepdims=True)
    gidx = jnp.min(jnp.where(lg == gmax, row, big), axis=0, keepdims=True)
    pg_sel = 1.0 / jnp.sum(jnp.where(is_g, jnp.exp(lg - gmax), 0.0), axis=0, keepdims=True)
    sel = logits_t[epg:2 * epg]
    for g in range(1, N_GROUPS):
        sel = jnp.where(gidx == g, logits_t[epg * (g + 1):epg * (g + 2)], sel)
    e = jnp.exp(sel - jnp.max(sel, axis=0, keepdims=True))
    pe = e / jnp.sum(e, axis=0, keepdims=True)
    t1 = jnp.max(pe, axis=0, keepdims=True)
    i1 = jnp.min(jnp.where(pe == t1, row, big), axis=0, keepdims=True)
    pe2 = jnp.where(row == i1, -1.0, pe)
    t2 = jnp.max(pe2, axis=0, keepdims=True)
    i2 = jnp.min(jnp.where(pe2 == t2, row, big), axis=0, keepdims=True)
    den = t1 + t2
    gates = jnp.where(row == i1, t1 / den * pg_sel, 0.0) + jnp.where(row == i2, t2 / den * pg_sel, 0.0)
    ids = jnp.where(row == 0, gidx, 0.0)
    block = jnp.concatenate([gates, ids, jnp.zeros((LANES - 2 * epg, n), F32)], axis=0)
    return block.T


def _outproj_pair_kernel(xl_ref, mixl_ref, xal_ref, xs_ref, mixs_ref, xas_ref, wa_ref, wb_ref, g_ref, wr_ref,
                         o_ref, r_ref, *, sub, n_long):
    i = pl.program_id(0)

    @pl.when(i < n_long)
    def _():
        _outproj_kernel(xl_ref, mixl_ref, xal_ref, wa_ref, wb_ref, g_ref, wr_ref, o_ref, r_ref, sub=sub)

    @pl.when(i >= n_long)
    def _():
        _outproj_kernel(xs_ref, mixs_ref, xas_ref, wa_ref, wb_ref, g_ref, wr_ref, o_ref, r_ref, sub=sub)


def _outproj_kernel(x_ref, mix_ref, xa_ref, wa_ref, wb_ref, g_ref, wr_ref, o_ref, r_ref, *, sub):
    acc = jnp.dot(mix_ref[...], wa_ref[...], preferred_element_type=F32)
    acc = acc + jnp.dot(xa_ref[...], wb_ref[...], preferred_element_type=F32)
    o_ref[:, :D_MODEL] = x_ref[...] + acc
    logits = []
    for s in range(x_ref.shape[0] // sub):
        rows = slice(s * sub, (s + 1) * sub)
        h2 = _rms(o_ref[rows, :D_MODEL], g_ref[...])
        hi = h2.astype(BF16)
        lo = (h2 - hi.astype(F32)).astype(BF16)
        p_hi = lax.dot_general(wr_ref[...], hi, NT_DIMS, preferred_element_type=F32)
        p_lo = lax.dot_general(wr_ref[0:ROUTER_ROWS, :], lo, NT_DIMS, preferred_element_type=F32)
        logits.append((p_hi[:ROUTER_ROWS] + p_lo) + p_hi[ROUTER_ROWS:])
    route = _route(jnp.concatenate(logits, axis=1))
    o_ref[:, D_MODEL:] = route
    r_ref[...] = route


def _router_weights(w_group, w_router):
    epg = EXPERTS_PER_GROUP
    zeros = lambda r: jnp.zeros((DEPTH, r, D_MODEL), F32)
    wt = jnp.concatenate([jnp.swapaxes(w_group, 1, 2), zeros(epg - N_GROUPS), jnp.swapaxes(w_router, 1, 2),
                          zeros(ROUTER_ROWS - epg - N_EXPERTS)], axis=1)
    hi = wt.astype(BF16)
    return jnp.concatenate([hi, (wt - hi.astype(F32)).astype(BF16)], axis=1)


def _outproj_router(x, x_off, mix, xa, w_out, g_ffn, w_router2):
    m, cm = mix.shape
    tm = min(512, m)
    xb = x_off // tm
    row = lambda w, off=0: pl.BlockSpec((tm, w), lambda i: (i + off, 0))
    return pl.pallas_call(
        functools.partial(_outproj_kernel, sub=min(128, tm)),
        grid=(m // tm,),
        in_specs=[row(D_MODEL, xb), row(cm), row(D_MODEL), _resident((cm, D_MODEL)), _resident((D_MODEL, D_MODEL)),
                  _resident((1, D_MODEL)), _resident((2 * ROUTER_ROWS, D_MODEL))],
        out_specs=[row(AUG_W), row(LANES)],
        out_shape=[jax.ShapeDtypeStruct((m, AUG_W), F32), jax.ShapeDtypeStruct((m, LANES), F32)],
        compiler_params=_cparams(("arbitrary",)),
        name="outproj_router",
    )(x, mix, xa, w_out[:cm], w_out[cm:], g_ffn.reshape(1, D_MODEL), w_router2)


def _outproj_router_pair(long, short, w_out, g_ffn, w_router2):
    tm = 512
    cm = long[1].shape[1]
    nl, ns = long[0].shape[0] // tm, short[0].shape[0] // tm
    first = lambda w: pl.BlockSpec((tm, w), lambda i: (jnp.minimum(i, nl - 1), 0))
    second = lambda w: pl.BlockSpec((tm, w), lambda i: (jnp.maximum(i - nl, 0), 0))
    row = lambda w: pl.BlockSpec((tm, w), lambda i: (i, 0))
    m = (nl + ns) * tm
    return pl.pallas_call(
        functools.partial(_outproj_pair_kernel, sub=128, n_long=nl),
        grid=(nl + ns,),
        in_specs=[first(D_MODEL), first(cm), first(D_MODEL), second(D_MODEL), second(cm), second(D_MODEL),
                  _resident((cm, D_MODEL)), _resident((D_MODEL, D_MODEL)),
                  _resident((1, D_MODEL)), _resident((2 * ROUTER_ROWS, D_MODEL))],
        out_specs=[row(AUG_W), row(LANES)],
        out_shape=[jax.ShapeDtypeStruct((m, AUG_W), F32), jax.ShapeDtypeStruct((m, LANES), F32)],
        compiler_params=_cparams(("arbitrary",)),
        name="outproj_router_pair",
    )(*long, *short, w_out[:cm], w_out[cm:], g_ffn.reshape(1, D_MODEL), w_router2)


STEP_VALID, STEP_FIRST, STEP_LAST, STEP_NEWGROUP = 1, 2, 4, 8


def _moe_plan(gid, tm):
    m = gid.shape[0]
    ntile = m // tm
    nstep = ntile + N_GROUPS - 1
    counts = jnp.sum((gid[:, None] == jnp.arange(N_GROUPS, dtype=I32)[None, :]).astype(I32), axis=0)
    gend = jnp.cumsum(counts)
    gstart = gend - counts
    row_tok = jnp.argsort(gid, stable=True).astype(I32)
    inner_ok = (counts[1:] > 0) & (gstart[1:] % tm != 0)
    cuts = jnp.concatenate([jnp.arange(ntile, dtype=I32) * tm, jnp.where(inner_ok, gstart[1:], m)])
    cut_ok = jnp.concatenate([jnp.ones((ntile,), bool), inner_ok])
    rank = jnp.sum(((cuts[None, :] < cuts[:, None]) & cut_ok[None, :]).astype(I32), axis=1)
    sel = jnp.where(cut_ok, rank, nstep)[None, :] == jnp.arange(nstep, dtype=I32)[:, None]
    valid = jnp.any(sel, axis=1)
    start = jnp.sum(jnp.where(sel, cuts[None, :], 0), axis=1)
    start = jnp.where(valid, start, jnp.max(start))
    st_tile = start // tm
    st_gid = jnp.sum((gend[None, :] <= start[:, None]).astype(I32), axis=1)
    first = valid & (start % tm == 0)
    next_first = jnp.concatenate([first[1:], jnp.ones((1,), bool)])
    next_valid = jnp.concatenate([valid[1:], jnp.zeros((1,), bool)])
    last = valid & (next_first | ~next_valid)
    prev_gid = jnp.concatenate([jnp.full((1,), -1, I32), st_gid[:-1]])
    newgroup = valid & (st_gid != prev_gid)
    flags = valid * STEP_VALID + first * STEP_FIRST + last * STEP_LAST + newgroup * STEP_NEWGROUP
    return st_tile, st_gid, flags.astype(I32), row_tok


def _moe_kernel(tile_ref, gid_ref, flag_ref, tok_ref, xaug_hbm, g_ref, fin_ref, wgu_hbm, wd_hbm, out_hbm,
                xbuf, ybuf, wgu_bf, wd_bf, wgu_stage, wd_stage, gsem, ssem, wsem, *, tm, ntile, layer, final_norm):
    s = pl.program_id(0)
    tile = tile_ref[s]
    grp = gid_ref[s]
    flags = flag_ref[s]
    valid = (flags & STEP_VALID) > 0
    first = (flags & STEP_FIRST) > 0
    last = (flags & STEP_LAST) > 0
    newgroup = (flags & STEP_NEWGROUP) > 0
    slot = tile % 2

    def weight_copies(j):
        e = grp * EXPERTS_PER_GROUP + j
        return (pltpu.make_async_copy(wgu_hbm.at[layer, e], wgu_stage.at[j % 2], wsem.at[j % 2]),
                pltpu.make_async_copy(wd_hbm.at[layer, e], wd_stage.at[j % 2], wsem.at[j % 2]))

    def gather_start(t, sl):
        for r in range(tm):
            tok = tok_ref[t * tm + r]
            pltpu.make_async_copy(xaug_hbm.at[pl.ds(tok, 1)], xbuf.at[sl, pl.ds(r, 1)],
                                  gsem.at[sl]).start(priority=r % 2)

    def scatter_start(t, sl):
        for r in range(tm):
            tok = tok_ref[t * tm + r]
            pltpu.make_async_copy(ybuf.at[sl, pl.ds(r, 1)], out_hbm.at[pl.ds(tok, 1)],
                                  ssem.at[sl]).start(priority=r % 2)

    def gather_wait(sl):
        pltpu.make_async_copy(xaug_hbm.at[pl.ds(0, tm)], xbuf.at[sl], gsem.at[sl]).wait()

    def scatter_wait(sl):
        pltpu.make_async_copy(ybuf.at[sl], out_hbm.at[pl.ds(0, tm)], ssem.at[sl]).wait()

    @pl.when(s == 0)
    def _():
        gather_start(0, 0)

    @pl.when(first & (tile + 1 < ntile))
    def _():
        gather_start(tile + 1, 1 - slot)

    @pl.when(first)
    def _():
        gather_wait(slot)

        @pl.when(tile >= 2)
        def _():
            scatter_wait(slot)

        ybuf[slot] = xbuf[slot, :, 0:D_MODEL]

    def experts(stream_weights):
        xn = xbuf[slot, :, 0:D_MODEL]
        gl = xbuf[slot, :, GATE_LANE0:AUG_W]
        mine = gl[:, GID_LANE:GID_LANE + 1] == grp.astype(F32)
        h2 = _rms(xn, g_ref[...]).astype(BF16)
        acc = jnp.zeros((tm, D_MODEL), F32)
        if stream_weights:
            for c in weight_copies(0):
                c.start()
        for j in range(EXPERTS_PER_GROUP):
            if stream_weights:
                if j + 1 < EXPERTS_PER_GROUP:
                    for c in weight_copies(j + 1):
                        c.start()
                for c in weight_copies(j):
                    c.wait()
                wgu_bf[j] = wgu_stage[j % 2].astype(BF16)
                wd_bf[j] = wd_stage[j % 2].astype(BF16)
            gu = jnp.dot(h2, wgu_bf[j], preferred_element_type=F32)
            gate = jnp.where(mine, gl[:, j:j + 1], 0.0)
            act = _silu(gu[:, :EXPERT_FF]) * gu[:, EXPERT_FF:] * gate
            acc = acc + jnp.dot(act.astype(BF16), wd_bf[j], preferred_element_type=F32)
        ybuf[slot] = ybuf[slot] + acc

    @pl.when(newgroup)
    def _():
        experts(True)

    @pl.when(valid & jnp.logical_not(newgroup))
    def _():
        experts(False)

    @pl.when(last)
    def _():
        if final_norm:
            ybuf[slot] = _rms(ybuf[slot], fin_ref[...])
        scatter_start(tile, slot)

        @pl.when(tile == ntile - 1)
        def _():
            if ntile >= 2:
                scatter_wait(1 - slot)
            scatter_wait(slot)


def _moe(xaug, route, g_ffn, final_g, w_gu, w_down, layer, final_norm):
    m = xaug.shape[0]
    tm = min(MOE_TM, m)
    gid = route[:, GID_LANE].astype(I32)
    st_tile, st_gid, flags, row_tok = _moe_plan(gid, tm)
    epg = EXPERTS_PER_GROUP
    grid_spec = pltpu.PrefetchScalarGridSpec(
        num_scalar_prefetch=4,
        grid=(st_tile.shape[0],),
        in_specs=[
            pl.BlockSpec(memory_space=pl.ANY),
            pl.BlockSpec((1, D_MODEL), lambda s, *_: (0, 0)),
            pl.BlockSpec((1, D_MODEL), lambda s, *_: (0, 0)),
            pl.BlockSpec(memory_space=pl.ANY),
            pl.BlockSpec(memory_space=pl.ANY),
        ],
        out_specs=pl.BlockSpec(memory_space=pl.ANY),
        scratch_shapes=[
            pltpu.VMEM((2, tm, AUG_W), F32),
            pltpu.VMEM((2, tm, D_MODEL), F32),
            pltpu.VMEM((epg, D_MODEL, 2 * EXPERT_FF), BF16),
            pltpu.VMEM((epg, EXPERT_FF, D_MODEL), BF16),
            pltpu.VMEM((2, D_MODEL, 2 * EXPERT_FF), F32),
            pltpu.VMEM((2, EXPERT_FF, D_MODEL), F32),
            pltpu.SemaphoreType.DMA((2,)),
            pltpu.SemaphoreType.DMA((2,)),
            pltpu.SemaphoreType.DMA((2,)),
        ],
    )
    return pl.pallas_call(
        functools.partial(_moe_kernel, tm=tm, ntile=m // tm, layer=layer, final_norm=final_norm),
        grid_spec=grid_spec,
        out_shape=jax.ShapeDtypeStruct((m, D_MODEL), F32),
        compiler_params=_cparams(("arbitrary",)),
        name="moe",
    )(st_tile, st_gid, flags, row_tok, xaug, g_ffn.reshape(1, D_MODEL), final_g.reshape(1, D_MODEL), w_gu, w_down)


SHORT_NB = 4


def _forward(x_long, x_short, pos_long, pos_short, conv_state, ret_state, mem_long, mem_short, wts):
    assert DEPTH == 2
    (bl, tl), (bs, ts) = x_long.shape[:2], x_short.shape[:2]
    ml, ms = bl * tl, bs * ts
    xl, xs = x_long.reshape(ml, D_MODEL), x_short.reshape(ms, D_MODEL)
    tq = min(1024, tl)
    moe = lambda rows, route, i: _moe(rows, route, wts["norm_ffn"][i], wts["final_norm"], wts["moe_w_gate_up"],
                                      wts["moe_w_down"], i, final_norm=(i == DEPTH - 1))

    zl = _inproj(xl, wts["norm_mix"][0], wts["conv_w_in"])
    zs = _inproj(xs, wts["norm_mix"][0], wts["conv_w_in"])
    mix_l, conv_l = _conv_long(zl, wts["conv_w"], jnp.zeros((bl, CONV_WIDTH - 1, D_MODEL), F32), bl, tl)
    mix_s, conv_s = _conv_short(zs, wts["conv_w"], conv_state, bs, ts)
    xa_l = _xattn(zl, bl, tl, 3, *mem_long, 0, 1, tq)
    xa_s = _xattn(zs, bs, ts, 3, *mem_short, 0, SHORT_NB, ts)
    routed = _outproj_router_pair((xl, mix_l, xa_l), (xs, mix_s, xa_s), wts["conv_w_out"], wts["norm_ffn"][0],
                                  wts["w_router"][0])
    x1 = moe(*routed, 0)

    z = _inproj(x1, wts["norm_mix"][1], wts["ret_w_in"])
    zs = z[ml:]
    mix_l, ret_l = _ret_long(z, bl, tl, wts["ret_gn"], pos_long)
    mix_s, ret_s = _ret_short(zs, bs, ts, ret_state, wts["ret_gn"], pos_short)
    xa_l = _xattn(z, bl, tl, 6, *mem_long, 1, 1, tq)
    xa_s = _xattn(zs, bs, ts, 6, *mem_short, 1, SHORT_NB, ts)
    y_l = moe(*_outproj_router(x1, 0, mix_l, xa_l, wts["ret_w_out"], wts["norm_ffn"][1], wts["w_router"][1]), 1)
    y_s = moe(*_outproj_router(x1, ml, mix_s, xa_s, wts["ret_w_out"], wts["norm_ffn"][1], wts["w_router"][1]), 1)
    return (y_l.reshape(bl, tl, D_MODEL), y_s.reshape(bs, ts, D_MODEL), conv_l, conv_s, ret_l, ret_s)


def kernel(x_prompt, x_sample, state_conv, state_ret, cache_mem_k, cache_mem_v, mem_prompt, norm_mix, norm_ffn,
           mem_norm, w_mem_kv, conv_w_in, conv_w, conv_w_out, ret_w_in, ret_gn, ret_w_out, moe_w_group,
           moe_w_router, moe_w_gate_up, moe_w_down, final_norm):
    batch, seq = x_prompt.shape[0], x_prompt.shape[1]
    dec_batch, dec_seq = x_sample.shape[0], x_sample.shape[1]
    w_router = _router_weights(moe_w_group, moe_w_router)
    wts = dict(
        norm_mix=norm_mix, norm_ffn=norm_ffn, final_norm=final_norm, w_router=w_router,
        conv_w_in=conv_w_in[0].astype(BF16), conv_w=conv_w[0], conv_w_out=conv_w_out[0].astype(BF16),
        ret_w_in=ret_w_in[0].astype(BF16), ret_gn=ret_gn[0], ret_w_out=ret_w_out[0].astype(BF16),
        moe_w_gate_up=moe_w_gate_up, moe_w_down=moe_w_down,
    )
    mem_k_prompt, mem_v_prompt = _mem_kv(mem_prompt.reshape(batch * N_MEM, D_MODEL), mem_norm, w_mem_kv.astype(BF16))

    y_prompt, y_sample, conv_p, conv_s, ret_p, ret_s = _forward(
        x_prompt, x_sample, jnp.arange(seq, dtype=I32), PAST_LEN + jnp.arange(dec_seq, dtype=I32),
        state_conv[0], state_ret[0], (mem_k_prompt, mem_v_prompt), (cache_mem_k, cache_mem_v), wts)
    return (y_prompt, y_sample, conv_p[None], conv_s[None], ret_p[None], ret_s[None], mem_k_prompt, mem_v_prompt)
```

```python
import functools

import jax
import jax.numpy as jnp
from jax import lax
from jax.experimental import pallas as pl
from jax.experimental.pallas import tpu as pltpu

F32 = jnp.float32
BF16 = jnp.bfloat16
I32 = jnp.int32

D_MODEL = 1024
DEPTH = 2
CONV_WIDTH = 3
RET_HEADS = 4
RET_DK = 256
RET_DV = 512
RET_CHUNK = 128
ROPE_BASE = 10000.0
N_MEM = 256
XA_HEADS = 4
XA_DIM = 256
N_GROUPS = 4
EXPERTS_PER_GROUP = 8
N_EXPERTS = N_GROUPS * EXPERTS_PER_GROUP
EXPERT_FF = 256
EPS = 1e-6

LANES = 128
AUG_W = D_MODEL + LANES
GATE_LANE0 = D_MODEL
GID_LANE = EXPERTS_PER_GROUP
ROUTER_ROWS = 64
VMEM_LIMIT = 56 * 1024 * 1024
MOE_TM = 512
SHORT_ROWS = 8
PAST_LEN = 16384
NT_DIMS = (((1,), (1,)), ((), ()))
TN_DIMS = (((0,), (0,)), ((), ()))


def _cparams(sem):
    return pltpu.CompilerParams(dimension_semantics=sem, vmem_limit_bytes=VMEM_LIMIT)


def _rms(x, g):
    return x * lax.rsqrt(jnp.mean(x * x, axis=-1, keepdims=True) + EPS) * g


def _silu(x):
    return x * jax.nn.sigmoid(x)


def _resident(shape):
    nd = len(shape)
    return pl.BlockSpec(shape, lambda *_: (0,) * nd, pipeline_mode=pl.Buffered(1))


def _memkv_kernel(x_ref, g_ref, w_ref, k_ref, v_ref, *, nb):
    h = _rms(x_ref[...], g_ref[...]).astype(BF16)
    kv = jnp.dot(h, w_ref[...], preferred_element_type=F32)
    for b in range(nb):
        rows = slice(b * N_MEM, (b + 1) * N_MEM)
        for hd in range(XA_HEADS):
            k_ref[b, :, hd, :] = kv[rows, hd * XA_DIM:(hd + 1) * XA_DIM]
            v_ref[b, :, hd, :] = kv[rows, D_MODEL + hd * XA_DIM:D_MODEL + (hd + 1) * XA_DIM]


def _mem_kv(mem2d, mem_norm, w_kv):
    batch = mem2d.shape[0] // N_MEM
    nb = min(2, batch)
    tm = nb * N_MEM
    out = jax.ShapeDtypeStruct((DEPTH, batch, N_MEM, XA_HEADS, XA_DIM), F32)
    out_spec = pl.BlockSpec((None, nb, N_MEM, XA_HEADS, XA_DIM), lambda l, j: (l, j, 0, 0, 0))
    return pl.pallas_call(
        functools.partial(_memkv_kernel, nb=nb),
        grid=(DEPTH, batch // nb),
        in_specs=[
            pl.BlockSpec((tm, D_MODEL), lambda l, j: (j, 0)),
            pl.BlockSpec((None, 1, D_MODEL), lambda l, j: (l, 0, 0)),
            pl.BlockSpec((None, D_MODEL, 2 * D_MODEL), lambda l, j: (l, 0, 0)),
        ],
        out_specs=[out_spec, out_spec],
        out_shape=[out, out],
        compiler_params=_cparams(("arbitrary", "arbitrary")),
        name="mem_kv",
    )(mem2d, mem_norm.reshape(DEPTH, 1, D_MODEL), w_kv)


def _inproj_kernel(x_ref, g_ref, w_ref, o_ref, *, tn):
    h = _rms(x_ref[...], g_ref[...]).astype(BF16)
    for j in range(o_ref.shape[1] // tn):
        sl = slice(j * tn, (j + 1) * tn)
        o_ref[:, sl] = jnp.dot(h, w_ref[:, sl], preferred_element_type=F32).astype(BF16)


def _inproj(x, g, w):
    m, n = x.shape[0], w.shape[1]
    tm = min(512, m)
    return pl.pallas_call(
        functools.partial(_inproj_kernel, tn=512),
        grid=(m // tm,),
        in_specs=[
            pl.BlockSpec((tm, D_MODEL), lambda i: (i, 0)),
            _resident((1, D_MODEL)),
            _resident((D_MODEL, n)),
        ],
        out_specs=pl.BlockSpec((tm, n), lambda i: (i, 0)),
        out_shape=jax.ShapeDtypeStruct((m, n), BF16),
        compiler_params=_cparams(("arbitrary",)),
        name="inproj",
    )(x, g.reshape(1, D_MODEL), w)


def _conv_taps(u, prev1, prev2, w_ref):
    return (w_ref[0:1, :] * prev2 + w_ref[1:2, :] * prev1) + w_ref[2:3, :] * u


def _inproj_conv_kernel(x_ref, g_ref, w_ref, cw_ref, mix_ref, q_ref, st_ref, carry, *, tiles_per_seq):
    i = pl.program_id(0)
    tt = x_ref.shape[0]
    col = lambda j: slice(j * D_MODEL, (j + 1) * D_MODEL)
    h = _rms(x_ref[...], g_ref[...]).astype(BF16)
    proj = lambda j: jnp.dot(h, w_ref[:, col(j)], preferred_element_type=F32)

    @pl.when(i % tiles_per_seq == 0)
    def _():
        carry[...] = jnp.zeros_like(carry)

    q_ref[...] = proj(3).astype(BF16)
    u = proj(1) * proj(2)
    pm2 = carry[6:7, :]
    pm1 = carry[7:8, :]
    row = lax.broadcasted_iota(I32, u.shape, 0)
    prev1 = jnp.where(row == 0, pm1, pltpu.roll(u, 1, 0))
    prev2 = jnp.where(row == 0, pm2, jnp.where(row == 1, pm1, pltpu.roll(u, 2, 0)))
    mix_ref[...] = (proj(0) * _conv_taps(u, prev1, prev2, cw_ref)).astype(BF16)
    carry[...] = u[tt - 8:tt, :]
    st_ref[0] = carry[6:8, :]


def _inproj_conv(x, g, w, conv_w, batch, seq):
    tt = min(512, seq)
    tps = seq // tt
    row = pl.BlockSpec((tt, D_MODEL), lambda i: (i, 0))
    out = jax.ShapeDtypeStruct((batch * seq, D_MODEL), BF16)
    return pl.pallas_call(
        functools.partial(_inproj_conv_kernel, tiles_per_seq=tps),
        grid=(batch * tps,),
        in_specs=[row, _resident((1, D_MODEL)), _resident(w.shape), _resident((CONV_WIDTH, D_MODEL))],
        out_specs=[row, row, pl.BlockSpec((1, 2, D_MODEL), lambda i: (i // tps, 0, 0))],
        out_shape=[out, out, jax.ShapeDtypeStruct((batch, 2, D_MODEL), F32)],
        scratch_shapes=[pltpu.VMEM((8, D_MODEL), F32)],
        compiler_params=_cparams(("arbitrary",)),
        name="inproj_conv",
    )(x, g.reshape(1, D_MODEL), w, conv_w)


def _conv_short_kernel(b_ref, c_ref, v_ref, w_ref, p1_ref, p2_ref, mix_ref, u_ref, *, seq):
    u = c_ref[...].astype(F32) * v_ref[...].astype(F32)
    t = lax.broadcasted_iota(I32, u.shape, 0) % seq
    prev1 = jnp.where(t == 0, p1_ref[...], pltpu.roll(u, 1, 0))
    prev2 = jnp.where(t <= 1, p2_ref[...], pltpu.roll(u, 2, 0))
    mix_ref[...] = (b_ref[...].astype(F32) * _conv_taps(u, prev1, prev2, w_ref)).astype(BF16)
    u_ref[...] = u


def _conv_short(z, conv_w, state0, batch, seq):
    m = batch * seq
    zeros = jnp.zeros((batch, seq - 1, D_MODEL), F32)
    p1 = jnp.concatenate([state0[:, 1:2], zeros], axis=1).reshape(m, D_MODEL)
    p2 = jnp.concatenate([state0, zeros[:, 1:]], axis=1).reshape(m, D_MODEL)
    col = lambda j: pl.BlockSpec((m, D_MODEL), lambda i, j=j: (0, j))
    full = pl.BlockSpec((m, D_MODEL), lambda i: (0, 0))
    mix, u = pl.pallas_call(
        functools.partial(_conv_short_kernel, seq=seq),
        grid=(1,),
        in_specs=[col(0), col(1), col(2), _resident((CONV_WIDTH, D_MODEL)), full, full],
        out_specs=[full, full],
        out_shape=[jax.ShapeDtypeStruct((m, D_MODEL), BF16), jax.ShapeDtypeStruct((m, D_MODEL), F32)],
        compiler_params=_cparams(("arbitrary",)),
        name="conv_short",
    )(z, z, z, conv_w, p1, p2)
    return mix, u.reshape(batch, seq, D_MODEL)[:, seq - 2:]


def _xattn_kernel(q_ref, k_hbm, v_hbm, o_ref, kbuf, vbuf, sem, *, nb, layer, n_bt):
    i = pl.program_id(0)
    t = pl.program_id(1)
    slot = i % 2

    def copies(bt, sl):
        out = []
        for b in range(nb):
            for h in range(XA_HEADS):
                out.append(pltpu.make_async_copy(k_hbm.at[layer, bt * nb + b, :, h, :], kbuf.at[sl, b, h], sem.at[0, sl]))
                out.append(pltpu.make_async_copy(v_hbm.at[layer, bt * nb + b, :, h, :], vbuf.at[sl, b, h], sem.at[1, sl]))
        return out

    @pl.when((i == 0) & (t == 0))
    def _():
        for c in copies(0, 0):
            c.start()

    @pl.when(t == 0)
    def _():
        @pl.when(i + 1 < n_bt)
        def _():
            for c in copies(i + 1, 1 - slot):
                c.start()

        for c in copies(i, slot):
            c.wait()

    scale = XA_DIM ** -0.5
    tq = q_ref.shape[1]
    rows = -(-tq // SHORT_ROWS) * SHORT_ROWS
    pairs = [(b, h) for b in range(nb) for h in range(XA_HEADS)]
    scores = []
    for b, h in pairs:
        q = q_ref[b, :, h * XA_DIM:(h + 1) * XA_DIM]
        if rows != tq:
            q = jnp.concatenate([q.astype(F32), jnp.zeros((rows - tq, XA_DIM), F32)], axis=0).astype(BF16)
        k = kbuf[slot, b, h].astype(BF16)
        scores.append(lax.dot_general(q, k, NT_DIMS, preferred_element_type=F32) * scale)
    s = jnp.concatenate(scores, axis=0)
    p = jnp.exp(s - jnp.max(s, axis=-1, keepdims=True))
    p = (p / jnp.sum(p, axis=-1, keepdims=True)).astype(BF16)
    for n, (b, h) in enumerate(pairs):
        v = vbuf[slot, b, h].astype(BF16)
        o = jnp.dot(p[n * rows:(n + 1) * rows], v, preferred_element_type=F32)
        o_ref[b, :, h * XA_DIM:(h + 1) * XA_DIM] = o[:tq].astype(BF16)


def _xattn(z, batch, seq, qcol, mem_k, mem_v, layer, nb, tq):
    n_bt = batch // nb
    plane = pltpu.VMEM((2, nb, XA_HEADS, N_MEM, XA_DIM), F32)
    if nb == 1:
        q_arr = z.reshape(1, z.shape[0], z.shape[1])
        nt = seq // tq
        q_spec = pl.BlockSpec((1, tq, D_MODEL), lambda i, t: (0, i * nt + t, qcol))
        o_spec = pl.BlockSpec((1, tq, D_MODEL), lambda i, t: (0, i * nt + t, 0))
        o_shape = (1, batch * seq, D_MODEL)
    else:
        q_arr = z.reshape(batch, seq, z.shape[1])
        q_spec = pl.BlockSpec((nb, tq, D_MODEL), lambda i, t: (i, t, qcol))
        o_spec = pl.BlockSpec((nb, tq, D_MODEL), lambda i, t: (i, t, 0))
        o_shape = (batch, seq, D_MODEL)
    return pl.pallas_call(
        functools.partial(_xattn_kernel, nb=nb, layer=layer, n_bt=n_bt),
        grid=(n_bt, seq // tq),
        in_specs=[q_spec, pl.BlockSpec(memory_space=pl.ANY), pl.BlockSpec(memory_space=pl.ANY)],
        out_specs=o_spec,
        out_shape=jax.ShapeDtypeStruct(o_shape, BF16),
        scratch_shapes=[plane, plane, pltpu.SemaphoreType.DMA((2, 2))],
        compiler_params=_cparams(("arbitrary", "arbitrary")),
        name="xattn",
    )(q_arr, mem_k, mem_v).reshape(batch * seq, D_MODEL)


def _rotary(x, cos, sin):
    half = x.shape[-1] // 2
    x1, x2 = x[:, :half], x[:, half:]
    return jnp.concatenate([x1 * cos - x2 * sin, x1 * sin + x2 * cos], axis=-1)


def _group_norm_gate(o, gate, gn):
    mu = jnp.mean(o, axis=-1, keepdims=True)
    var = jnp.mean(jnp.square(o - mu), axis=-1, keepdims=True)
    return _silu(gate) * ((o - mu) * lax.rsqrt(var + EPS) * gn)


def _ret_long_kernel(full_ref, q_ref, k_ref, v_ref, g_ref, cos_ref, sin_ref, dec_ref, inn_ref, tail_ref,
                     gn_ref, mix_ref, sfin_ref, s_acc, *, chunk):
    t = pl.program_id(1)
    heads = range(RET_HEADS)

    @pl.when(t == 0)
    def _():
        s_acc[...] = jnp.zeros_like(s_acc)

    cos, sin = cos_ref[...], sin_ref[...]
    qs = [_rotary(q_ref[0, :, h * RET_DK:(h + 1) * RET_DK].astype(F32), cos, sin) for h in heads]
    ks = [_rotary(k_ref[0, :, h * RET_DK:(h + 1) * RET_DK].astype(F32), cos, sin) * (RET_DK ** -0.5) for h in heads]
    for c in range(q_ref.shape[1] // chunk):
        sl = slice(c * chunk, (c + 1) * chunk)
        qc = [qs[h][sl].astype(BF16) for h in heads]
        kc = [ks[h][sl] for h in heads]
        vc = [v_ref[0, sl, h * RET_DV:(h + 1) * RET_DV] for h in heads]
        s_old = [s_acc[h] for h in heads]
        scores = [lax.dot_general(qc[h], kc[h].astype(BF16), NT_DIMS, preferred_element_type=F32) * dec_ref[h]
                  for h in heads]
        cross = [jnp.dot(qc[h], s_old[h].astype(BF16), preferred_element_type=F32) * inn_ref[h][:, 0:1]
                 for h in heads]
        o = [jnp.dot(scores[h].astype(BF16), vc[h], preferred_element_type=F32) + cross[h] for h in heads]
        for h in heads:
            kt = (kc[h] * tail_ref[h][:, 0:1]).astype(BF16)
            s_acc[h] = full_ref[h] * s_old[h] + lax.dot_general(kt, vc[h], TN_DIMS, preferred_element_type=F32)
        og = jnp.concatenate(o, axis=0)
        mu = jnp.mean(og, axis=-1, keepdims=True)
        var = jnp.mean(jnp.square(og - mu), axis=-1, keepdims=True)
        on = (og - mu) * lax.rsqrt(var + EPS)
        for h in heads:
            vs = slice(h * RET_DV, (h + 1) * RET_DV)
            gate = g_ref[0, sl, vs]
            mix_ref[0, sl, vs] = _silu(gate) * (on[h * chunk:(h + 1) * chunk] * gn_ref[:, vs]).astype(BF16)

    @pl.when(t == pl.num_programs(1) - 1)
    def _():
        sfin_ref[0] = s_acc[...]


def _ret_tables(pos, chunk):
    half = RET_DK // 2
    inv = ROPE_BASE ** (-jnp.arange(half, dtype=F32) / half)
    ang = pos.astype(F32)[:, None] * inv[None, :]
    log_g = jnp.log1p(-jnp.exp2(-5.0 - jnp.arange(RET_HEADS, dtype=F32)))
    idx = jnp.arange(chunk)
    diff = idx[:, None] - idx[None, :]
    decay = jnp.where(diff[None] >= 0,
                      jnp.exp(jnp.maximum(diff, 0)[None].astype(F32) * log_g[:, None, None]), 0.0)
    inner = jnp.exp((idx + 1).astype(F32)[None, :] * log_g[:, None])
    tail = jnp.exp((chunk - 1 - idx).astype(F32)[None, :] * log_g[:, None])
    full = jnp.exp(chunk * log_g)
    return jnp.cos(ang), jnp.sin(ang), decay, inner, tail, full


def _ret_long(z, batch, seq, gn, pos):
    z3 = z.reshape(1, z.shape[0], z.shape[1])
    tc = min(512, seq)
    nt = seq // tc
    chunk = 2 * RET_CHUNK if tc % (2 * RET_CHUNK) == 0 else RET_CHUNK
    cos, sin, decay, inner, tail, full = _ret_tables(pos, chunk)
    lane_b = lambda a: jnp.broadcast_to(a[:, :, None], (RET_HEADS, chunk, LANES))
    qw = RET_HEADS * RET_DK
    vw = RET_HEADS * RET_DV
    state_spec = pl.BlockSpec((1, RET_HEADS, RET_DK, RET_DV), lambda b, t: (b, 0, 0, 0))
    mix, state = pl.pallas_call(
        functools.partial(_ret_long_kernel, chunk=chunk),
        grid=(batch, seq // tc),
        in_specs=[
            pl.BlockSpec(memory_space=pltpu.SMEM),
            pl.BlockSpec((1, tc, qw), lambda b, t: (0, b * nt + t, 0)),
            pl.BlockSpec((1, tc, qw), lambda b, t: (0, b * nt + t, 1)),
            pl.BlockSpec((1, tc, vw), lambda b, t: (0, b * nt + t, 1)),
            pl.BlockSpec((1, tc, vw), lambda b, t: (0, b * nt + t, 2)),
            pl.BlockSpec((tc, RET_DK // 2), lambda b, t: (t, 0)),
            pl.BlockSpec((tc, RET_DK // 2), lambda b, t: (t, 0)),
            _resident((RET_HEADS, chunk, chunk)), _resident((RET_HEADS, chunk, LANES)),
            _resident((RET_HEADS, chunk, LANES)), _resident((1, vw)),
        ],
        out_specs=[pl.BlockSpec((1, tc, vw), lambda b, t: (0, b * nt + t, 0)), state_spec],
        out_shape=[jax.ShapeDtypeStruct((1, batch * seq, vw), BF16),
                   jax.ShapeDtypeStruct((batch, RET_HEADS, RET_DK, RET_DV), F32)],
        scratch_shapes=[pltpu.VMEM((RET_HEADS, RET_DK, RET_DV), F32)],
        compiler_params=_cparams(("arbitrary", "arbitrary")),
        name="ret_long",
    )(full, z3, z3, z3, z3, cos, sin, decay, lane_b(inner), lane_b(tail), gn.reshape(1, -1))
    return mix.reshape(batch * seq, vw), state


def _ret_short_kernel(full_ref, q_ref, k_ref, v_ref, g_ref, cos_ref, sin_ref, dec_ref, inn_ref, tail_ref,
                      gn_ref, s0_ref, mix_ref, snew_ref, *, nb, seq):
    def pad_rows(a):
        return jnp.concatenate([a, jnp.zeros((SHORT_ROWS - seq, a.shape[1]), F32)], axis=0)

    cos, sin = cos_ref[...], sin_ref[...]
    for b in range(nb):
        for h in range(RET_HEADS):
            qs = slice(h * RET_DK, (h + 1) * RET_DK)
            vs = slice(h * RET_DV, (h + 1) * RET_DV)
            q = pad_rows(_rotary(q_ref[b, :, qs].astype(F32), cos, sin)).astype(BF16)
            k = pad_rows(_rotary(k_ref[b, :, qs].astype(F32), cos, sin) * (RET_DK ** -0.5))
            v = pad_rows(v_ref[b, :, vs].astype(F32)).astype(BF16)
            s_old = s0_ref[b, h]
            scores = lax.dot_general(q, k.astype(BF16), NT_DIMS, preferred_element_type=F32) * dec_ref[h]
            o = jnp.dot(scores.astype(BF16), v, preferred_element_type=F32)
            o = o + jnp.dot(q, s_old.astype(BF16), preferred_element_type=F32) * inn_ref[h]
            snew_ref[b, h] = full_ref[h] * s_old + lax.dot_general((k * tail_ref[h]).astype(BF16), v, TN_DIMS,
                                                                   preferred_element_type=F32)
            mix_ref[b, :, vs] = _group_norm_gate(o[:seq], g_ref[b, :, vs].astype(F32), gn_ref[:, vs]).astype(BF16)


def _ret_short(z, batch, seq, state0, gn, pos, nb=4):
    z3 = z.reshape(batch, seq, z.shape[1])
    cos, sin, decay, inner, tail, full = _ret_tables(pos, seq)
    qw = RET_HEADS * RET_DK
    vw = RET_HEADS * RET_DV
    small = lambda a: _resident(a.shape)
    rp = SHORT_ROWS - seq
    decay = jnp.pad(decay, ((0, 0), (0, rp), (0, rp)))
    inner3 = jnp.pad(inner, ((0, 0), (0, rp)))[:, :, None]
    tail3 = jnp.pad(tail, ((0, 0), (0, rp)))[:, :, None]
    mix, state = pl.pallas_call(
        functools.partial(_ret_short_kernel, nb=nb, seq=seq),
        grid=(batch // nb,),
        in_specs=[
            pl.BlockSpec(memory_space=pltpu.SMEM),
            pl.BlockSpec((nb, seq, qw), lambda i: (i, 0, 0)),
            pl.BlockSpec((nb, seq, qw), lambda i: (i, 0, 1)),
            pl.BlockSpec((nb, seq, vw), lambda i: (i, 0, 1)),
            pl.BlockSpec((nb, seq, vw), lambda i: (i, 0, 2)),
            small(cos), small(sin), small(decay), small(inner3), small(tail3),
            _resident((1, vw)),
            pl.BlockSpec((nb, RET_HEADS, RET_DK, RET_DV), lambda i: (i, 0, 0, 0)),
        ],
        out_specs=[
            pl.BlockSpec((nb, seq, vw), lambda i: (i, 0, 0)),
            pl.BlockSpec((nb, RET_HEADS, RET_DK, RET_DV), lambda i: (i, 0, 0, 0)),
        ],
        out_shape=[jax.ShapeDtypeStruct((batch, seq, vw), BF16),
                   jax.ShapeDtypeStruct((batch, RET_HEADS, RET_DK, RET_DV), F32)],
        compiler_params=_cparams(("arbitrary",)),
        name="ret_short",
    )(full, z3, z3, z3, z3, cos, sin, decay, inner3, tail3, gn.reshape(1, -1), state0)
    return mix.reshape(batch * seq, vw), state


def _route(logits_t):
    n = logits_t.shape[1]
    epg = EXPERTS_PER_GROUP
    row = lax.broadcasted_iota(I32, (epg, n), 0).astype(F32)
    neg = jnp.float32(-1e30)
    big = jnp.float32(epg)
    is_g = row < N_GROUPS
    lg = jnp.where(is_g, logits_t[0:epg], neg)
    gmax = jnp.max(lg, axis=0, keepdims=True)
    gidx = jnp.min(jnp.where(lg == gmax, row, big), axis=0, keepdims=True)
    pg_sel = 1.0 / jnp.sum(jnp.where(is_g, jnp.exp(lg - gmax), 0.0), axis=0, keepdims=True)
    sel = logits_t[epg:2 * epg]
    for g in range(1, N_GROUPS):
        sel = jnp.where(gidx == g, logits_t[epg * (g + 1):epg * (g + 2)], sel)
    e = jnp.exp(sel - jnp.max(sel, axis=0, keepdims=True))
    pe = e / jnp.sum(e, axis=0, keepdims=True)
    t1 = jnp.max(pe, axis=0, keepdims=True)
    i1 = jnp.min(jnp.where(pe == t1, row, big), axis=0, keepdims=True)
    pe2 = jnp.where(row == i1, -1.0, pe)
    t2 = jnp.max(pe2, axis=0, keepdims=True)
    i2 = jnp.min(jnp.where(pe2 == t2, row, big), axis=0, keepdims=True)
    den = t1 + t2
    gates = jnp.where(row == i1, t1 / den * pg_sel, 0.0) + jnp.where(row == i2, t2 / den * pg_sel, 0.0)
    ids = jnp.where(row == 0, gidx, 0.0)
    block = jnp.concatenate([gates, ids, jnp.zeros((LANES - 2 * epg, n), F32)], axis=0)
    return block.T


def _outproj_pair_kernel(xl_ref, mixl_ref, xal_ref, xs_ref, mixs_ref, xas_ref, wa_ref, wb_ref, g_ref, wr_ref,
                         o_ref, r_ref, *, sub, n_long):
    i = pl.program_id(0)

    @pl.when(i < n_long)
    def _():
        _outproj_kernel(xl_ref, mixl_ref, xal_ref, wa_ref, wb_ref, g_ref, wr_ref, o_ref, r_ref, sub=sub)

    @pl.when(i >= n_long)
    def _():
        _outproj_kernel(xs_ref, mixs_ref, xas_ref, wa_ref, wb_ref, g_ref, wr_ref, o_ref, r_ref, sub=sub)


def _outproj_kernel(x_ref, mix_ref, xa_ref, wa_ref, wb_ref, g_ref, wr_ref, o_ref, r_ref, *, sub):
    acc = jnp.dot(mix_ref[...], wa_ref[...], preferred_element_type=F32)
    acc = acc + jnp.dot(xa_ref[...], wb_ref[...], preferred_element_type=F32)
    o_ref[:, :D_MODEL] = x_ref[...] + acc
    logits = []
    for s in range(x_ref.shape[0] // sub):
        rows = slice(s * sub, (s + 1) * sub)
        h2 = _rms(o_ref[rows, :D_MODEL], g_ref[...])
        hi = h2.astype(BF16)
        lo = (h2 - hi.astype(F32)).astype(BF16)
        p_hi = lax.dot_general(wr_ref[...], hi, NT_DIMS, preferred_element_type=F32)
        p_lo = lax.dot_general(wr_ref[0:ROUTER_ROWS, :], lo, NT_DIMS, preferred_element_type=F32)
        logits.append((p_hi[:ROUTER_ROWS] + p_lo) + p_hi[ROUTER_ROWS:])
    route = _route(jnp.concatenate(logits, axis=1))
    o_ref[:, D_MODEL:] = route
    r_ref[...] = route


def _router_weights(w_group, w_router):
    epg = EXPERTS_PER_GROUP
    zeros = lambda r: jnp.zeros((DEPTH, r, D_MODEL), F32)
    wt = jnp.concatenate([jnp.swapaxes(w_group, 1, 2), zeros(epg - N_GROUPS), jnp.swapaxes(w_router, 1, 2),
                          zeros(ROUTER_ROWS - epg - N_EXPERTS)], axis=1)
    hi = wt.astype(BF16)
    return jnp.concatenate([hi, (wt - hi.astype(F32)).astype(BF16)], axis=1)


def _outproj_router(x, x_off, mix, xa, w_out, g_ffn, w_router2):
    m, cm = mix.shape
    tm = min(512, m)
    xb = x_off // tm
    row = lambda w, off=0: pl.BlockSpec((tm, w), lambda i: (i + off, 0))
    return pl.pallas_call(
        functools.partial(_outproj_kernel, sub=min(128, tm)),
        grid=(m // tm,),
        in_specs=[row(D_MODEL, xb), row(cm), row(D_MODEL), _resident((cm, D_MODEL)), _resident((D_MODEL, D_MODEL)),
                  _resident((1, D_MODEL)), _resident((2 * ROUTER_ROWS, D_MODEL))],
        out_specs=[row(AUG_W), row(LANES)],
        out_shape=[jax.ShapeDtypeStruct((m, AUG_W), F32), jax.ShapeDtypeStruct((m, LANES), F32)],
        compiler_params=_cparams(("arbitrary",)),
        name="outproj_router",
    )(x, mix, xa, w_out[:cm], w_out[cm:], g_ffn.reshape(1, D_MODEL), w_router2)


def _outproj_router_pair(long, short, w_out, g_ffn, w_router2):
    tm = 512
    cm = long[1].shape[1]
    nl, ns = long[0].shape[0] // tm, short[0].shape[0] // tm
    first = lambda w: pl.BlockSpec((tm, w), lambda i: (jnp.minimum(i, nl - 1), 0))
    second = lambda w: pl.BlockSpec((tm, w), lambda i: (jnp.maximum(i - nl, 0), 0))
    row = lambda w: pl.BlockSpec((tm, w), lambda i: (i, 0))
    m = (nl + ns) * tm
    return pl.pallas_call(
        functools.partial(_outproj_pair_kernel, sub=128, n_long=nl),
        grid=(nl + ns,),
        in_specs=[first(D_MODEL), first(cm), first(D_MODEL), second(D_MODEL), second(cm), second(D_MODEL),
                  _resident((cm, D_MODEL)), _resident((D_MODEL, D_MODEL)),
                  _resident((1, D_MODEL)), _resident((2 * ROUTER_ROWS, D_MODEL))],
        out_specs=[row(AUG_W), row(LANES)],
        out_shape=[jax.ShapeDtypeStruct((m, AUG_W), F32), jax.ShapeDtypeStruct((m, LANES), F32)],
        compiler_params=_cparams(("arbitrary",)),
        name="outproj_router_pair",
    )(*long, *short, w_out[:cm], w_out[cm:], g_ffn.reshape(1, D_MODEL), w_router2)


STEP_VALID, STEP_FIRST, STEP_LAST, STEP_NEWGROUP = 1, 2, 4, 8


def _moe_plan(gid, tm):
    m = gid.shape[0]
    ntile = m // tm
    nstep = ntile + N_GROUPS - 1
    counts = jnp.sum((gid[:, None] == jnp.arange(N_GROUPS, dtype=I32)[None, :]).astype(I32), axis=0)
    gend = jnp.cumsum(counts)
    gstart = gend - counts
    row_tok = jnp.argsort(gid, stable=True).astype(I32)
    inner_ok = (counts[1:] > 0) & (gstart[1:] % tm != 0)
    cuts = jnp.concatenate([jnp.arange(ntile, dtype=I32) * tm, jnp.where(inner_ok, gstart[1:], m)])
    cut_ok = jnp.concatenate([jnp.ones((ntile,), bool), inner_ok])
    rank = jnp.sum(((cuts[None, :] < cuts[:, None]) & cut_ok[None, :]).astype(I32), axis=1)
    sel = jnp.where(cut_ok, rank, nstep)[None, :] == jnp.arange(nstep, dtype=I32)[:, None]
    valid = jnp.any(sel, axis=1)
    start = jnp.sum(jnp.where(sel, cuts[None, :], 0), axis=1)
    start = jnp.where(valid, start, jnp.max(start))
    st_tile = start // tm
    st_gid = jnp.sum((gend[None, :] <= start[:, None]).astype(I32), axis=1)
    first = valid & (start % tm == 0)
    next_first = jnp.concatenate([first[1:], jnp.ones((1,), bool)])
    next_valid = jnp.concatenate([valid[1:], jnp.zeros((1,), bool)])
    last = valid & (next_first | ~next_valid)
    prev_gid = jnp.concatenate([jnp.full((1,), -1, I32), st_gid[:-1]])
    newgroup = valid & (st_gid != prev_gid)
    flags = valid * STEP_VALID + first * STEP_FIRST + last * STEP_LAST + newgroup * STEP_NEWGROUP
    return st_tile, st_gid, flags.astype(I32), row_tok


def _moe_kernel(tile_ref, gid_ref, flag_ref, tok_ref, xaug_hbm, g_ref, fin_ref, wgu_hbm, wd_hbm, out_hbm,
                xbuf, ybuf, wgu_bf, wd_bf, wgu_stage, wd_stage, gsem, ssem, wsem, *, tm, ntile, layer, final_norm):
    s = pl.program_id(0)
    tile = tile_ref[s]
    grp = gid_ref[s]
    flags = flag_ref[s]
    valid = (flags & STEP_VALID) > 0
    first = (flags & STEP_FIRST) > 0
    last = (flags & STEP_LAST) > 0
    newgroup = (flags & STEP_NEWGROUP) > 0
    slot = tile % 2

    def weight_copies(j):
        e = grp * EXPERTS_PER_GROUP + j
        return (pltpu.make_async_copy(wgu_hbm.at[layer, e], wgu_stage.at[j % 2], wsem.at[j % 2]),
                pltpu.make_async_copy(wd_hbm.at[layer, e], wd_stage.at[j % 2], wsem.at[j % 2]))

    def gather_start(t, sl):
        for r in range(tm):
            tok = tok_ref[t * tm + r]
            pltpu.make_async_copy(xaug_hbm.at[pl.ds(tok, 1)], xbuf.at[sl, pl.ds(r, 1)],
                                  gsem.at[sl]).start(priority=r % 2)

    def scatter_start(t, sl):
        for r in range(tm):
            tok = tok_ref[t * tm + r]
            pltpu.make_async_copy(ybuf.at[sl, pl.ds(r, 1)], out_hbm.at[pl.ds(tok, 1)],
                                  ssem.at[sl]).start(priority=r % 2)

    def gather_wait(sl):
        pltpu.make_async_copy(xaug_hbm.at[pl.ds(0, tm)], xbuf.at[sl], gsem.at[sl]).wait()

    def scatter_wait(sl):
        pltpu.make_async_copy(ybuf.at[sl], out_hbm.at[pl.ds(0, tm)], ssem.at[sl]).wait()

    @pl.when(s == 0)
    def _():
        gather_start(0, 0)

    @pl.when(first & (tile + 1 < ntile))
    def _():
        gather_start(tile + 1, 1 - slot)

    @pl.when(first)
    def _():
        gather_wait(slot)

        @pl.when(tile >= 2)
        def _():
            scatter_wait(slot)

        ybuf[slot] = xbuf[slot, :, 0:D_MODEL]

    def experts(stream_weights):
        xn = xbuf[slot, :, 0:D_MODEL]
        gl = xbuf[slot, :, GATE_LANE0:AUG_W]
        mine = gl[:, GID_LANE:GID_LANE + 1] == grp.astype(F32)
        h2 = _rms(xn, g_ref[...]).astype(BF16)
        acc = jnp.zeros((tm, D_MODEL), F32)
        if stream_weights:
            for c in weight_copies(0):
                c.start()
        for j in range(EXPERTS_PER_GROUP):
            if stream_weights:
                if j + 1 < EXPERTS_PER_GROUP:
                    for c in weight_copies(j + 1):
                        c.start()
                for c in weight_copies(j):
                    c.wait()
                wgu_bf[j] = wgu_stage[j % 2].astype(BF16)
                wd_bf[j] = wd_stage[j % 2].astype(BF16)
            gu = jnp.dot(h2, wgu_bf[j], preferred_element_type=F32)
            gate = jnp.where(mine, gl[:, j:j + 1], 0.0)
            act = _silu(gu[:, :EXPERT_FF]) * gu[:, EXPERT_FF:] * gate
            acc = acc + jnp.dot(act.astype(BF16), wd_bf[j], preferred_element_type=F32)
        ybuf[slot] = ybuf[slot] + acc

    @pl.when(newgroup)
    def _():
        experts(True)

    @pl.when(valid & jnp.logical_not(newgroup))
    def _():
        experts(False)

    @pl.when(last)
    def _():
        if final_norm:
            ybuf[slot] = _rms(ybuf[slot], fin_ref[...])
        scatter_start(tile, slot)

        @pl.when(tile == ntile - 1)
        def _():
            if ntile >= 2:
                scatter_wait(1 - slot)
            scatter_wait(slot)


def _moe(xaug, route, g_ffn, final_g, w_gu, w_down, layer, final_norm):
    m = xaug.shape[0]
    tm = MOE_TM if m >= N_GROUPS * MOE_TM else min(MOE_TM // N_GROUPS, m)
    gid = route[:, GID_LANE].astype(I32)
    st_tile, st_gid, flags, row_tok = _moe_plan(gid, tm)
    epg = EXPERTS_PER_GROUP
    grid_spec = pltpu.PrefetchScalarGridSpec(
        num_scalar_prefetch=4,
        grid=(st_tile.shape[0],),
        in_specs=[
            pl.BlockSpec(memory_space=pl.ANY),
            pl.BlockSpec((1, D_MODEL), lambda s, *_: (0, 0)),
            pl.BlockSpec((1, D_MODEL), lambda s, *_: (0, 0)),
            pl.BlockSpec(memory_space=pl.ANY),
            pl.BlockSpec(memory_space=pl.ANY),
        ],
        out_specs=pl.BlockSpec(memory_space=pl.ANY),
        scratch_shapes=[
            pltpu.VMEM((2, tm, AUG_W), F32),
            pltpu.VMEM((2, tm, D_MODEL), F32),
            pltpu.VMEM((epg, D_MODEL, 2 * EXPERT_FF), BF16),
            pltpu.VMEM((epg, EXPERT_FF, D_MODEL), BF16),
            pltpu.VMEM((2, D_MODEL, 2 * EXPERT_FF), F32),
            pltpu.VMEM((2, EXPERT_FF, D_MODEL), F32),
            pltpu.SemaphoreType.DMA((2,)),
            pltpu.SemaphoreType.DMA((2,)),
            pltpu.SemaphoreType.DMA((2,)),
        ],
    )
    return pl.pallas_call(
        functools.partial(_moe_kernel, tm=tm, ntile=m // tm, layer=layer, final_norm=final_norm),
        grid_spec=grid_spec,
        out_shape=jax.ShapeDtypeStruct((m, D_MODEL), F32),
        compiler_params=_cparams(("arbitrary",)),
        name="moe",
    )(st_tile, st_gid, flags, row_tok, xaug, g_ffn.reshape(1, D_MODEL), final_g.reshape(1, D_MODEL), w_gu, w_down)


SHORT_NB = 4


def _forward(x_long, x_short, pos_long, pos_short, conv_state, ret_state, mem_long, mem_short, wts):
    assert DEPTH == 2
    (bl, tl), (bs, ts) = x_long.shape[:2], x_short.shape[:2]
    ml, ms = bl * tl, bs * ts
    xl, xs = x_long.reshape(ml, D_MODEL), x_short.reshape(ms, D_MODEL)
    tq = min(1024, tl)
    moe = lambda rows, route, i: _moe(rows, route, wts["norm_ffn"][i], wts["final_norm"], wts["moe_w_gate_up"],
                                      wts["moe_w_down"], i, final_norm=(i == DEPTH - 1))

    mix_l, q_l, conv_l = _inproj_conv(xl, wts["norm_mix"][0], wts["conv_w_in"], wts["conv_w"], bl, tl)
    zs = _inproj(xs, wts["norm_mix"][0], wts["conv_w_in"])
    mix_s, conv_s = _conv_short(zs, wts["conv_w"], conv_state, bs, ts)
    xa_l = _xattn(q_l, bl, tl, 0, *mem_long, 0, 1, tq)
    xa_s = _xattn(zs, bs, ts, 3, *mem_short, 0, SHORT_NB, ts)
    routed = _outproj_router_pair((xl, mix_l, xa_l), (xs, mix_s, xa_s), wts["conv_w_out"], wts["norm_ffn"][0],
                                  wts["w_router"][0])
    x1 = moe(*routed, 0)

    z = _inproj(x1, wts["norm_mix"][1], wts["ret_w_in"])
    zs = z[ml:]
    mix_l, ret_l = _ret_long(z, bl, tl, wts["ret_gn"], pos_long)
    mix_s, ret_s = _ret_short(zs, bs, ts, ret_state, wts["ret_gn"], pos_short)
    xa_l = _xattn(z, bl, tl, 6, *mem_long, 1, 1, tq)
    xa_s = _xattn(zs, bs, ts, 6, *mem_short, 1, SHORT_NB, ts)
    y_l = moe(*_outproj_router(x1, 0, mix_l, xa_l, wts["ret_w_out"], wts["norm_ffn"][1], wts["w_router"][1]), 1)
    y_s = moe(*_outproj_router(x1, ml, mix_s, xa_s, wts["ret_w_out"], wts["norm_ffn"][1], wts["w_router"][1]), 1)
    return (y_l.reshape(bl, tl, D_MODEL), y_s.reshape(bs, ts, D_MODEL), conv_l, conv_s, ret_l, ret_s)


def kernel(x_prompt, x_sample, state_conv, state_ret, cache_mem_k, cache_mem_v, mem_prompt, norm_mix, norm_ffn,
           mem_norm, w_mem_kv, conv_w_in, conv_w, conv_w_out, ret_w_in, ret_gn, ret_w_out, moe_w_group,
           moe_w_router, moe_w_gate_up, moe_w_down, final_norm):
    batch, seq = x_prompt.shape[0], x_prompt.shape[1]
    dec_batch, dec_seq = x_sample.shape[0], x_sample.shape[1]
    w_router = _router_weights(moe_w_group, moe_w_router)
    wts = dict(
        norm_mix=norm_mix, norm_ffn=norm_ffn, final_norm=final_norm, w_router=w_router,
        conv_w_in=conv_w_in[0].astype(BF16), conv_w=conv_w[0], conv_w_out=conv_w_out[0].astype(BF16),
        ret_w_in=ret_w_in[0].astype(BF16), ret_gn=ret_gn[0], ret_w_out=ret_w_out[0].astype(BF16),
        moe_w_gate_up=moe_w_gate_up, moe_w_down=moe_w_down,
    )
    mem_k_prompt, mem_v_prompt = _mem_kv(mem_prompt.reshape(batch * N_MEM, D_MODEL), mem_norm, w_mem_kv.astype(BF16))

    y_prompt, y_sample, conv_p, conv_s, ret_p, ret_s = _forward(
        x_prompt, x_sample, jnp.arange(seq, dtype=I32), PAST_LEN + jnp.arange(dec_seq, dtype=I32),
        state_conv[0], state_ret[0], (mem_k_prompt, mem_v_prompt), (cache_mem_k, cache_mem_v), wts)
    return (y_prompt, y_sample, conv_p[None], conv_s[None], ret_p[None], ret_s[None], mem_k_prompt, mem_v_prompt)
```

```python
import functools

import jax
import jax.numpy as jnp
import numpy as np
from jax import lax
from jax.experimental import pallas as pl
from jax.experimental.pallas import tpu as pltpu

F32 = jnp.float32
BF16 = jnp.bfloat16
I32 = jnp.int32

D_MODEL = 1024
DEPTH = 2
CONV_WIDTH = 3
RET_HEADS = 4
RET_DK = 256
RET_DV = 512
RET_CHUNK = 128
ROPE_BASE = 10000.0
N_MEM = 256
XA_HEADS = 4
XA_DIM = 256
N_GROUPS = 4
EXPERTS_PER_GROUP = 8
N_EXPERTS = N_GROUPS * EXPERTS_PER_GROUP
EXPERT_FF = 256
EPS = 1e-6

LANES = 128
AUG_W = D_MODEL + LANES
GATE_LANE0 = D_MODEL
GID_LANE = EXPERTS_PER_GROUP
ROUTER_ROWS = 64
VMEM_LIMIT = 56 * 1024 * 1024
MOE_TM = 512
SHORT_ROWS = 8
PAST_LEN = 16384
NT_DIMS = (((1,), (1,)), ((), ()))
TN_DIMS = (((0,), (0,)), ((), ()))


def _cparams(sem):
    return pltpu.CompilerParams(dimension_semantics=sem, vmem_limit_bytes=VMEM_LIMIT)


def _rms(x, g):
    return x * lax.rsqrt(jnp.mean(x * x, axis=-1, keepdims=True) + EPS) * g


def _silu(x):
    return x * jax.nn.sigmoid(x)


def _resident(shape):
    nd = len(shape)
    return pl.BlockSpec(shape, lambda *_: (0,) * nd, pipeline_mode=pl.Buffered(1))


def _memkv_kernel(x_ref, g_ref, w_ref, k_ref, v_ref, *, nb):
    h = _rms(x_ref[...], g_ref[...]).astype(BF16)
    kv = jnp.dot(h, w_ref[...], preferred_element_type=F32)
    for b in range(nb):
        rows = slice(b * N_MEM, (b + 1) * N_MEM)
        for hd in range(XA_HEADS):
            k_ref[b, :, hd, :] = kv[rows, hd * XA_DIM:(hd + 1) * XA_DIM]
            v_ref[b, :, hd, :] = kv[rows, D_MODEL + hd * XA_DIM:D_MODEL + (hd + 1) * XA_DIM]


def _mem_kv(mem2d, mem_norm, w_kv):
    batch = mem2d.shape[0] // N_MEM
    nb = min(2, batch)
    tm = nb * N_MEM
    out = jax.ShapeDtypeStruct((DEPTH, batch, N_MEM, XA_HEADS, XA_DIM), F32)
    out_spec = pl.BlockSpec((None, nb, N_MEM, XA_HEADS, XA_DIM), lambda l, j: (l, j, 0, 0, 0))
    return pl.pallas_call(
        functools.partial(_memkv_kernel, nb=nb),
        grid=(DEPTH, batch // nb),
        in_specs=[
            pl.BlockSpec((tm, D_MODEL), lambda l, j: (j, 0)),
            pl.BlockSpec((None, 1, D_MODEL), lambda l, j: (l, 0, 0)),
            pl.BlockSpec((None, D_MODEL, 2 * D_MODEL), lambda l, j: (l, 0, 0)),
        ],
        out_specs=[out_spec, out_spec],
        out_shape=[out, out],
        compiler_params=_cparams(("arbitrary", "arbitrary")),
        name="mem_kv",
    )(mem2d, mem_norm.reshape(DEPTH, 1, D_MODEL), w_kv)


def _inproj_kernel(x_ref, g_ref, w_ref, o_ref, *, tn):
    h = _rms(x_ref[...], g_ref[...]).astype(BF16)
    for j in range(o_ref.shape[1] // tn):
        sl = slice(j * tn, (j + 1) * tn)
        o_ref[:, sl] = jnp.dot(h, w_ref[:, sl], preferred_element_type=F32).astype(BF16)


def _inproj(x, g, w):
    m, n = x.shape[0], w.shape[1]
    tm = min(512, m)
    return pl.pallas_call(
        functools.partial(_inproj_kernel, tn=512),
        grid=(m // tm,),
        in_specs=[
            pl.BlockSpec((tm, D_MODEL), lambda i: (i, 0)),
            _resident((1, D_MODEL)),
            _resident((D_MODEL, n)),
        ],
        out_specs=pl.BlockSpec((tm, n), lambda i: (i, 0)),
        out_shape=jax.ShapeDtypeStruct((m, n), BF16),
        compiler_params=_cparams(("arbitrary",)),
        name="inproj",
    )(x, g.reshape(1, D_MODEL), w)


def _conv_taps(u, prev1, prev2, w_ref):
    return (w_ref[0:1, :] * prev2 + w_ref[1:2, :] * prev1) + w_ref[2:3, :] * u


def _inproj_conv_kernel(x_ref, g_ref, w_ref, cw_ref, mix_ref, q_ref, st_ref, carry, *, tiles_per_seq):
    i = pl.program_id(0)
    tt = x_ref.shape[0]
    col = lambda j: slice(j * D_MODEL, (j + 1) * D_MODEL)
    h = _rms(x_ref[...], g_ref[...]).astype(BF16)
    proj = lambda j: jnp.dot(h, w_ref[:, col(j)], preferred_element_type=F32)

    @pl.when(i % tiles_per_seq == 0)
    def _():
        carry[...] = jnp.zeros_like(carry)

    q_ref[...] = proj(3).astype(BF16)
    u = proj(1) * proj(2)
    pm2 = carry[6:7, :]
    pm1 = carry[7:8, :]
    row = lax.broadcasted_iota(I32, u.shape, 0)
    prev1 = jnp.where(row == 0, pm1, pltpu.roll(u, 1, 0))
    prev2 = jnp.where(row == 0, pm2, jnp.where(row == 1, pm1, pltpu.roll(u, 2, 0)))
    mix_ref[...] = (proj(0) * _conv_taps(u, prev1, prev2, cw_ref)).astype(BF16)
    carry[...] = u[tt - 8:tt, :]
    st_ref[0] = carry[6:8, :]


def _inproj_conv(x, g, w, conv_w, batch, seq):
    tt = min(512, seq)
    tps = seq // tt
    row = pl.BlockSpec((tt, D_MODEL), lambda i: (i, 0))
    out = jax.ShapeDtypeStruct((batch * seq, D_MODEL), BF16)
    return pl.pallas_call(
        functools.partial(_inproj_conv_kernel, tiles_per_seq=tps),
        grid=(batch * tps,),
        in_specs=[row, _resident((1, D_MODEL)), _resident(w.shape), _resident((CONV_WIDTH, D_MODEL))],
        out_specs=[row, row, pl.BlockSpec((1, 2, D_MODEL), lambda i: (i // tps, 0, 0))],
        out_shape=[out, out, jax.ShapeDtypeStruct((batch, 2, D_MODEL), F32)],
        scratch_shapes=[pltpu.VMEM((8, D_MODEL), F32)],
        compiler_params=_cparams(("arbitrary",)),
        name="inproj_conv",
    )(x, g.reshape(1, D_MODEL), w, conv_w)


def _conv_short_kernel(b_ref, c_ref, v_ref, w_ref, p1_ref, p2_ref, mix_ref, u_ref, *, seq):
    u = c_ref[...].astype(F32) * v_ref[...].astype(F32)
    t = lax.broadcasted_iota(I32, u.shape, 0) % seq
    prev1 = jnp.where(t == 0, p1_ref[...], pltpu.roll(u, 1, 0))
    prev2 = jnp.where(t <= 1, p2_ref[...], pltpu.roll(u, 2, 0))
    mix_ref[...] = (b_ref[...].astype(F32) * _conv_taps(u, prev1, prev2, w_ref)).astype(BF16)
    u_ref[...] = u


def _conv_short(z, conv_w, state0, batch, seq):
    m = batch * seq
    zeros = jnp.zeros((batch, seq - 1, D_MODEL), F32)
    p1 = jnp.concatenate([state0[:, 1:2], zeros], axis=1).reshape(m, D_MODEL)
    p2 = jnp.concatenate([state0, zeros[:, 1:]], axis=1).reshape(m, D_MODEL)
    col = lambda j: pl.BlockSpec((m, D_MODEL), lambda i, j=j: (0, j))
    full = pl.BlockSpec((m, D_MODEL), lambda i: (0, 0))
    mix, u = pl.pallas_call(
        functools.partial(_conv_short_kernel, seq=seq),
        grid=(1,),
        in_specs=[col(0), col(1), col(2), _resident((CONV_WIDTH, D_MODEL)), full, full],
        out_specs=[full, full],
        out_shape=[jax.ShapeDtypeStruct((m, D_MODEL), BF16), jax.ShapeDtypeStruct((m, D_MODEL), F32)],
        compiler_params=_cparams(("arbitrary",)),
        name="conv_short",
    )(z, z, z, conv_w, p1, p2)
    return mix, u.reshape(batch, seq, D_MODEL)[:, seq - 2:]


def _xattn_kernel(q_ref, k_hbm, v_hbm, o_ref, kbuf, vbuf, sem, *, nb, layer, n_bt):
    i = pl.program_id(0)
    t = pl.program_id(1)
    slot = i % 2

    def copies(bt, sl):
        out = []
        for b in range(nb):
            for h in range(XA_HEADS):
                out.append(pltpu.make_async_copy(k_hbm.at[layer, bt * nb + b, :, h, :], kbuf.at[sl, b, h], sem.at[0, sl]))
                out.append(pltpu.make_async_copy(v_hbm.at[layer, bt * nb + b, :, h, :], vbuf.at[sl, b, h], sem.at[1, sl]))
        return out

    @pl.when((i == 0) & (t == 0))
    def _():
        for c in copies(0, 0):
            c.start()

    @pl.when(t == 0)
    def _():
        @pl.when(i + 1 < n_bt)
        def _():
            for c in copies(i + 1, 1 - slot):
                c.start()

        for c in copies(i, slot):
            c.wait()

    scale = XA_DIM ** -0.5
    tq = q_ref.shape[1]
    rows = -(-tq // SHORT_ROWS) * SHORT_ROWS
    pairs = [(b, h) for b in range(nb) for h in range(XA_HEADS)]
    scores = []
    for b, h in pairs:
        q = q_ref[b, :, h * XA_DIM:(h + 1) * XA_DIM]
        if rows != tq:
            q = jnp.concatenate([q.astype(F32), jnp.zeros((rows - tq, XA_DIM), F32)], axis=0).astype(BF16)
        k = kbuf[slot, b, h].astype(BF16)
        scores.append(lax.dot_general(q, k, NT_DIMS, preferred_element_type=F32) * scale)
    s = jnp.concatenate(scores, axis=0)
    p = jnp.exp(s - jnp.max(s, axis=-1, keepdims=True))
    p = (p / jnp.sum(p, axis=-1, keepdims=True)).astype(BF16)
    for n, (b, h) in enumerate(pairs):
        v = vbuf[slot, b, h].astype(BF16)
        o = jnp.dot(p[n * rows:(n + 1) * rows], v, preferred_element_type=F32)
        o_ref[b, :, h * XA_DIM:(h + 1) * XA_DIM] = o[:tq].astype(BF16)


def _xattn(z, batch, seq, qcol, mem_k, mem_v, layer, nb, tq):
    n_bt = batch // nb
    plane = pltpu.VMEM((2, nb, XA_HEADS, N_MEM, XA_DIM), F32)
    if nb == 1:
        q_arr = z.reshape(1, z.shape[0], z.shape[1])
        nt = seq // tq
        q_spec = pl.BlockSpec((1, tq, D_MODEL), lambda i, t: (0, i * nt + t, qcol))
        o_spec = pl.BlockSpec((1, tq, D_MODEL), lambda i, t: (0, i * nt + t, 0))
        o_shape = (1, batch * seq, D_MODEL)
    else:
        q_arr = z.reshape(batch, seq, z.shape[1])
        q_spec = pl.BlockSpec((nb, tq, D_MODEL), lambda i, t: (i, t, qcol))
        o_spec = pl.BlockSpec((nb, tq, D_MODEL), lambda i, t: (i, t, 0))
        o_shape = (batch, seq, D_MODEL)
    return pl.pallas_call(
        functools.partial(_xattn_kernel, nb=nb, layer=layer, n_bt=n_bt),
        grid=(n_bt, seq // tq),
        in_specs=[q_spec, pl.BlockSpec(memory_space=pl.ANY), pl.BlockSpec(memory_space=pl.ANY)],
        out_specs=o_spec,
        out_shape=jax.ShapeDtypeStruct(o_shape, BF16),
        scratch_shapes=[plane, plane, pltpu.SemaphoreType.DMA((2, 2))],
        compiler_params=_cparams(("arbitrary", "arbitrary")),
        name="xattn",
    )(q_arr, mem_k, mem_v).reshape(batch * seq, D_MODEL)


def _rotary(x, cos, sin):
    half = x.shape[-1] // 2
    x1, x2 = x[:, :half], x[:, half:]
    return jnp.concatenate([x1 * cos - x2 * sin, x1 * sin + x2 * cos], axis=-1)


def _group_norm_gate(o, gate, gn):
    mu = jnp.mean(o, axis=-1, keepdims=True)
    var = jnp.mean(jnp.square(o - mu), axis=-1, keepdims=True)
    return _silu(gate) * ((o - mu) * lax.rsqrt(var + EPS) * gn)


def _ret_long_kernel(full_ref, q_ref, k_ref, v_ref, g_ref, cos_ref, sin_ref, dec_ref, inn_ref, tail_ref,
                     gn_ref, mix_ref, sfin_ref, s_acc, *, chunk):
    t = pl.program_id(1)
    heads = range(RET_HEADS)

    @pl.when(t == 0)
    def _():
        s_acc[...] = jnp.zeros_like(s_acc)

    cos, sin = cos_ref[...], sin_ref[...]
    qs = [_rotary(q_ref[0, :, h * RET_DK:(h + 1) * RET_DK].astype(F32), cos, sin) for h in heads]
    ks = [_rotary(k_ref[0, :, h * RET_DK:(h + 1) * RET_DK].astype(F32), cos, sin) * (RET_DK ** -0.5) for h in heads]
    for c in range(q_ref.shape[1] // chunk):
        sl = slice(c * chunk, (c + 1) * chunk)
        qc = [qs[h][sl].astype(BF16) for h in heads]
        kc = [ks[h][sl] for h in heads]
        vc = [v_ref[0, sl, h * RET_DV:(h + 1) * RET_DV] for h in heads]
        s_old = [s_acc[h] for h in heads]
        scores = [lax.dot_general(qc[h], kc[h].astype(BF16), NT_DIMS, preferred_element_type=F32) * dec_ref[h]
                  for h in heads]
        cross = [jnp.dot(qc[h], s_old[h].astype(BF16), preferred_element_type=F32) * inn_ref[h][:, 0:1]
                 for h in heads]
        o = [jnp.dot(scores[h].astype(BF16), vc[h], preferred_element_type=F32) + cross[h] for h in heads]
        for h in heads:
            kt = (kc[h] * tail_ref[h][:, 0:1]).astype(BF16)
            s_acc[h] = full_ref[h] * s_old[h] + lax.dot_general(kt, vc[h], TN_DIMS, preferred_element_type=F32)
        og = jnp.concatenate(o, axis=0)
        mu = jnp.mean(og, axis=-1, keepdims=True)
        var = jnp.mean(jnp.square(og - mu), axis=-1, keepdims=True)
        on = (og - mu) * lax.rsqrt(var + EPS)
        for h in heads:
            vs = slice(h * RET_DV, (h + 1) * RET_DV)
            gate = g_ref[0, sl, vs]
            mix_ref[0, sl, vs] = _silu(gate) * (on[h * chunk:(h + 1) * chunk] * gn_ref[:, vs]).astype(BF16)

    @pl.when(t == pl.num_programs(1) - 1)
    def _():
        sfin_ref[0] = s_acc[...]


def _ret_tables(pos, chunk):
    f32 = np.float32
    half = RET_DK // 2
    inv = np.power(f32(ROPE_BASE), -np.arange(half, dtype=f32) / f32(half)).astype(f32)
    ang = (pos.astype(f32)[:, None] * inv[None, :]).astype(f32)
    log_g = np.log1p(-np.exp2(f32(-5.0) - np.arange(RET_HEADS, dtype=f32))).astype(f32)
    idx = np.arange(chunk)
    diff = idx[:, None] - idx[None, :]
    decay = np.where(diff[None] >= 0, np.exp(np.maximum(diff, 0)[None].astype(f32) * log_g[:, None, None]), 0.0)
    inner = np.exp((idx + 1).astype(f32)[None, :] * log_g[:, None])
    tail = np.exp((chunk - 1 - idx).astype(f32)[None, :] * log_g[:, None])
    full = np.exp(f32(chunk) * log_g)
    return tuple(a.astype(f32) for a in (np.cos(ang), np.sin(ang), decay, inner, tail, full))


def _ret_long(z, batch, seq, gn, pos):
    z3 = z.reshape(1, z.shape[0], z.shape[1])
    tc = min(512, seq)
    nt = seq // tc
    chunk = 2 * RET_CHUNK if tc % (2 * RET_CHUNK) == 0 else RET_CHUNK
    cos, sin, decay, inner, tail, full = _ret_tables(pos, chunk)
    lane_b = lambda a: np.ascontiguousarray(np.broadcast_to(a[:, :, None], (RET_HEADS, chunk, LANES)))
    qw = RET_HEADS * RET_DK
    vw = RET_HEADS * RET_DV
    state_spec = pl.BlockSpec((1, RET_HEADS, RET_DK, RET_DV), lambda b, t: (b, 0, 0, 0))
    mix, state = pl.pallas_call(
        functools.partial(_ret_long_kernel, chunk=chunk),
        grid=(batch, seq // tc),
        in_specs=[
            pl.BlockSpec(memory_space=pltpu.SMEM),
            pl.BlockSpec((1, tc, qw), lambda b, t: (0, b * nt + t, 0)),
            pl.BlockSpec((1, tc, qw), lambda b, t: (0, b * nt + t, 1)),
            pl.BlockSpec((1, tc, vw), lambda b, t: (0, b * nt + t, 1)),
            pl.BlockSpec((1, tc, vw), lambda b, t: (0, b * nt + t, 2)),
            pl.BlockSpec((tc, RET_DK // 2), lambda b, t: (t, 0)),
            pl.BlockSpec((tc, RET_DK // 2), lambda b, t: (t, 0)),
            _resident((RET_HEADS, chunk, chunk)), _resident((RET_HEADS, chunk, LANES)),
            _resident((RET_HEADS, chunk, LANES)), _resident((1, vw)),
        ],
        out_specs=[pl.BlockSpec((1, tc, vw), lambda b, t: (0, b * nt + t, 0)), state_spec],
        out_shape=[jax.ShapeDtypeStruct((1, batch * seq, vw), BF16),
                   jax.ShapeDtypeStruct((batch, RET_HEADS, RET_DK, RET_DV), F32)],
        scratch_shapes=[pltpu.VMEM((RET_HEADS, RET_DK, RET_DV), F32)],
        compiler_params=_cparams(("arbitrary", "arbitrary")),
        name="ret_long",
    )(full, z3, z3, z3, z3, cos, sin, decay, lane_b(inner), lane_b(tail), gn.reshape(1, -1))
    return mix.reshape(batch * seq, vw), state


def _ret_short_kernel(full_ref, q_ref, k_ref, v_ref, g_ref, cos_ref, sin_ref, dec_ref, inn_ref, tail_ref,
                      gn_ref, s0_ref, mix_ref, snew_ref, *, nb, seq):
    def pad_rows(a):
        return jnp.concatenate([a, jnp.zeros((SHORT_ROWS - seq, a.shape[1]), F32)], axis=0)

    cos, sin = cos_ref[...], sin_ref[...]
    for b in range(nb):
        for h in range(RET_HEADS):
            qs = slice(h * RET_DK, (h + 1) * RET_DK)
            vs = slice(h * RET_DV, (h + 1) * RET_DV)
            q = pad_rows(_rotary(q_ref[b, :, qs].astype(F32), cos, sin)).astype(BF16)
            k = pad_rows(_rotary(k_ref[b, :, qs].astype(F32), cos, sin) * (RET_DK ** -0.5))
            v = pad_rows(v_ref[b, :, vs].astype(F32)).astype(BF16)
            s_old = s0_ref[b, h]
            scores = lax.dot_general(q, k.astype(BF16), NT_DIMS, preferred_element_type=F32) * dec_ref[h]
            o = jnp.dot(scores.astype(BF16), v, preferred_element_type=F32)
            o = o + jnp.dot(q, s_old.astype(BF16), preferred_element_type=F32) * inn_ref[h]
            snew_ref[b, h] = full_ref[h] * s_old + lax.dot_general((k * tail_ref[h]).astype(BF16), v, TN_DIMS,
                                                                   preferred_element_type=F32)
            mix_ref[b, :, vs] = _group_norm_gate(o[:seq], g_ref[b, :, vs].astype(F32), gn_ref[:, vs]).astype(BF16)


def _ret_short(z, batch, seq, state0, gn, pos, nb=4):
    z3 = z.reshape(batch, seq, z.shape[1])
    cos, sin, decay, inner, tail, full = _ret_tables(pos, seq)
    qw = RET_HEADS * RET_DK
    vw = RET_HEADS * RET_DV
    small = lambda a: _resident(a.shape)
    rp = SHORT_ROWS - seq
    decay = np.pad(decay, ((0, 0), (0, rp), (0, rp)))
    inner3 = np.pad(inner, ((0, 0), (0, rp)))[:, :, None]
    tail3 = np.pad(tail, ((0, 0), (0, rp)))[:, :, None]
    mix, state = pl.pallas_call(
        functools.partial(_ret_short_kernel, nb=nb, seq=seq),
        grid=(batch // nb,),
        in_specs=[
            pl.BlockSpec(memory_space=pltpu.SMEM),
            pl.BlockSpec((nb, seq, qw), lambda i: (i, 0, 0)),
            pl.BlockSpec((nb, seq, qw), lambda i: (i, 0, 1)),
            pl.BlockSpec((nb, seq, vw), lambda i: (i, 0, 1)),
            pl.BlockSpec((nb, seq, vw), lambda i: (i, 0, 2)),
            small(cos), small(sin), small(decay), small(inner3), small(tail3),
            _resident((1, vw)),
            pl.BlockSpec((nb, RET_HEADS, RET_DK, RET_DV), lambda i: (i, 0, 0, 0)),
        ],
        out_specs=[
            pl.BlockSpec((nb, seq, vw), lambda i: (i, 0, 0)),
            pl.BlockSpec((nb, RET_HEADS, RET_DK, RET_DV), lambda i: (i, 0, 0, 0)),
        ],
        out_shape=[jax.ShapeDtypeStruct((batch, seq, vw), BF16),
                   jax.ShapeDtypeStruct((batch, RET_HEADS, RET_DK, RET_DV), F32)],
        compiler_params=_cparams(("arbitrary",)),
        name="ret_short",
    )(full, z3, z3, z3, z3, cos, sin, decay, inner3, tail3, gn.reshape(1, -1), state0)
    return mix.reshape(batch * seq, vw), state


def _route(logits_t):
    n = logits_t.shape[1]
    epg = EXPERTS_PER_GROUP
    row = lax.broadcasted_iota(I32, (epg, n), 0).astype(F32)
    neg = jnp.float32(-1e30)
    big = jnp.float32(epg)
    is_g = row < N_GROUPS
    lg = jnp.where(is_g, logits_t[0:epg], neg)
    gmax = jnp.max(lg, axis=0, keepdims=True)
    gidx = jnp.min(jnp.where(lg == gmax, row, big), axis=0, keepdims=True)
    pg_sel = 1.0 / jnp.sum(jnp.where(is_g, jnp.exp(lg - gmax), 0.0), axis=0, keepdims=True)
    sel = logits_t[epg:2 * epg]
    for g in range(1, N_GROUPS):
        sel = jnp.where(gidx == g, logits_t[epg * (g + 1):epg * (g + 2)], sel)
    e = jnp.exp(sel - jnp.max(sel, axis=0, keepdims=True))
    pe = e / jnp.sum(e, axis=0, keepdims=True)
    t1 = jnp.max(pe, axis=0, keepdims=True)
    i1 = jnp.min(jnp.where(pe == t1, row, big), axis=0, keepdims=True)
    pe2 = jnp.where(row == i1, -1.0, pe)
    t2 = jnp.max(pe2, axis=0, keepdims=True)
    i2 = jnp.min(jnp.where(pe2 == t2, row, big), axis=0, keepdims=True)
    den = t1 + t2
    gates = jnp.where(row == i1, t1 / den * pg_sel, 0.0) + jnp.where(row == i2, t2 / den * pg_sel, 0.0)
    ids = jnp.where(row == 0, gidx, 0.0)
    block = jnp.concatenate([gates, ids, jnp.zeros((LANES - 2 * epg, n), F32)], axis=0)
    return block.T


def _outproj_pair_kernel(xl_ref, mixl_ref, xal_ref, xs_ref, mixs_ref, xas_ref, wa_ref, wb_ref, g_ref, wr_ref,
                         o_ref, r_ref, *, sub, n_long):
    i = pl.program_id(0)

    @pl.when(i < n_long)
    def _():
        _outproj_kernel(xl_ref, mixl_ref, xal_ref, wa_ref, wb_ref, g_ref, wr_ref, o_ref, r_ref, sub=sub)

    @pl.when(i >= n_long)
    def _():
        _outproj_kernel(xs_ref, mixs_ref, xas_ref, wa_ref, wb_ref, g_ref, wr_ref, o_ref, r_ref, sub=sub)


def _outproj_kernel(x_ref, mix_ref, xa_ref, wa_ref, wb_ref, g_ref, wr_ref, o_ref, r_ref, *, sub):
    acc = jnp.dot(mix_ref[...], wa_ref[...], preferred_element_type=F32)
    acc = acc + jnp.dot(xa_ref[...], wb_ref[...], preferred_element_type=F32)
    o_ref[:, :D_MODEL] = x_ref[...] + acc
    logits = []
    for s in range(x_ref.shape[0] // sub):
        rows = slice(s * sub, (s + 1) * sub)
        h2 = _rms(o_ref[rows, :D_MODEL], g_ref[...])
        hi = h2.astype(BF16)
        lo = (h2 - hi.astype(F32)).astype(BF16)
        p_hi = lax.dot_general(wr_ref[...], hi, NT_DIMS, preferred_element_type=F32)
        p_lo = lax.dot_general(wr_ref[0:ROUTER_ROWS, :], lo, NT_DIMS, preferred_element_type=F32)
        logits.append((p_hi[:ROUTER_ROWS] + p_lo) + p_hi[ROUTER_ROWS:])
    route = _route(jnp.concatenate(logits, axis=1))
    o_ref[:, D_MODEL:] = route
    r_ref[...] = route


def _router_weights(w_group, w_router):
    epg = EXPERTS_PER_GROUP
    zeros = lambda r: jnp.zeros((DEPTH, r, D_MODEL), F32)
    wt = jnp.concatenate([jnp.swapaxes(w_group, 1, 2), zeros(epg - N_GROUPS), jnp.swapaxes(w_router, 1, 2),
                          zeros(ROUTER_ROWS - epg - N_EXPERTS)], axis=1)
    hi = wt.astype(BF16)
    return jnp.concatenate([hi, (wt - hi.astype(F32)).astype(BF16)], axis=1)


def _outproj_router(x, x_off, mix, xa, w_out, g_ffn, w_router2):
    m, cm = mix.shape
    tm = min(512, m)
    xb = x_off // tm
    row = lambda w, off=0: pl.BlockSpec((tm, w), lambda i: (i + off, 0))
    return pl.pallas_call(
        functools.partial(_outproj_kernel, sub=min(128, tm)),
        grid=(m // tm,),
        in_specs=[row(D_MODEL, xb), row(cm), row(D_MODEL), _resident((cm, D_MODEL)), _resident((D_MODEL, D_MODEL)),
                  _resident((1, D_MODEL)), _resident((2 * ROUTER_ROWS, D_MODEL))],
        out_specs=[row(AUG_W), row(LANES)],
        out_shape=[jax.ShapeDtypeStruct((m, AUG_W), F32), jax.ShapeDtypeStruct((m, LANES), F32)],
        compiler_params=_cparams(("arbitrary",)),
        name="outproj_router",
    )(x, mix, xa, w_out[:cm], w_out[cm:], g_ffn.reshape(1, D_MODEL), w_router2)


def _outproj_router_pair(long, short, w_out, g_ffn, w_router2):
    tm = 512
    cm = long[1].shape[1]
    nl, ns = long[0].shape[0] // tm, short[0].shape[0] // tm
    first = lambda w: pl.BlockSpec((tm, w), lambda i: (jnp.minimum(i, nl - 1), 0))
    second = lambda w: pl.BlockSpec((tm, w), lambda i: (jnp.maximum(i - nl, 0), 0))
    row = lambda w: pl.BlockSpec((tm, w), lambda i: (i, 0))
    m = (nl + ns) * tm
    return pl.pallas_call(
        functools.partial(_outproj_pair_kernel, sub=128, n_long=nl),
        grid=(nl + ns,),
        in_specs=[first(D_MODEL), first(cm), first(D_MODEL), second(D_MODEL), second(cm), second(D_MODEL),
                  _resident((cm, D_MODEL)), _resident((D_MODEL, D_MODEL)),
                  _resident((1, D_MODEL)), _resident((2 * ROUTER_ROWS, D_MODEL))],
        out_specs=[row(AUG_W), row(LANES)],
        out_shape=[jax.ShapeDtypeStruct((m, AUG_W), F32), jax.ShapeDtypeStruct((m, LANES), F32)],
        compiler_params=_cparams(("arbitrary",)),
        name="outproj_router_pair",
    )(*long, *short, w_out[:cm], w_out[cm:], g_ffn.reshape(1, D_MODEL), w_router2)


STEP_VALID, STEP_FIRST, STEP_LAST, STEP_NEWGROUP = 1, 2, 4, 8


def _moe_plan(gid, tm):
    m = gid.shape[0]
    ntile = m // tm
    nstep = ntile + N_GROUPS - 1
    counts = jnp.sum((gid[:, None] == jnp.arange(N_GROUPS, dtype=I32)[None, :]).astype(I32), axis=0)
    gend = jnp.cumsum(counts)
    gstart = gend - counts
    row_tok = jnp.argsort(gid, stable=True).astype(I32)
    inner_ok = (counts[1:] > 0) & (gstart[1:] % tm != 0)
    cuts = jnp.concatenate([jnp.arange(ntile, dtype=I32) * tm, jnp.where(inner_ok, gstart[1:], m)])
    cut_ok = jnp.concatenate([jnp.ones((ntile,), bool), inner_ok])
    rank = jnp.sum(((cuts[None, :] < cuts[:, None]) & cut_ok[None, :]).astype(I32), axis=1)
    sel = jnp.where(cut_ok, rank, nstep)[None, :] == jnp.arange(nstep, dtype=I32)[:, None]
    valid = jnp.any(sel, axis=1)
    start = jnp.sum(jnp.where(sel, cuts[None, :], 0), axis=1)
    start = jnp.where(valid, start, jnp.max(start))
    st_tile = start // tm
    st_gid = jnp.sum((gend[None, :] <= start[:, None]).astype(I32), axis=1)
    first = valid & (start % tm == 0)
    next_first = jnp.concatenate([first[1:], jnp.ones((1,), bool)])
    next_valid = jnp.concatenate([valid[1:], jnp.zeros((1,), bool)])
    last = valid & (next_first | ~next_valid)
    prev_gid = jnp.concatenate([jnp.full((1,), -1, I32), st_gid[:-1]])
    newgroup = valid & (st_gid != prev_gid)
    flags = valid * STEP_VALID + first * STEP_FIRST + last * STEP_LAST + newgroup * STEP_NEWGROUP
    return st_tile, st_gid, flags.astype(I32), row_tok


def _moe_kernel(tile_ref, gid_ref, flag_ref, tok_ref, xaug_hbm, g_ref, fin_ref, wgu_hbm, wd_hbm, out_hbm,
                xbuf, ybuf, wgu_bf, wd_bf, wgu_stage, wd_stage, gsem, ssem, wsem, *, tm, ntile, layer, final_norm):
    s = pl.program_id(0)
    tile = tile_ref[s]
    grp = gid_ref[s]
    flags = flag_ref[s]
    valid = (flags & STEP_VALID) > 0
    first = (flags & STEP_FIRST) > 0
    last = (flags & STEP_LAST) > 0
    newgroup = (flags & STEP_NEWGROUP) > 0
    slot = tile % 2

    def weight_copies(j):
        e = grp * EXPERTS_PER_GROUP + j
        return (pltpu.make_async_copy(wgu_hbm.at[layer, e], wgu_stage.at[j % 2], wsem.at[j % 2]),
                pltpu.make_async_copy(wd_hbm.at[layer, e], wd_stage.at[j % 2], wsem.at[j % 2]))

    def gather_start(t, sl):
        for r in range(tm):
            tok = tok_ref[t * tm + r]
            pltpu.make_async_copy(xaug_hbm.at[pl.ds(tok, 1)], xbuf.at[sl, pl.ds(r, 1)],
                                  gsem.at[sl]).start(priority=r % 2)

    def scatter_start(t, sl):
        for r in range(tm):
            tok = tok_ref[t * tm + r]
            pltpu.make_async_copy(ybuf.at[sl, pl.ds(r, 1)], out_hbm.at[pl.ds(tok, 1)],
                                  ssem.at[sl]).start(priority=r % 2)

    def gather_wait(sl):
        pltpu.make_async_copy(xaug_hbm.at[pl.ds(0, tm)], xbuf.at[sl], gsem.at[sl]).wait()

    def scatter_wait(sl):
        pltpu.make_async_copy(ybuf.at[sl], out_hbm.at[pl.ds(0, tm)], ssem.at[sl]).wait()

    @pl.when(s == 0)
    def _():
        gather_start(0, 0)

    @pl.when(first & (tile + 1 < ntile))
    def _():
        gather_start(tile + 1, 1 - slot)

    @pl.when(first)
    def _():
        gather_wait(slot)

        @pl.when(tile >= 2)
        def _():
            scatter_wait(slot)

        ybuf[slot] = xbuf[slot, :, 0:D_MODEL]

    def experts(stream_weights):
        xn = xbuf[slot, :, 0:D_MODEL]
        gl = xbuf[slot, :, GATE_LANE0:AUG_W]
        mine = gl[:, GID_LANE:GID_LANE + 1] == grp.astype(F32)
        h2 = _rms(xn, g_ref[...]).astype(BF16)
        acc = jnp.zeros((tm, D_MODEL), F32)
        if stream_weights:
            for c in weight_copies(0):
                c.start()
        for j in range(EXPERTS_PER_GROUP):
            if stream_weights:
                if j + 1 < EXPERTS_PER_GROUP:
                    for c in weight_copies(j + 1):
                        c.start()
                for c in weight_copies(j):
                    c.wait()
                wgu_bf[j] = wgu_stage[j % 2].astype(BF16)
                wd_bf[j] = wd_stage[j % 2].astype(BF16)
            gu = jnp.dot(h2, wgu_bf[j], preferred_element_type=F32)
            gate = jnp.where(mine, gl[:, j:j + 1], 0.0)
            act = _silu(gu[:, :EXPERT_FF]) * gu[:, EXPERT_FF:] * gate
            acc = acc + jnp.dot(act.astype(BF16), wd_bf[j], preferred_element_type=F32)
        ybuf[slot] = ybuf[slot] + acc

    @pl.when(newgroup)
    def _():
        experts(True)

    @pl.when(valid & jnp.logical_not(newgroup))
    def _():
        experts(False)

    @pl.when(last)
    def _():
        if final_norm:
            ybuf[slot] = _rms(ybuf[slot], fin_ref[...])
        scatter_start(tile, slot)

        @pl.when(tile == ntile - 1)
        def _():
            if ntile >= 2:
                scatter_wait(1 - slot)
            scatter_wait(slot)


def _moe(xaug, route, g_ffn, final_g, w_gu, w_down, layer, final_norm):
    m = xaug.shape[0]
    tm = min(MOE_TM, m)
    gid = route[:, GID_LANE].astype(I32)
    st_tile, st_gid, flags, row_tok = _moe_plan(gid, tm)
    epg = EXPERTS_PER_GROUP
    grid_spec = pltpu.PrefetchScalarGridSpec(
        num_scalar_prefetch=4,
        grid=(st_tile.shape[0],),
        in_specs=[
            pl.BlockSpec(memory_space=pl.ANY),
            pl.BlockSpec((1, D_MODEL), lambda s, *_: (0, 0)),
            pl.BlockSpec((1, D_MODEL), lambda s, *_: (0, 0)),
            pl.BlockSpec(memory_space=pl.ANY),
            pl.BlockSpec(memory_space=pl.ANY),
        ],
        out_specs=pl.BlockSpec(memory_space=pl.ANY),
        scratch_shapes=[
            pltpu.VMEM((2, tm, AUG_W), F32),
            pltpu.VMEM((2, tm, D_MODEL), F32),
            pltpu.VMEM((epg, D_MODEL, 2 * EXPERT_FF), BF16),
            pltpu.VMEM((epg, EXPERT_FF, D_MODEL), BF16),
            pltpu.VMEM((2, D_MODEL, 2 * EXPERT_FF), F32),
            pltpu.VMEM((2, EXPERT_FF, D_MODEL), F32),
            pltpu.SemaphoreType.DMA((2,)),
            pltpu.SemaphoreType.DMA((2,)),
            pltpu.SemaphoreType.DMA((2,)),
        ],
    )
    return pl.pallas_call(
        functools.partial(_moe_kernel, tm=tm, ntile=m // tm, layer=layer, final_norm=final_norm),
        grid_spec=grid_spec,
        out_shape=jax.ShapeDtypeStruct((m, D_MODEL), F32),
        compiler_params=_cparams(("arbitrary",)),
        name="moe",
    )(st_tile, st_gid, flags, row_tok, xaug, g_ffn.reshape(1, D_MODEL), final_g.reshape(1, D_MODEL), w_gu, w_down)


SHORT_NB = 4


def _forward(x_long, x_short, pos_long, pos_short, conv_state, ret_state, mem_long, mem_short, wts):
    assert DEPTH == 2
    (bl, tl), (bs, ts) = x_long.shape[:2], x_short.shape[:2]
    ml, ms = bl * tl, bs * ts
    xl, xs = x_long.reshape(ml, D_MODEL), x_short.reshape(ms, D_MODEL)
    tq = min(1024, tl)
    moe = lambda rows, route, i: _moe(rows, route, wts["norm_ffn"][i], wts["final_norm"], wts["moe_w_gate_up"],
                                      wts["moe_w_down"], i, final_norm=(i == DEPTH - 1))

    mix_l, q_l, conv_l = _inproj_conv(xl, wts["norm_mix"][0], wts["conv_w_in"], wts["conv_w"], bl, tl)
    zs = _inproj(xs, wts["norm_mix"][0], wts["conv_w_in"])
    mix_s, conv_s = _conv_short(zs, wts["conv_w"], conv_state, bs, ts)
    xa_l = _xattn(q_l, bl, tl, 0, *mem_long, 0, 1, tq)
    xa_s = _xattn(zs, bs, ts, 3, *mem_short, 0, SHORT_NB, ts)
    routed = _outproj_router_pair((xl, mix_l, xa_l), (xs, mix_s, xa_s), wts["conv_w_out"], wts["norm_ffn"][0],
                                  wts["w_router"][0])
    x1 = moe(*routed, 0)

    z = _inproj(x1, wts["norm_mix"][1], wts["ret_w_in"])
    zs = z[ml:]
    mix_l, ret_l = _ret_long(z, bl, tl, wts["ret_gn"], pos_long)
    mix_s, ret_s = _ret_short(zs, bs, ts, ret_state, wts["ret_gn"], pos_short)
    xa_l = _xattn(z, bl, tl, 6, *mem_long, 1, 1, tq)
    xa_s = _xattn(zs, bs, ts, 6, *mem_short, 1, SHORT_NB, ts)
    y_l = moe(*_outproj_router(x1, 0, mix_l, xa_l, wts["ret_w_out"], wts["norm_ffn"][1], wts["w_router"][1]), 1)
    y_s = moe(*_outproj_router(x1, ml, mix_s, xa_s, wts["ret_w_out"], wts["norm_ffn"][1], wts["w_router"][1]), 1)
    return (y_l.reshape(bl, tl, D_MODEL), y_s.reshape(bs, ts, D_MODEL), conv_l, conv_s, ret_l, ret_s)


def kernel(x_prompt, x_sample, state_conv, state_ret, cache_mem_k, cache_mem_v, mem_prompt, norm_mix, norm_ffn,
           mem_norm, w_mem_kv, conv_w_in, conv_w, conv_w_out, ret_w_in, ret_gn, ret_w_out, moe_w_group,
           moe_w_router, moe_w_gate_up, moe_w_down, final_norm):
    batch, seq = x_prompt.shape[0], x_prompt.shape[1]
    dec_seq = x_sample.shape[1]
    w_router = _router_weights(moe_w_group, moe_w_router)
    wts = dict(
        norm_mix=norm_mix, norm_ffn=norm_ffn, final_norm=final_norm, w_router=w_router,
        conv_w_in=conv_w_in[0].astype(BF16), conv_w=conv_w[0], conv_w_out=conv_w_out[0].astype(BF16),
        ret_w_in=ret_w_in[0].astype(BF16), ret_gn=ret_gn[0], ret_w_out=ret_w_out[0].astype(BF16),
        moe_w_gate_up=moe_w_gate_up, moe_w_down=moe_w_down,
    )
    mem_k_prompt, mem_v_prompt = _mem_kv(mem_prompt.reshape(batch * N_MEM, D_MODEL), mem_norm, w_mem_kv.astype(BF16))

    y_prompt, y_sample, conv_p, conv_s, ret_p, ret_s = _forward(
        x_prompt, x_sample, np.arange(seq), PAST_LEN + np.arange(dec_seq),
        state_conv[0], state_ret[0], (mem_k_prompt, mem_v_prompt), (cache_mem_k, cache_mem_v), wts)
    return (y_prompt, y_sample, conv_p[None], conv_s[None], ret_p[None], ret_s[None], mem_k_prompt, mem_v_prompt)
```

```python
import functools

import jax
import jax.numpy as jnp
import numpy as np
from jax import lax
from jax.experimental import pallas as pl
from jax.experimental.pallas import tpu as pltpu

F32 = jnp.float32
BF16 = jnp.bfloat16
I32 = jnp.int32

D_MODEL = 1024
DEPTH = 2
CONV_WIDTH = 3
RET_HEADS = 4
RET_DK = 256
RET_DV = 512
RET_CHUNK = 128
ROPE_BASE = 10000.0
N_MEM = 256
XA_HEADS = 4
XA_DIM = 256
N_GROUPS = 4
EXPERTS_PER_GROUP = 8
N_EXPERTS = N_GROUPS * EXPERTS_PER_GROUP
EXPERT_FF = 256
EPS = 1e-6

LANES = 128
AUG_W = D_MODEL + LANES
GATE_LANE0 = D_MODEL
GID_LANE = EXPERTS_PER_GROUP
ROUTER_ROWS = 64
VMEM_LIMIT = 56 * 1024 * 1024
MOE_TM = 512
SHORT_ROWS = 8
PAST_LEN = 16384
NT_DIMS = (((1,), (1,)), ((), ()))
TN_DIMS = (((0,), (0,)), ((), ()))


def _cparams(sem):
    return pltpu.CompilerParams(dimension_semantics=sem, vmem_limit_bytes=VMEM_LIMIT)


def _rms(x, g):
    return x * lax.rsqrt(jnp.mean(x * x, axis=-1, keepdims=True) + EPS) * g


def _silu(x):
    return x * jax.nn.sigmoid(x)


def _resident(shape):
    nd = len(shape)
    return pl.BlockSpec(shape, lambda *_: (0,) * nd, pipeline_mode=pl.Buffered(1))


def _memkv_kernel(x_ref, g_ref, w_ref, k_ref, v_ref, *, nb):
    h = _rms(x_ref[...], g_ref[...]).astype(BF16)
    kv = jnp.dot(h, w_ref[...], preferred_element_type=F32)
    for b in range(nb):
        rows = slice(b * N_MEM, (b + 1) * N_MEM)
        for hd in range(XA_HEADS):
            k_ref[b, :, hd, :] = kv[rows, hd * XA_DIM:(hd + 1) * XA_DIM]
            v_ref[b, :, hd, :] = kv[rows, D_MODEL + hd * XA_DIM:D_MODEL + (hd + 1) * XA_DIM]


def _mem_kv(mem2d, mem_norm, w_kv):
    batch = mem2d.shape[0] // N_MEM
    nb = min(2, batch)
    tm = nb * N_MEM
    out = jax.ShapeDtypeStruct((DEPTH, batch, N_MEM, XA_HEADS, XA_DIM), F32)
    out_spec = pl.BlockSpec((None, nb, N_MEM, XA_HEADS, XA_DIM), lambda l, j: (l, j, 0, 0, 0))
    return pl.pallas_call(
        functools.partial(_memkv_kernel, nb=nb),
        grid=(DEPTH, batch // nb),
        in_specs=[
            pl.BlockSpec((tm, D_MODEL), lambda l, j: (j, 0)),
            pl.BlockSpec((None, 1, D_MODEL), lambda l, j: (l, 0, 0)),
            pl.BlockSpec((None, D_MODEL, 2 * D_MODEL), lambda l, j: (l, 0, 0)),
        ],
        out_specs=[out_spec, out_spec],
        out_shape=[out, out],
        compiler_params=_cparams(("arbitrary", "arbitrary")),
        name="mem_kv",
    )(mem2d, mem_norm.reshape(DEPTH, 1, D_MODEL), w_kv)


def _inproj_kernel(x_ref, g_ref, w_ref, o_ref, *, tn):
    h = _rms(x_ref[...], g_ref[...]).astype(BF16)
    for j in range(o_ref.shape[1] // tn):
        sl = slice(j * tn, (j + 1) * tn)
        o_ref[:, sl] = jnp.dot(h, w_ref[:, sl], preferred_element_type=F32).astype(BF16)


def _inproj(x, g, w):
    m, n = x.shape[0], w.shape[1]
    tm = min(512, m)
    return pl.pallas_call(
        functools.partial(_inproj_kernel, tn=512),
        grid=(m // tm,),
        in_specs=[
            pl.BlockSpec((tm, D_MODEL), lambda i: (i, 0)),
            _resident((1, D_MODEL)),
            _resident((D_MODEL, n)),
        ],
        out_specs=pl.BlockSpec((tm, n), lambda i: (i, 0)),
        out_shape=jax.ShapeDtypeStruct((m, n), BF16),
        compiler_params=_cparams(("arbitrary",)),
        name="inproj",
    )(x, g.reshape(1, D_MODEL), w)


def _conv_taps(u, prev1, prev2, w_ref):
    return (w_ref[0:1, :] * prev2 + w_ref[1:2, :] * prev1) + w_ref[2:3, :] * u


def _inproj_conv_kernel(x_ref, g_ref, w_ref, cw_ref, mix_ref, q_ref, st_ref, carry, *, tiles_per_seq):
    i = pl.program_id(0)
    tt = x_ref.shape[0]
    col = lambda j: slice(j * D_MODEL, (j + 1) * D_MODEL)
    h = _rms(x_ref[...], g_ref[...]).astype(BF16)
    proj = lambda j: jnp.dot(h, w_ref[:, col(j)], preferred_element_type=F32)

    @pl.when(i % tiles_per_seq == 0)
    def _():
        carry[...] = jnp.zeros_like(carry)

    q_ref[...] = proj(3).astype(BF16)
    u = proj(1) * proj(2)
    pm2 = carry[6:7, :]
    pm1 = carry[7:8, :]
    row = lax.broadcasted_iota(I32, u.shape, 0)
    prev1 = jnp.where(row == 0, pm1, pltpu.roll(u, 1, 0))
    prev2 = jnp.where(row == 0, pm2, jnp.where(row == 1, pm1, pltpu.roll(u, 2, 0)))
    mix_ref[...] = (proj(0) * _conv_taps(u, prev1, prev2, cw_ref)).astype(BF16)
    carry[...] = u[tt - 8:tt, :]
    st_ref[0] = carry[6:8, :]


def _inproj_conv(x, g, w, conv_w, batch, seq):
    tt = min(512, seq)
    tps = seq // tt
    row = pl.BlockSpec((tt, D_MODEL), lambda i: (i, 0))
    out = jax.ShapeDtypeStruct((batch * seq, D_MODEL), BF16)
    return pl.pallas_call(
        functools.partial(_inproj_conv_kernel, tiles_per_seq=tps),
        grid=(batch * tps,),
        in_specs=[row, _resident((1, D_MODEL)), _resident(w.shape), _resident((CONV_WIDTH, D_MODEL))],
        out_specs=[row, row, pl.BlockSpec((1, 2, D_MODEL), lambda i: (i // tps, 0, 0))],
        out_shape=[out, out, jax.ShapeDtypeStruct((batch, 2, D_MODEL), F32)],
        scratch_shapes=[pltpu.VMEM((8, D_MODEL), F32)],
        compiler_params=_cparams(("arbitrary",)),
        name="inproj_conv",
    )(x, g.reshape(1, D_MODEL), w, conv_w)


def _conv_short_kernel(b_ref, c_ref, v_ref, w_ref, p1_ref, p2_ref, mix_ref, u_ref, *, seq):
    u = c_ref[...].astype(F32) * v_ref[...].astype(F32)
    t = lax.broadcasted_iota(I32, u.shape, 0) % seq
    prev1 = jnp.where(t == 0, p1_ref[...], pltpu.roll(u, 1, 0))
    prev2 = jnp.where(t <= 1, p2_ref[...], pltpu.roll(u, 2, 0))
    mix_ref[...] = (b_ref[...].astype(F32) * _conv_taps(u, prev1, prev2, w_ref)).astype(BF16)
    u_ref[...] = u


def _conv_short(z, conv_w, state0, batch, seq):
    m = batch * seq
    zeros = jnp.zeros((batch, seq - 1, D_MODEL), F32)
    p1 = jnp.concatenate([state0[:, 1:2], zeros], axis=1).reshape(m, D_MODEL)
    p2 = jnp.concatenate([state0, zeros[:, 1:]], axis=1).reshape(m, D_MODEL)
    col = lambda j: pl.BlockSpec((m, D_MODEL), lambda i, j=j: (0, j))
    full = pl.BlockSpec((m, D_MODEL), lambda i: (0, 0))
    mix, u = pl.pallas_call(
        functools.partial(_conv_short_kernel, seq=seq),
        grid=(1,),
        in_specs=[col(0), col(1), col(2), _resident((CONV_WIDTH, D_MODEL)), full, full],
        out_specs=[full, full],
        out_shape=[jax.ShapeDtypeStruct((m, D_MODEL), BF16), jax.ShapeDtypeStruct((m, D_MODEL), F32)],
        compiler_params=_cparams(("arbitrary",)),
        name="conv_short",
    )(z, z, z, conv_w, p1, p2)
    return mix, u.reshape(batch, seq, D_MODEL)[:, seq - 2:]


def _xattn_kernel(q_ref, k_hbm, v_hbm, o_ref, kbuf, vbuf, sem, *, nb, layer, n_bt):
    i = pl.program_id(0)
    t = pl.program_id(1)
    slot = i % 2

    def copies(bt, sl):
        out = []
        for b in range(nb):
            for h in range(XA_HEADS):
                out.append(pltpu.make_async_copy(k_hbm.at[layer, bt * nb + b, :, h, :], kbuf.at[sl, b, h], sem.at[0, sl]))
                out.append(pltpu.make_async_copy(v_hbm.at[layer, bt * nb + b, :, h, :], vbuf.at[sl, b, h], sem.at[1, sl]))
        return out

    @pl.when((i == 0) & (t == 0))
    def _():
        for c in copies(0, 0):
            c.start()

    @pl.when(t == 0)
    def _():
        @pl.when(i + 1 < n_bt)
        def _():
            for c in copies(i + 1, 1 - slot):
                c.start()

        for c in copies(i, slot):
            c.wait()

    scale = XA_DIM ** -0.5
    tq = q_ref.shape[1]
    rows = -(-tq // SHORT_ROWS) * SHORT_ROWS
    pairs = [(b, h) for b in range(nb) for h in range(XA_HEADS)]
    scores = []
    for b, h in pairs:
        q = q_ref[b, :, h * XA_DIM:(h + 1) * XA_DIM]
        if rows != tq:
            q = jnp.concatenate([q.astype(F32), jnp.zeros((rows - tq, XA_DIM), F32)], axis=0).astype(BF16)
        k = kbuf[slot, b, h].astype(BF16)
        scores.append(lax.dot_general(q, k, NT_DIMS, preferred_element_type=F32) * scale)
    s = jnp.concatenate(scores, axis=0)
    p = jnp.exp(s - jnp.max(s, axis=-1, keepdims=True))
    p = (p / jnp.sum(p, axis=-1, keepdims=True)).astype(BF16)
    for n, (b, h) in enumerate(pairs):
        v = vbuf[slot, b, h].astype(BF16)
        o = jnp.dot(p[n * rows:(n + 1) * rows], v, preferred_element_type=F32)
        o_ref[b, :, h * XA_DIM:(h + 1) * XA_DIM] = o[:tq].astype(BF16)


def _xattn(z, batch, seq, qcol, mem_k, mem_v, layer, nb, tq):
    n_bt = batch // nb
    plane = pltpu.VMEM((2, nb, XA_HEADS, N_MEM, XA_DIM), F32)
    if nb == 1:
        q_arr = z.reshape(1, z.shape[0], z.shape[1])
        nt = seq // tq
        q_spec = pl.BlockSpec((1, tq, D_MODEL), lambda i, t: (0, i * nt + t, qcol))
        o_spec = pl.BlockSpec((1, tq, D_MODEL), lambda i, t: (0, i * nt + t, 0))
        o_shape = (1, batch * seq, D_MODEL)
    else:
        q_arr = z.reshape(batch, seq, z.shape[1])
        q_spec = pl.BlockSpec((nb, tq, D_MODEL), lambda i, t: (i, t, qcol))
        o_spec = pl.BlockSpec((nb, tq, D_MODEL), lambda i, t: (i, t, 0))
        o_shape = (batch, seq, D_MODEL)
    return pl.pallas_call(
        functools.partial(_xattn_kernel, nb=nb, layer=layer, n_bt=n_bt),
        grid=(n_bt, seq // tq),
        in_specs=[q_spec, pl.BlockSpec(memory_space=pl.ANY), pl.BlockSpec(memory_space=pl.ANY)],
        out_specs=o_spec,
        out_shape=jax.ShapeDtypeStruct(o_shape, BF16),
        scratch_shapes=[plane, plane, pltpu.SemaphoreType.DMA((2, 2))],
        compiler_params=_cparams(("arbitrary", "arbitrary")),
        name="xattn",
    )(q_arr, mem_k, mem_v).reshape(batch * seq, D_MODEL)


def _rotary(x, cos, sin):
    half = x.shape[-1] // 2
    x1, x2 = x[:, :half], x[:, half:]
    return jnp.concatenate([x1 * cos - x2 * sin, x1 * sin + x2 * cos], axis=-1)


def _group_norm_gate(o, gate, gn):
    mu = jnp.mean(o, axis=-1, keepdims=True)
    var = jnp.mean(jnp.square(o - mu), axis=-1, keepdims=True)
    return _silu(gate) * ((o - mu) * lax.rsqrt(var + EPS) * gn)


def _ret_long_kernel(full_ref, q_ref, k_ref, v_ref, g_ref, cos_ref, sin_ref, dec_ref, inn_ref, tail_ref,
                     gn_ref, mix_ref, sfin_ref, s_acc, *, chunk):
    t = pl.program_id(1)
    heads = range(RET_HEADS)

    @pl.when(t == 0)
    def _():
        s_acc[...] = jnp.zeros_like(s_acc)

    cos, sin = cos_ref[...], sin_ref[...]
    qs = [_rotary(q_ref[0, :, h * RET_DK:(h + 1) * RET_DK].astype(F32), cos, sin) for h in heads]
    ks = [_rotary(k_ref[0, :, h * RET_DK:(h + 1) * RET_DK].astype(F32), cos, sin) * (RET_DK ** -0.5) for h in heads]
    for c in range(q_ref.shape[1] // chunk):
        sl = slice(c * chunk, (c + 1) * chunk)
        qc = [qs[h][sl].astype(BF16) for h in heads]
        kc = [ks[h][sl] for h in heads]
        vc = [v_ref[0, sl, h * RET_DV:(h + 1) * RET_DV] for h in heads]
        s_old = [s_acc[h] for h in heads]
        scores = [lax.dot_general(qc[h], kc[h].astype(BF16), NT_DIMS, preferred_element_type=F32) * dec_ref[h]
                  for h in heads]
        cross = [jnp.dot(qc[h], s_old[h].astype(BF16), preferred_element_type=F32) * inn_ref[h][:, 0:1]
                 for h in heads]
        o = [jnp.dot(scores[h].astype(BF16), vc[h], preferred_element_type=F32) + cross[h] for h in heads]
        for h in heads:
            kt = (kc[h] * tail_ref[h][:, 0:1]).astype(BF16)
            s_acc[h] = full_ref[h] * s_old[h] + lax.dot_general(kt, vc[h], TN_DIMS, preferred_element_type=F32)
        og = jnp.concatenate(o, axis=0)
        mu = jnp.mean(og, axis=-1, keepdims=True)
        var = jnp.mean(jnp.square(og - mu), axis=-1, keepdims=True)
        on = (og - mu) * lax.rsqrt(var + EPS)
        for h in heads:
            vs = slice(h * RET_DV, (h + 1) * RET_DV)
            gate = g_ref[0, sl, vs]
            mix_ref[0, sl, vs] = _silu(gate) * (on[h * chunk:(h + 1) * chunk] * gn_ref[:, vs]).astype(BF16)

    @pl.when(t == pl.num_programs(1) - 1)
    def _():
        sfin_ref[0] = s_acc[...]


def _ret_tables(pos, chunk):
    half = RET_DK // 2
    inv = np.power(ROPE_BASE, -np.arange(half, dtype=np.float64) / half)
    ang = pos.astype(np.float64)[:, None] * inv[None, :]
    log_g = np.log1p(-np.exp2(-5.0 - np.arange(RET_HEADS, dtype=np.float64)))
    idx = np.arange(chunk)
    diff = idx[:, None] - idx[None, :]
    decay = np.where(diff[None] >= 0, np.exp(np.maximum(diff, 0)[None] * log_g[:, None, None]), 0.0)
    inner = np.exp((idx + 1)[None, :] * log_g[:, None])
    tail = np.exp((chunk - 1 - idx)[None, :] * log_g[:, None])
    full = np.exp(chunk * log_g)
    return tuple(a.astype(np.float32) for a in (np.cos(ang), np.sin(ang), decay, inner, tail, full))


def _ret_long(z, batch, seq, gn, pos):
    z3 = z.reshape(1, z.shape[0], z.shape[1])
    tc = min(512, seq)
    nt = seq // tc
    chunk = 2 * RET_CHUNK if tc % (2 * RET_CHUNK) == 0 else RET_CHUNK
    cos, sin, decay, inner, tail, full = _ret_tables(pos, chunk)
    lane_b = lambda a: np.ascontiguousarray(np.broadcast_to(a[:, :, None], (RET_HEADS, chunk, LANES)))
    qw = RET_HEADS * RET_DK
    vw = RET_HEADS * RET_DV
    state_spec = pl.BlockSpec((1, RET_HEADS, RET_DK, RET_DV), lambda b, t: (b, 0, 0, 0))
    mix, state = pl.pallas_call(
        functools.partial(_ret_long_kernel, chunk=chunk),
        grid=(batch, seq // tc),
        in_specs=[
            pl.BlockSpec(memory_space=pltpu.SMEM),
            pl.BlockSpec((1, tc, qw), lambda b, t: (0, b * nt + t, 0)),
            pl.BlockSpec((1, tc, qw), lambda b, t: (0, b * nt + t, 1)),
            pl.BlockSpec((1, tc, vw), lambda b, t: (0, b * nt + t, 1)),
            pl.BlockSpec((1, tc, vw), lambda b, t: (0, b * nt + t, 2)),
            pl.BlockSpec((tc, RET_DK // 2), lambda b, t: (t, 0)),
            pl.BlockSpec((tc, RET_DK // 2), lambda b, t: (t, 0)),
            _resident((RET_HEADS, chunk, chunk)), _resident((RET_HEADS, chunk, LANES)),
            _resident((RET_HEADS, chunk, LANES)), _resident((1, vw)),
        ],
        out_specs=[pl.BlockSpec((1, tc, vw), lambda b, t: (0, b * nt + t, 0)), state_spec],
        out_shape=[jax.ShapeDtypeStruct((1, batch * seq, vw), BF16),
                   jax.ShapeDtypeStruct((batch, RET_HEADS, RET_DK, RET_DV), F32)],
        scratch_shapes=[pltpu.VMEM((RET_HEADS, RET_DK, RET_DV), F32)],
        compiler_params=_cparams(("arbitrary", "arbitrary")),
        name="ret_long",
    )(full, z3, z3, z3, z3, cos, sin, decay, lane_b(inner), lane_b(tail), gn.reshape(1, -1))
    return mix.reshape(batch * seq, vw), state


def _ret_short_kernel(full_ref, q_ref, k_ref, v_ref, g_ref, cos_ref, sin_ref, dec_ref, inn_ref, tail_ref,
                      gn_ref, s0_ref, mix_ref, snew_ref, *, nb, seq):
    def pad_rows(a):
        return jnp.concatenate([a, jnp.zeros((SHORT_ROWS - seq, a.shape[1]), F32)], axis=0)

    cos, sin = cos_ref[...], sin_ref[...]
    for b in range(nb):
        for h in range(RET_HEADS):
            qs = slice(h * RET_DK, (h + 1) * RET_DK)
            vs = slice(h * RET_DV, (h + 1) * RET_DV)
            q = pad_rows(_rotary(q_ref[b, :, qs].astype(F32), cos, sin)).astype(BF16)
            k = pad_rows(_rotary(k_ref[b, :, qs].astype(F32), cos, sin) * (RET_DK ** -0.5))
            v = pad_rows(v_ref[b, :, vs].astype(F32)).astype(BF16)
            s_old = s0_ref[b, h]
            scores = lax.dot_general(q, k.astype(BF16), NT_DIMS, preferred_element_type=F32) * dec_ref[h]
            o = jnp.dot(scores.astype(BF16), v, preferred_element_type=F32)
            o = o + jnp.dot(q, s_old.astype(BF16), preferred_element_type=F32) * inn_ref[h]
            snew_ref[b, h] = full_ref[h] * s_old + lax.dot_general((k * tail_ref[h]).astype(BF16), v, TN_DIMS,
                                                                   preferred_element_type=F32)
            mix_ref[b, :, vs] = _group_norm_gate(o[:seq], g_ref[b, :, vs].astype(F32), gn_ref[:, vs]).astype(BF16)


def _ret_short(z, batch, seq, state0, gn, pos, nb=4):
    z3 = z.reshape(batch, seq, z.shape[1])
    cos, sin, decay, inner, tail, full = _ret_tables(pos, seq)
    qw = RET_HEADS * RET_DK
    vw = RET_HEADS * RET_DV
    small = lambda a: _resident(a.shape)
    rp = SHORT_ROWS - seq
    decay = np.pad(decay, ((0, 0), (0, rp), (0, rp)))
    inner3 = np.pad(inner, ((0, 0), (0, rp)))[:, :, None]
    tail3 = np.pad(tail, ((0, 0), (0, rp)))[:, :, None]
    mix, state = pl.pallas_call(
        functools.partial(_ret_short_kernel, nb=nb, seq=seq),
        grid=(batch // nb,),
        in_specs=[
            pl.BlockSpec(memory_space=pltpu.SMEM),
            pl.BlockSpec((nb, seq, qw), lambda i: (i, 0, 0)),
            pl.BlockSpec((nb, seq, qw), lambda i: (i, 0, 1)),
            pl.BlockSpec((nb, seq, vw), lambda i: (i, 0, 1)),
            pl.BlockSpec((nb, seq, vw), lambda i: (i, 0, 2)),
            small(cos), small(sin), small(decay), small(inner3), small(tail3),
            _resident((1, vw)),
            pl.BlockSpec((nb, RET_HEADS, RET_DK, RET_DV), lambda i: (i, 0, 0, 0)),
        ],
        out_specs=[
            pl.BlockSpec((nb, seq, vw), lambda i: (i, 0, 0)),
            pl.BlockSpec((nb, RET_HEADS, RET_DK, RET_DV), lambda i: (i, 0, 0, 0)),
        ],
        out_shape=[jax.ShapeDtypeStruct((batch, seq, vw), BF16),
                   jax.ShapeDtypeStruct((batch, RET_HEADS, RET_DK, RET_DV), F32)],
        compiler_params=_cparams(("arbitrary",)),
        name="ret_short",
    )(full, z3, z3, z3, z3, cos, sin, decay, inner3, tail3, gn.reshape(1, -1), state0)
    return mix.reshape(batch * seq, vw), state


def _route(logits_t):
    n = logits_t.shape[1]
    epg = EXPERTS_PER_GROUP
    row = lax.broadcasted_iota(I32, (epg, n), 0).astype(F32)
    neg = jnp.float32(-1e30)
    big = jnp.float32(epg)
    is_g = row < N_GROUPS
    lg = jnp.where(is_g, logits_t[0:epg], neg)
    gmax = jnp.max(lg, axis=0, keepdims=True)
    gidx = jnp.min(jnp.where(lg == gmax, row, big), axis=0, keepdims=True)
    pg_sel = 1.0 / jnp.sum(jnp.where(is_g, jnp.exp(lg - gmax), 0.0), axis=0, keepdims=True)
    sel = logits_t[epg:2 * epg]
    for g in range(1, N_GROUPS):
        sel = jnp.where(gidx == g, logits_t[epg * (g + 1):epg * (g + 2)], sel)
    e = jnp.exp(sel - jnp.max(sel, axis=0, keepdims=True))
    pe = e / jnp.sum(e, axis=0, keepdims=True)
    t1 = jnp.max(pe, axis=0, keepdims=True)
    i1 = jnp.min(jnp.where(pe == t1, row, big), axis=0, keepdims=True)
    pe2 = jnp.where(row == i1, -1.0, pe)
    t2 = jnp.max(pe2, axis=0, keepdims=True)
    i2 = jnp.min(jnp.where(pe2 == t2, row, big), axis=0, keepdims=True)
    den = t1 + t2
    gates = jnp.where(row == i1, t1 / den * pg_sel, 0.0) + jnp.where(row == i2, t2 / den * pg_sel, 0.0)
    ids = jnp.where(row == 0, gidx, 0.0)
    block = jnp.concatenate([gates, ids, jnp.zeros((LANES - 2 * epg, n), F32)], axis=0)
    return block.T


def _outproj_pair_kernel(xl_ref, mixl_ref, xal_ref, xs_ref, mixs_ref, xas_ref, wa_ref, wb_ref, g_ref, wr_ref,
                         o_ref, r_ref, *, sub, n_long):
    i = pl.program_id(0)

    @pl.when(i < n_long)
    def _():
        _outproj_kernel(xl_ref, mixl_ref, xal_ref, wa_ref, wb_ref, g_ref, wr_ref, o_ref, r_ref, sub=sub)

    @pl.when(i >= n_long)
    def _():
        _outproj_kernel(xs_ref, mixs_ref, xas_ref, wa_ref, wb_ref, g_ref, wr_ref, o_ref, r_ref, sub=sub)


def _outproj_kernel(x_ref, mix_ref, xa_ref, wa_ref, wb_ref, g_ref, wr_ref, o_ref, r_ref, *, sub):
    acc = jnp.dot(mix_ref[...], wa_ref[...], preferred_element_type=F32)
    acc = acc + jnp.dot(xa_ref[...], wb_ref[...], preferred_element_type=F32)
    o_ref[:, :D_MODEL] = x_ref[...] + acc
    logits = []
    for s in range(x_ref.shape[0] // sub):
        rows = slice(s * sub, (s + 1) * sub)
        h2 = _rms(o_ref[rows, :D_MODEL], g_ref[...])
        hi = h2.astype(BF16)
        lo = (h2 - hi.astype(F32)).astype(BF16)
        p_hi = lax.dot_general(wr_ref[...], hi, NT_DIMS, preferred_element_type=F32)
        p_lo = lax.dot_general(wr_ref[0:ROUTER_ROWS, :], lo, NT_DIMS, preferred_element_type=F32)
        logits.append((p_hi[:ROUTER_ROWS] + p_lo) + p_hi[ROUTER_ROWS:])
    route = _route(jnp.concatenate(logits, axis=1))
    o_ref[:, D_MODEL:] = route
    r_ref[...] = route


def _router_weights(w_group, w_router):
    epg = EXPERTS_PER_GROUP
    zeros = lambda r: jnp.zeros((DEPTH, r, D_MODEL), F32)
    wt = jnp.concatenate([jnp.swapaxes(w_group, 1, 2), zeros(epg - N_GROUPS), jnp.swapaxes(w_router, 1, 2),
                          zeros(ROUTER_ROWS - epg - N_EXPERTS)], axis=1)
    hi = wt.astype(BF16)
    return jnp.concatenate([hi, (wt - hi.astype(F32)).astype(BF16)], axis=1)


def _outproj_router(x, x_off, mix, xa, w_out, g_ffn, w_router2):
    m, cm = mix.shape
    tm = min(512, m)
    xb = x_off // tm
    row = lambda w, off=0: pl.BlockSpec((tm, w), lambda i: (i + off, 0))
    return pl.pallas_call(
        functools.partial(_outproj_kernel, sub=min(128, tm)),
        grid=(m // tm,),
        in_specs=[row(D_MODEL, xb), row(cm), row(D_MODEL), _resident((cm, D_MODEL)), _resident((D_MODEL, D_MODEL)),
                  _resident((1, D_MODEL)), _resident((2 * ROUTER_ROWS, D_MODEL))],
        out_specs=[row(AUG_W), row(LANES)],
        out_shape=[jax.ShapeDtypeStruct((m, AUG_W), F32), jax.ShapeDtypeStruct((m, LANES), F32)],
        compiler_params=_cparams(("arbitrary",)),
        name="outproj_router",
    )(x, mix, xa, w_out[:cm], w_out[cm:], g_ffn.reshape(1, D_MODEL), w_router2)


def _outproj_router_pair(long, short, w_out, g_ffn, w_router2):
    tm = 512
    cm = long[1].shape[1]
    nl, ns = long[0].shape[0] // tm, short[0].shape[0] // tm
    first = lambda w: pl.BlockSpec((tm, w), lambda i: (jnp.minimum(i, nl - 1), 0))
    second = lambda w: pl.BlockSpec((tm, w), lambda i: (jnp.maximum(i - nl, 0), 0))
    row = lambda w: pl.BlockSpec((tm, w), lambda i: (i, 0))
    m = (nl + ns) * tm
    return pl.pallas_call(
        functools.partial(_outproj_pair_kernel, sub=128, n_long=nl),
        grid=(nl + ns,),
        in_specs=[first(D_MODEL), first(cm), first(D_MODEL), second(D_MODEL), second(cm), second(D_MODEL),
                  _resident((cm, D_MODEL)), _resident((D_MODEL, D_MODEL)),
                  _resident((1, D_MODEL)), _resident((2 * ROUTER_ROWS, D_MODEL))],
        out_specs=[row(AUG_W), row(LANES)],
        out_shape=[jax.ShapeDtypeStruct((m, AUG_W), F32), jax.ShapeDtypeStruct((m, LANES), F32)],
        compiler_params=_cparams(("arbitrary",)),
        name="outproj_router_pair",
    )(*long, *short, w_out[:cm], w_out[cm:], g_ffn.reshape(1, D_MODEL), w_router2)


STEP_VALID, STEP_FIRST, STEP_LAST, STEP_NEWGROUP = 1, 2, 4, 8


def _moe_plan(gid, tm):
    m = gid.shape[0]
    ntile = m // tm
    nstep = ntile + N_GROUPS - 1
    counts = jnp.sum((gid[:, None] == jnp.arange(N_GROUPS, dtype=I32)[None, :]).astype(I32), axis=0)
    gend = jnp.cumsum(counts)
    gstart = gend - counts
    row_tok = jnp.argsort(gid, stable=True).astype(I32)
    inner_ok = (counts[1:] > 0) & (gstart[1:] % tm != 0)
    cuts = jnp.concatenate([jnp.arange(ntile, dtype=I32) * tm, jnp.where(inner_ok, gstart[1:], m)])
    cut_ok = jnp.concatenate([jnp.ones((ntile,), bool), inner_ok])
    rank = jnp.sum(((cuts[None, :] < cuts[:, None]) & cut_ok[None, :]).astype(I32), axis=1)
    sel = jnp.where(cut_ok, rank, nstep)[None, :] == jnp.arange(nstep, dtype=I32)[:, None]
    valid = jnp.any(sel, axis=1)
    start = jnp.sum(jnp.where(sel, cuts[None, :], 0), axis=1)
    start = jnp.where(valid, start, jnp.max(start))
    st_tile = start // tm
    st_gid = jnp.sum((gend[None, :] <= start[:, None]).astype(I32), axis=1)
    first = valid & (start % tm == 0)
    next_first = jnp.concatenate([first[1:], jnp.ones((1,), bool)])
    next_valid = jnp.concatenate([valid[1:], jnp.zeros((1,), bool)])
    last = valid & (next_first | ~next_valid)
    prev_gid = jnp.concatenate([jnp.full((1,), -1, I32), st_gid[:-1]])
    newgroup = valid & (st_gid != prev_gid)
    flags = valid * STEP_VALID + first * STEP_FIRST + last * STEP_LAST + newgroup * STEP_NEWGROUP
    return st_tile, st_gid, flags.astype(I32), row_tok


def _moe_kernel(tile_ref, gid_ref, flag_ref, tok_ref, xaug_hbm, g_ref, fin_ref, wgu_hbm, wd_hbm, out_hbm,
                xbuf, ybuf, wgu_bf, wd_bf, wgu_stage, wd_stage, gsem, ssem, wsem, *, tm, ntile, layer, final_norm):
    s = pl.program_id(0)
    tile = tile_ref[s]
    grp = gid_ref[s]
    flags = flag_ref[s]
    valid = (flags & STEP_VALID) > 0
    first = (flags & STEP_FIRST) > 0
    last = (flags & STEP_LAST) > 0
    newgroup = (flags & STEP_NEWGROUP) > 0
    slot = tile % 2

    def weight_copies(j):
        e = grp * EXPERTS_PER_GROUP + j
        return (pltpu.make_async_copy(wgu_hbm.at[layer, e], wgu_stage.at[j % 2], wsem.at[j % 2]),
                pltpu.make_async_copy(wd_hbm.at[layer, e], wd_stage.at[j % 2], wsem.at[j % 2]))

    def gather_start(t, sl):
        for r in range(tm):
            tok = tok_ref[t * tm + r]
            pltpu.make_async_copy(xaug_hbm.at[pl.ds(tok, 1)], xbuf.at[sl, pl.ds(r, 1)],
                                  gsem.at[sl]).start(priority=r % 2)

    def scatter_start(t, sl):
        for r in range(tm):
            tok = tok_ref[t * tm + r]
            pltpu.make_async_copy(ybuf.at[sl, pl.ds(r, 1)], out_hbm.at[pl.ds(tok, 1)],
                                  ssem.at[sl]).start(priority=r % 2)

    def gather_wait(sl):
        pltpu.make_async_copy(xaug_hbm.at[pl.ds(0, tm)], xbuf.at[sl], gsem.at[sl]).wait()

    def scatter_wait(sl):
        pltpu.make_async_copy(ybuf.at[sl], out_hbm.at[pl.ds(0, tm)], ssem.at[sl]).wait()

    @pl.when(s == 0)
    def _():
        gather_start(0, 0)

    @pl.when(first & (tile + 1 < ntile))
    def _():
        gather_start(tile + 1, 1 - slot)

    @pl.when(first)
    def _():
        gather_wait(slot)

        @pl.when(tile >= 2)
        def _():
            scatter_wait(slot)

        ybuf[slot] = xbuf[slot, :, 0:D_MODEL]

    def experts(stream_weights):
        xn = xbuf[slot, :, 0:D_MODEL]
        gl = xbuf[slot, :, GATE_LANE0:AUG_W]
        mine = gl[:, GID_LANE:GID_LANE + 1] == grp.astype(F32)
        h2 = _rms(xn, g_ref[...]).astype(BF16)
        acc = jnp.zeros((tm, D_MODEL), F32)
        if stream_weights:
            for c in weight_copies(0):
                c.start()
        for j in range(EXPERTS_PER_GROUP):
            if stream_weights:
                if j + 1 < EXPERTS_PER_GROUP:
                    for c in weight_copies(j + 1):
                        c.start()
                for c in weight_copies(j):
                    c.wait()
                wgu_bf[j] = wgu_stage[j % 2].astype(BF16)
                wd_bf[j] = wd_stage[j % 2].astype(BF16)
            gu = jnp.dot(h2, wgu_bf[j], preferred_element_type=F32)
            gate = jnp.where(mine, gl[:, j:j + 1], 0.0)
            act = _silu(gu[:, :EXPERT_FF]) * gu[:, EXPERT_FF:] * gate
            acc = acc + jnp.dot(act.astype(BF16), wd_bf[j], preferred_element_type=F32)
        ybuf[slot] = ybuf[slot] + acc

    @pl.when(newgroup)
    def _():
        experts(True)

    @pl.when(valid & jnp.logical_not(newgroup))
    def _():
        experts(False)

    @pl.when(last)
    def _():
        if final_norm:
            ybuf[slot] = _rms(ybuf[slot], fin_ref[...])
        scatter_start(tile, slot)

        @pl.when(tile == ntile - 1)
        def _():
            if ntile >= 2:
                scatter_wait(1 - slot)
            scatter_wait(slot)


def _moe(xaug, route, g_ffn, final_g, w_gu, w_down, layer, final_norm):
    m = xaug.shape[0]
    tm = min(MOE_TM, m)
    gid = route[:, GID_LANE].astype(I32)
    st_tile, st_gid, flags, row_tok = _moe_plan(gid, tm)
    epg = EXPERTS_PER_GROUP
    grid_spec = pltpu.PrefetchScalarGridSpec(
        num_scalar_prefetch=4,
        grid=(st_tile.shape[0],),
        in_specs=[
            pl.BlockSpec(memory_space=pl.ANY),
            pl.BlockSpec((1, D_MODEL), lambda s, *_: (0, 0)),
            pl.BlockSpec((1, D_MODEL), lambda s, *_: (0, 0)),
            pl.BlockSpec(memory_space=pl.ANY),
            pl.BlockSpec(memory_space=pl.ANY),
        ],
        out_specs=pl.BlockSpec(memory_space=pl.ANY),
        scratch_shapes=[
            pltpu.VMEM((2, tm, AUG_W), F32),
            pltpu.VMEM((2, tm, D_MODEL), F32),
            pltpu.VMEM((epg, D_MODEL, 2 * EXPERT_FF), BF16),
            pltpu.VMEM((epg, EXPERT_FF, D_MODEL), BF16),
            pltpu.VMEM((2, D_MODEL, 2 * EXPERT_FF), F32),
            pltpu.VMEM((2, EXPERT_FF, D_MODEL), F32),
            pltpu.SemaphoreType.DMA((2,)),
            pltpu.SemaphoreType.DMA((2,)),
            pltpu.SemaphoreType.DMA((2,)),
        ],
    )
    return pl.pallas_call(
        functools.partial(_moe_kernel, tm=tm, ntile=m // tm, layer=layer, final_norm=final_norm),
        grid_spec=grid_spec,
        out_shape=jax.ShapeDtypeStruct((m, D_MODEL), F32),
        compiler_params=_cparams(("arbitrary",)),
        name="moe",
    )(st_tile, st_gid, flags, row_tok, xaug, g_ffn.reshape(1, D_MODEL), final_g.reshape(1, D_MODEL), w_gu, w_down)


SHORT_NB = 4


def _forward(x_long, x_short, pos_long, pos_short, conv_state, ret_state, mem_long, mem_short, wts):
    assert DEPTH == 2
    (bl, tl), (bs, ts) = x_long.shape[:2], x_short.shape[:2]
    ml, ms = bl * tl, bs * ts
    xl, xs = x_long.reshape(ml, D_MODEL), x_short.reshape(ms, D_MODEL)
    tq = min(1024, tl)
    moe = lambda rows, route, i: _moe(rows, route, wts["norm_ffn"][i], wts["final_norm"], wts["moe_w_gate_up"],
                                      wts["moe_w_down"], i, final_norm=(i == DEPTH - 1))

    mix_l, q_l, conv_l = _inproj_conv(xl, wts["norm_mix"][0], wts["conv_w_in"], wts["conv_w"], bl, tl)
    zs = _inproj(xs, wts["norm_mix"][0], wts["conv_w_in"])
    mix_s, conv_s = _conv_short(zs, wts["conv_w"], conv_state, bs, ts)
    xa_l = _xattn(q_l, bl, tl, 0, *mem_long, 0, 1, tq)
    xa_s = _xattn(zs, bs, ts, 3, *mem_short, 0, SHORT_NB, ts)
    routed = _outproj_router_pair((xl, mix_l, xa_l), (xs, mix_s, xa_s), wts["conv_w_out"], wts["norm_ffn"][0],
                                  wts["w_router"][0])
    x1 = moe(*routed, 0)

    z = _inproj(x1, wts["norm_mix"][1], wts["ret_w_in"])
    zs = z[ml:]
    mix_l, ret_l = _ret_long(z, bl, tl, wts["ret_gn"], pos_long)
    mix_s, ret_s = _ret_short(zs, bs, ts, ret_state, wts["ret_gn"], pos_short)
    xa_l = _xattn(z, bl, tl, 6, *mem_long, 1, 1, tq)
    xa_s = _xattn(zs, bs, ts, 6, *mem_short, 1, SHORT_NB, ts)
    y_l = moe(*_outproj_router(x1, 0, mix_l, xa_l, wts["ret_w_out"], wts["norm_ffn"][1], wts["w_router"][1]), 1)
    y_s = moe(*_outproj_router(x1, ml, mix_s, xa_s, wts["ret_w_out"], wts["norm_ffn"][1], wts["w_router"][1]), 1)
    return (y_l.reshape(bl, tl, D_MODEL), y_s.reshape(bs, ts, D_MODEL), conv_l, conv_s, ret_l, ret_s)


def kernel(x_prompt, x_sample, state_conv, state_ret, cache_mem_k, cache_mem_v, mem_prompt, norm_mix, norm_ffn,
           mem_norm, w_mem_kv, conv_w_in, conv_w, conv_w_out, ret_w_in, ret_gn, ret_w_out, moe_w_group,
           moe_w_router, moe_w_gate_up, moe_w_down, final_norm):
    batch, seq = x_prompt.shape[0], x_prompt.shape[1]
    dec_seq = x_sample.shape[1]
    w_router = _router_weights(moe_w_group, moe_w_router)
    wts = dict(
        norm_mix=norm_mix, norm_ffn=norm_ffn, final_norm=final_norm, w_router=w_router,
        conv_w_in=conv_w_in[0].astype(BF16), conv_w=conv_w[0], conv_w_out=conv_w_out[0].astype(BF16),
        ret_w_in=ret_w_in[0].astype(BF16), ret_gn=ret_gn[0], ret_w_out=ret_w_out[0].astype(BF16),
        moe_w_gate_up=moe_w_gate_up, moe_w_down=moe_w_down,
    )
    mem_k_prompt, mem_v_prompt = _mem_kv(mem_prompt.reshape(batch * N_MEM, D_MODEL), mem_norm, w_mem_kv.astype(BF16))

    y_prompt, y_sample, conv_p, conv_s, ret_p, ret_s = _forward(
        x_prompt, x_sample, np.arange(seq), PAST_LEN + np.arange(dec_seq),
        state_conv[0], state_ret[0], (mem_k_prompt, mem_v_prompt), (cache_mem_k, cache_mem_v), wts)
    return (y_prompt, y_sample, conv_p[None], conv_s[None], ret_p[None], ret_s[None], mem_k_prompt, mem_v_prompt)
```

```python
import functools

import jax
import jax.numpy as jnp
import numpy as np
from jax import lax
from jax.experimental import pallas as pl
from jax.experimental.pallas import tpu as pltpu

F32 = jnp.float32
BF16 = jnp.bfloat16
I32 = jnp.int32

D_MODEL = 1024
DEPTH = 2
CONV_WIDTH = 3
RET_HEADS = 4
RET_DK = 256
RET_DV = 512
RET_CHUNK = 128
ROPE_BASE = 10000.0
N_MEM = 256
XA_HEADS = 4
XA_DIM = 256
N_GROUPS = 4
EXPERTS_PER_GROUP = 8
N_EXPERTS = N_GROUPS * EXPERTS_PER_GROUP
EXPERT_FF = 256
EPS = 1e-6

LANES = 128
AUG_W = D_MODEL + LANES
GATE_LANE0 = D_MODEL
GID_LANE = EXPERTS_PER_GROUP
ROUTER_ROWS = 64
VMEM_LIMIT = 56 * 1024 * 1024
ROW_TILE = 512
COL_CHUNK = 512
ROUTER_SUB = 128
XATTN_ROWS = 1024
MOE_TM = 512
SHORT_ROWS = 8
SHORT_NB = 8
PAST_LEN = 16384
NT_DIMS = (((1,), (1,)), ((), ()))
TN_DIMS = (((0,), (0,)), ((), ()))


def _cparams(sem):
    return pltpu.CompilerParams(dimension_semantics=sem, vmem_limit_bytes=VMEM_LIMIT)


def _rms(x, g):
    return x * lax.rsqrt(jnp.mean(x * x, axis=-1, keepdims=True) + EPS) * g


def _silu(x):
    return x * jax.nn.sigmoid(x)


def _resident(shape):
    nd = len(shape)
    return pl.BlockSpec(shape, lambda *_: (0,) * nd, pipeline_mode=pl.Buffered(1))


def _memkv_kernel(x_ref, g_ref, w_ref, k_ref, v_ref, *, nb):
    h = _rms(x_ref[...], g_ref[...]).astype(BF16)
    kv = jnp.dot(h, w_ref[...], preferred_element_type=F32)
    for b in range(nb):
        rows = slice(b * N_MEM, (b + 1) * N_MEM)
        for hd in range(XA_HEADS):
            k_ref[b, :, hd, :] = kv[rows, hd * XA_DIM:(hd + 1) * XA_DIM]
            v_ref[b, :, hd, :] = kv[rows, D_MODEL + hd * XA_DIM:D_MODEL + (hd + 1) * XA_DIM]


def _mem_kv(mem2d, mem_norm, w_kv):
    batch = mem2d.shape[0] // N_MEM
    nb = min(2, batch)
    tm = nb * N_MEM
    out = jax.ShapeDtypeStruct((DEPTH, batch, N_MEM, XA_HEADS, XA_DIM), F32)
    out_spec = pl.BlockSpec((None, nb, N_MEM, XA_HEADS, XA_DIM), lambda l, j: (l, j, 0, 0, 0))
    return pl.pallas_call(
        functools.partial(_memkv_kernel, nb=nb),
        grid=(DEPTH, batch // nb),
        in_specs=[
            pl.BlockSpec((tm, D_MODEL), lambda l, j: (j, 0)),
            pl.BlockSpec((None, 1, D_MODEL), lambda l, j: (l, 0, 0)),
            pl.BlockSpec((None, D_MODEL, 2 * D_MODEL), lambda l, j: (l, 0, 0)),
        ],
        out_specs=[out_spec, out_spec],
        out_shape=[out, out],
        compiler_params=_cparams(("arbitrary", "arbitrary")),
        name="mem_kv",
    )(mem2d, mem_norm.reshape(DEPTH, 1, D_MODEL), w_kv)


def _inproj_kernel(x_ref, g_ref, w_ref, o_ref, *, tn):
    h = _rms(x_ref[...], g_ref[...]).astype(BF16)
    for j in range(o_ref.shape[1] // tn):
        sl = slice(j * tn, (j + 1) * tn)
        o_ref[:, sl] = jnp.dot(h, w_ref[:, sl], preferred_element_type=F32).astype(BF16)


def _inproj(x, g, w):
    m, n = x.shape[0], w.shape[1]
    tm = min(ROW_TILE, m)
    return pl.pallas_call(
        functools.partial(_inproj_kernel, tn=COL_CHUNK),
        grid=(m // tm,),
        in_specs=[
            pl.BlockSpec((tm, D_MODEL), lambda i: (i, 0)),
            _resident((1, D_MODEL)),
            _resident((D_MODEL, n)),
        ],
        out_specs=pl.BlockSpec((tm, n), lambda i: (i, 0)),
        out_shape=jax.ShapeDtypeStruct((m, n), BF16),
        compiler_params=_cparams(("arbitrary",)),
        name="inproj",
    )(x, g.reshape(1, D_MODEL), w)


def _conv_taps(u, prev1, prev2, w_ref):
    return (w_ref[0:1, :] * prev2 + w_ref[1:2, :] * prev1) + w_ref[2:3, :] * u


def _inproj_conv_kernel(x_ref, g_ref, w_ref, cw_ref, mix_ref, q_ref, st_ref, carry, *, tiles_per_seq):
    i = pl.program_id(0)
    tt = x_ref.shape[0]
    col = lambda j: slice(j * D_MODEL, (j + 1) * D_MODEL)
    h = _rms(x_ref[...], g_ref[...]).astype(BF16)
    proj = lambda j: jnp.dot(h, w_ref[:, col(j)], preferred_element_type=F32)

    @pl.when(i % tiles_per_seq == 0)
    def _():
        carry[...] = jnp.zeros_like(carry)

    q_ref[...] = proj(3).astype(BF16)
    u = proj(1) * proj(2)
    pm2 = carry[6:7, :]
    pm1 = carry[7:8, :]
    row = lax.broadcasted_iota(I32, u.shape, 0)
    prev1 = jnp.where(row == 0, pm1, pltpu.roll(u, 1, 0))
    prev2 = jnp.where(row == 0, pm2, jnp.where(row == 1, pm1, pltpu.roll(u, 2, 0)))
    mix_ref[...] = (proj(0) * _conv_taps(u, prev1, prev2, cw_ref)).astype(BF16)
    carry[...] = u[tt - 8:tt, :]
    st_ref[0] = carry[6:8, :]


def _inproj_conv(x, g, w, conv_w, batch, seq):
    tt = min(ROW_TILE, seq)
    tps = seq // tt
    row = pl.BlockSpec((tt, D_MODEL), lambda i: (i, 0))
    out = jax.ShapeDtypeStruct((batch * seq, D_MODEL), BF16)
    return pl.pallas_call(
        functools.partial(_inproj_conv_kernel, tiles_per_seq=tps),
        grid=(batch * tps,),
        in_specs=[row, _resident((1, D_MODEL)), _resident(w.shape), _resident((CONV_WIDTH, D_MODEL))],
        out_specs=[row, row, pl.BlockSpec((1, 2, D_MODEL), lambda i: (i // tps, 0, 0))],
        out_shape=[out, out, jax.ShapeDtypeStruct((batch, 2, D_MODEL), F32)],
        scratch_shapes=[pltpu.VMEM((8, D_MODEL), F32)],
        compiler_params=_cparams(("arbitrary",)),
        name="inproj_conv",
    )(x, g.reshape(1, D_MODEL), w, conv_w)


def _conv_short_kernel(b_ref, c_ref, v_ref, w_ref, p1_ref, p2_ref, mix_ref, u_ref, *, seq):
    u = c_ref[...].astype(F32) * v_ref[...].astype(F32)
    t = lax.broadcasted_iota(I32, u.shape, 0) % seq
    prev1 = jnp.where(t == 0, p1_ref[...], pltpu.roll(u, 1, 0))
    prev2 = jnp.where(t <= 1, p2_ref[...], pltpu.roll(u, 2, 0))
    mix_ref[...] = (b_ref[...].astype(F32) * _conv_taps(u, prev1, prev2, w_ref)).astype(BF16)
    u_ref[...] = u


def _conv_short(z, conv_w, state0, batch, seq):
    m = batch * seq
    zeros = jnp.zeros((batch, seq - 1, D_MODEL), F32)
    p1 = jnp.concatenate([state0[:, 1:2], zeros], axis=1).reshape(m, D_MODEL)
    p2 = jnp.concatenate([state0, zeros[:, 1:]], axis=1).reshape(m, D_MODEL)
    col = lambda j: pl.BlockSpec((m, D_MODEL), lambda i, j=j: (0, j))
    full = pl.BlockSpec((m, D_MODEL), lambda i: (0, 0))
    mix, u = pl.pallas_call(
        functools.partial(_conv_short_kernel, seq=seq),
        grid=(1,),
        in_specs=[col(0), col(1), col(2), _resident((CONV_WIDTH, D_MODEL)), full, full],
        out_specs=[full, full],
        out_shape=[jax.ShapeDtypeStruct((m, D_MODEL), BF16), jax.ShapeDtypeStruct((m, D_MODEL), F32)],
        compiler_params=_cparams(("arbitrary",)),
        name="conv_short",
    )(z, z, z, conv_w, p1, p2)
    return mix, u.reshape(batch, seq, D_MODEL)[:, seq - 2:]


def _xattn_kernel(q_ref, k_hbm, v_hbm, o_ref, kbuf, vbuf, sem, *, nb, layer, n_bt):
    i = pl.program_id(0)
    t = pl.program_id(1)
    slot = i % 2

    def copies(bt, sl):
        out = []
        for b in range(nb):
            for h in range(XA_HEADS):
                out.append(pltpu.make_async_copy(k_hbm.at[layer, bt * nb + b, :, h, :], kbuf.at[sl, b, h], sem.at[0, sl]))
                out.append(pltpu.make_async_copy(v_hbm.at[layer, bt * nb + b, :, h, :], vbuf.at[sl, b, h], sem.at[1, sl]))
        return out

    @pl.when((i == 0) & (t == 0))
    def _():
        for c in copies(0, 0):
            c.start()

    @pl.when(t == 0)
    def _():
        @pl.when(i + 1 < n_bt)
        def _():
            for c in copies(i + 1, 1 - slot):
                c.start()

        for c in copies(i, slot):
            c.wait()

    scale = XA_DIM ** -0.5
    tq = q_ref.shape[1]
    rows = -(-tq // SHORT_ROWS) * SHORT_ROWS
    pairs = [(b, h) for b in range(nb) for h in range(XA_HEADS)]
    scores = []
    for b, h in pairs:
        q = q_ref[b, :, h * XA_DIM:(h + 1) * XA_DIM]
        if rows != tq:
            q = jnp.concatenate([q.astype(F32), jnp.zeros((rows - tq, XA_DIM), F32)], axis=0).astype(BF16)
        k = kbuf[slot, b, h].astype(BF16)
        scores.append(lax.dot_general(q, k, NT_DIMS, preferred_element_type=F32) * scale)
    s = jnp.concatenate(scores, axis=0)
    p = jnp.exp(s - jnp.max(s, axis=-1, keepdims=True))
    p = (p / jnp.sum(p, axis=-1, keepdims=True)).astype(BF16)
    for n, (b, h) in enumerate(pairs):
        v = vbuf[slot, b, h].astype(BF16)
        o = jnp.dot(p[n * rows:(n + 1) * rows], v, preferred_element_type=F32)
        o_ref[b, :, h * XA_DIM:(h + 1) * XA_DIM] = o[:tq].astype(BF16)


def _xattn(z, batch, seq, qcol, mem_k, mem_v, layer, nb, tq):
    n_bt = batch // nb
    plane = pltpu.VMEM((2, nb, XA_HEADS, N_MEM, XA_DIM), F32)
    if nb == 1:
        q_arr = z.reshape(1, z.shape[0], z.shape[1])
        nt = seq // tq
        q_spec = pl.BlockSpec((1, tq, D_MODEL), lambda i, t: (0, i * nt + t, qcol))
        o_spec = pl.BlockSpec((1, tq, D_MODEL), lambda i, t: (0, i * nt + t, 0))
        o_shape = (1, batch * seq, D_MODEL)
    else:
        q_arr = z.reshape(batch, seq, z.shape[1])
        q_spec = pl.BlockSpec((nb, tq, D_MODEL), lambda i, t: (i, t, qcol))
        o_spec = pl.BlockSpec((nb, tq, D_MODEL), lambda i, t: (i, t, 0))
        o_shape = (batch, seq, D_MODEL)
    return pl.pallas_call(
        functools.partial(_xattn_kernel, nb=nb, layer=layer, n_bt=n_bt),
        grid=(n_bt, seq // tq),
        in_specs=[q_spec, pl.BlockSpec(memory_space=pl.ANY), pl.BlockSpec(memory_space=pl.ANY)],
        out_specs=o_spec,
        out_shape=jax.ShapeDtypeStruct(o_shape, BF16),
        scratch_shapes=[plane, plane, pltpu.SemaphoreType.DMA((2, 2))],
        compiler_params=_cparams(("arbitrary", "arbitrary")),
        name="xattn",
    )(q_arr, mem_k, mem_v).reshape(batch * seq, D_MODEL)


def _rotary(x, cos, sin):
    half = x.shape[-1] // 2
    x1, x2 = x[:, :half], x[:, half:]
    return jnp.concatenate([x1 * cos - x2 * sin, x1 * sin + x2 * cos], axis=-1)


def _group_norm_gate(o, gate, gn):
    mu = jnp.mean(o, axis=-1, keepdims=True)
    var = jnp.mean(jnp.square(o - mu), axis=-1, keepdims=True)
    return _silu(gate) * ((o - mu) * lax.rsqrt(var + EPS) * gn)


def _ret_long_kernel(full_ref, q_ref, k_ref, v_ref, g_ref, cos_ref, sin_ref, dec_ref, inn_ref, tail_ref,
                     gn_ref, mix_ref, sfin_ref, s_acc, *, chunk):
    t = pl.program_id(1)
    heads = range(RET_HEADS)

    @pl.when(t == 0)
    def _():
        s_acc[...] = jnp.zeros_like(s_acc)

    cos, sin = cos_ref[...], sin_ref[...]
    qs = [_rotary(q_ref[0, :, h * RET_DK:(h + 1) * RET_DK].astype(F32), cos, sin) for h in heads]
    ks = [_rotary(k_ref[0, :, h * RET_DK:(h + 1) * RET_DK].astype(F32), cos, sin) * (RET_DK ** -0.5) for h in heads]
    for c in range(q_ref.shape[1] // chunk):
        sl = slice(c * chunk, (c + 1) * chunk)
        qc = [qs[h][sl].astype(BF16) for h in heads]
        kc = [ks[h][sl] for h in heads]
        vc = [v_ref[0, sl, h * RET_DV:(h + 1) * RET_DV] for h in heads]
        s_old = [s_acc[h] for h in heads]
        scores = [lax.dot_general(qc[h], kc[h].astype(BF16), NT_DIMS, preferred_element_type=F32) * dec_ref[h]
                  for h in heads]
        cross = [jnp.dot(qc[h], s_old[h].astype(BF16), preferred_element_type=F32) * inn_ref[h][:, 0:1]
                 for h in heads]
        o = [jnp.dot(scores[h].astype(BF16), vc[h], preferred_element_type=F32) + cross[h] for h in heads]
        for h in heads:
            kt = (kc[h] * tail_ref[h][:, 0:1]).astype(BF16)
            s_acc[h] = full_ref[h] * s_old[h] + lax.dot_general(kt, vc[h], TN_DIMS, preferred_element_type=F32)
        og = jnp.concatenate(o, axis=0)
        mu = jnp.mean(og, axis=-1, keepdims=True)
        var = jnp.mean(jnp.square(og - mu), axis=-1, keepdims=True)
        on = (og - mu) * lax.rsqrt(var + EPS)
        for h in heads:
            vs = slice(h * RET_DV, (h + 1) * RET_DV)
            gate = g_ref[0, sl, vs]
            mix_ref[0, sl, vs] = _silu(gate) * (on[h * chunk:(h + 1) * chunk] * gn_ref[:, vs]).astype(BF16)

    @pl.when(t == pl.num_programs(1) - 1)
    def _():
        sfin_ref[0] = s_acc[...]


def _ret_tables(pos, chunk):
    half = RET_DK // 2
    inv = np.power(ROPE_BASE, -np.arange(half, dtype=np.float64) / half)
    ang = pos.astype(np.float64)[:, None] * inv[None, :]
    log_g = np.log1p(-np.exp2(-5.0 - np.arange(RET_HEADS, dtype=np.float64)))
    idx = np.arange(chunk)
    diff = idx[:, None] - idx[None, :]
    decay = np.where(diff[None] >= 0, np.exp(np.maximum(diff, 0)[None] * log_g[:, None, None]), 0.0)
    inner = np.exp((idx + 1)[None, :] * log_g[:, None])
    tail = np.exp((chunk - 1 - idx)[None, :] * log_g[:, None])
    full = np.exp(chunk * log_g)
    return tuple(a.astype(np.float32) for a in (np.cos(ang), np.sin(ang), decay, inner, tail, full))


def _ret_long(z, batch, seq, gn, pos):
    z3 = z.reshape(1, z.shape[0], z.shape[1])
    tc = min(ROW_TILE, seq)
    nt = seq // tc
    chunk = 2 * RET_CHUNK if tc % (2 * RET_CHUNK) == 0 else RET_CHUNK
    cos, sin, decay, inner, tail, full = _ret_tables(pos, chunk)
    lane_b = lambda a: np.ascontiguousarray(np.broadcast_to(a[:, :, None], (RET_HEADS, chunk, LANES)))
    qw = RET_HEADS * RET_DK
    vw = RET_HEADS * RET_DV
    state_spec = pl.BlockSpec((1, RET_HEADS, RET_DK, RET_DV), lambda b, t: (b, 0, 0, 0))
    mix, state = pl.pallas_call(
        functools.partial(_ret_long_kernel, chunk=chunk),
        grid=(batch, seq // tc),
        in_specs=[
            pl.BlockSpec(memory_space=pltpu.SMEM),
            pl.BlockSpec((1, tc, qw), lambda b, t: (0, b * nt + t, 0)),
            pl.BlockSpec((1, tc, qw), lambda b, t: (0, b * nt + t, 1)),
            pl.BlockSpec((1, tc, vw), lambda b, t: (0, b * nt + t, 1)),
            pl.BlockSpec((1, tc, vw), lambda b, t: (0, b * nt + t, 2)),
            pl.BlockSpec((tc, RET_DK // 2), lambda b, t: (t, 0)),
            pl.BlockSpec((tc, RET_DK // 2), lambda b, t: (t, 0)),
            _resident((RET_HEADS, chunk, chunk)), _resident((RET_HEADS, chunk, LANES)),
            _resident((RET_HEADS, chunk, LANES)), _resident((1, vw)),
        ],
        out_specs=[pl.BlockSpec((1, tc, vw), lambda b, t: (0, b * nt + t, 0)), state_spec],
        out_shape=[jax.ShapeDtypeStruct((1, batch * seq, vw), BF16),
                   jax.ShapeDtypeStruct((batch, RET_HEADS, RET_DK, RET_DV), F32)],
        scratch_shapes=[pltpu.VMEM((RET_HEADS, RET_DK, RET_DV), F32)],
        compiler_params=_cparams(("arbitrary", "arbitrary")),
        name="ret_long",
    )(full, z3, z3, z3, z3, cos, sin, decay, lane_b(inner), lane_b(tail), gn.reshape(1, -1))
    return mix.reshape(batch * seq, vw), state


def _ret_short_kernel(full_ref, q_ref, k_ref, v_ref, g_ref, cos_ref, sin_ref, dec_ref, inn_ref, tail_ref,
                      gn_ref, s0_ref, mix_ref, snew_ref, *, nb, seq):
    def pad_rows(a):
        return jnp.concatenate([a, jnp.zeros((SHORT_ROWS - seq, a.shape[1]), F32)], axis=0)

    cos, sin = cos_ref[...], sin_ref[...]
    for b in range(nb):
        for h in range(RET_HEADS):
            qs = slice(h * RET_DK, (h + 1) * RET_DK)
            vs = slice(h * RET_DV, (h + 1) * RET_DV)
            q = pad_rows(_rotary(q_ref[b, :, qs].astype(F32), cos, sin)).astype(BF16)
            k = pad_rows(_rotary(k_ref[b, :, qs].astype(F32), cos, sin) * (RET_DK ** -0.5))
            v = pad_rows(v_ref[b, :, vs].astype(F32)).astype(BF16)
            s_old = s0_ref[b, h]
            scores = lax.dot_general(q, k.astype(BF16), NT_DIMS, preferred_element_type=F32) * dec_ref[h]
            o = jnp.dot(scores.astype(BF16), v, preferred_element_type=F32)
            o = o + jnp.dot(q, s_old.astype(BF16), preferred_element_type=F32) * inn_ref[h]
            snew_ref[b, h] = full_ref[h] * s_old + lax.dot_general((k * tail_ref[h]).astype(BF16), v, TN_DIMS,
                                                                   preferred_element_type=F32)
            mix_ref[b, :, vs] = _group_norm_gate(o[:seq], g_ref[b, :, vs].astype(F32), gn_ref[:, vs]).astype(BF16)


def _ret_short(z, batch, seq, state0, gn, pos, nb=4):
    z3 = z.reshape(batch, seq, z.shape[1])
    cos, sin, decay, inner, tail, full = _ret_tables(pos, seq)
    qw = RET_HEADS * RET_DK
    vw = RET_HEADS * RET_DV
    small = lambda a: _resident(a.shape)
    rp = SHORT_ROWS - seq
    decay = np.pad(decay, ((0, 0), (0, rp), (0, rp)))
    inner3 = np.pad(inner, ((0, 0), (0, rp)))[:, :, None]
    tail3 = np.pad(tail, ((0, 0), (0, rp)))[:, :, None]
    mix, state = pl.pallas_call(
        functools.partial(_ret_short_kernel, nb=nb, seq=seq),
        grid=(batch // nb,),
        in_specs=[
            pl.BlockSpec(memory_space=pltpu.SMEM),
            pl.BlockSpec((nb, seq, qw), lambda i: (i, 0, 0)),
            pl.BlockSpec((nb, seq, qw), lambda i: (i, 0, 1)),
            pl.BlockSpec((nb, seq, vw), lambda i: (i, 0, 1)),
            pl.BlockSpec((nb, seq, vw), lambda i: (i, 0, 2)),
            small(cos), small(sin), small(decay), small(inner3), small(tail3),
            _resident((1, vw)),
            pl.BlockSpec((nb, RET_HEADS, RET_DK, RET_DV), lambda i: (i, 0, 0, 0)),
        ],
        out_specs=[
            pl.BlockSpec((nb, seq, vw), lambda i: (i, 0, 0)),
            pl.BlockSpec((nb, RET_HEADS, RET_DK, RET_DV), lambda i: (i, 0, 0, 0)),
        ],
        out_shape=[jax.ShapeDtypeStruct((batch, seq, vw), BF16),
                   jax.ShapeDtypeStruct((batch, RET_HEADS, RET_DK, RET_DV), F32)],
        compiler_params=_cparams(("arbitrary",)),
        name="ret_short",
    )(full, z3, z3, z3, z3, cos, sin, decay, inner3, tail3, gn.reshape(1, -1), state0)
    return mix.reshape(batch * seq, vw), state


def _route(logits_t):
    n = logits_t.shape[1]
    epg = EXPERTS_PER_GROUP
    row = lax.broadcasted_iota(I32, (epg, n), 0).astype(F32)
    neg = jnp.float32(-1e30)
    big = jnp.float32(epg)
    is_g = row < N_GROUPS
    lg = jnp.where(is_g, logits_t[0:epg], neg)
    gmax = jnp.max(lg, axis=0, keepdims=True)
    gidx = jnp.min(jnp.where(lg == gmax, row, big), axis=0, keepdims=True)
    pg_sel = 1.0 / jnp.sum(jnp.where(is_g, jnp.exp(lg - gmax), 0.0), axis=0, keepdims=True)
    sel = logits_t[epg:2 * epg]
    for g in range(1, N_GROUPS):
        sel = jnp.where(gidx == g, logits_t[epg * (g + 1):epg * (g + 2)], sel)
    e = jnp.exp(sel - jnp.max(sel, axis=0, keepdims=True))
    pe = e / jnp.sum(e, axis=0, keepdims=True)
    t1 = jnp.max(pe, axis=0, keepdims=True)
    i1 = jnp.min(jnp.where(pe == t1, row, big), axis=0, keepdims=True)
    pe2 = jnp.where(row == i1, -1.0, pe)
    t2 = jnp.max(pe2, axis=0, keepdims=True)
    i2 = jnp.min(jnp.where(pe2 == t2, row, big), axis=0, keepdims=True)
    den = t1 + t2
    gates = jnp.where(row == i1, t1 / den * pg_sel, 0.0) + jnp.where(row == i2, t2 / den * pg_sel, 0.0)
    ids = jnp.where(row == 0, gidx, 0.0)
    block = jnp.concatenate([gates, ids, jnp.zeros((LANES - 2 * epg, n), F32)], axis=0)
    return block.T


def _outproj_pair_kernel(xl_ref, mixl_ref, xal_ref, xs_ref, mixs_ref, xas_ref, wa_ref, wb_ref, g_ref, wr_ref,
                         o_ref, r_ref, *, sub, n_long):
    i = pl.program_id(0)

    @pl.when(i < n_long)
    def _():
        _outproj_kernel(xl_ref, mixl_ref, xal_ref, wa_ref, wb_ref, g_ref, wr_ref, o_ref, r_ref, sub=sub)

    @pl.when(i >= n_long)
    def _():
        _outproj_kernel(xs_ref, mixs_ref, xas_ref, wa_ref, wb_ref, g_ref, wr_ref, o_ref, r_ref, sub=sub)


def _outproj_kernel(x_ref, mix_ref, xa_ref, wa_ref, wb_ref, g_ref, wr_ref, o_ref, r_ref, *, sub):
    acc = jnp.dot(mix_ref[...], wa_ref[...], preferred_element_type=F32)
    acc = acc + jnp.dot(xa_ref[...], wb_ref[...], preferred_element_type=F32)
    o_ref[:, :D_MODEL] = x_ref[...] + acc
    logits = []
    for s in range(x_ref.shape[0] // sub):
        rows = slice(s * sub, (s + 1) * sub)
        h2 = _rms(o_ref[rows, :D_MODEL], g_ref[...])
        hi = h2.astype(BF16)
        lo = (h2 - hi.astype(F32)).astype(BF16)
        p_hi = lax.dot_general(wr_ref[...], hi, NT_DIMS, preferred_element_type=F32)
        p_lo = lax.dot_general(wr_ref[0:ROUTER_ROWS, :], lo, NT_DIMS, preferred_element_type=F32)
        logits.append((p_hi[:ROUTER_ROWS] + p_lo) + p_hi[ROUTER_ROWS:])
    route = _route(jnp.concatenate(logits, axis=1))
    o_ref[:, D_MODEL:] = route
    r_ref[...] = route


def _router_weights(w_group, w_router):
    epg = EXPERTS_PER_GROUP
    zeros = lambda r: jnp.zeros((DEPTH, r, D_MODEL), F32)
    wt = jnp.concatenate([jnp.swapaxes(w_group, 1, 2), zeros(epg - N_GROUPS), jnp.swapaxes(w_router, 1, 2),
                          zeros(ROUTER_ROWS - epg - N_EXPERTS)], axis=1)
    hi = wt.astype(BF16)
    return jnp.concatenate([hi, (wt - hi.astype(F32)).astype(BF16)], axis=1)


def _outproj_router(x, x_off, mix, xa, w_out, g_ffn, w_router2):
    m, cm = mix.shape
    tm = min(ROW_TILE, m)
    xb = x_off // tm
    row = lambda w, off=0: pl.BlockSpec((tm, w), lambda i: (i + off, 0))
    return pl.pallas_call(
        functools.partial(_outproj_kernel, sub=min(ROUTER_SUB, tm)),
        grid=(m // tm,),
        in_specs=[row(D_MODEL, xb), row(cm), row(D_MODEL), _resident((cm, D_MODEL)), _resident((D_MODEL, D_MODEL)),
                  _resident((1, D_MODEL)), _resident((2 * ROUTER_ROWS, D_MODEL))],
        out_specs=[row(AUG_W), row(LANES)],
        out_shape=[jax.ShapeDtypeStruct((m, AUG_W), F32), jax.ShapeDtypeStruct((m, LANES), F32)],
        compiler_params=_cparams(("arbitrary",)),
        name="outproj_router",
    )(x, mix, xa, w_out[:cm], w_out[cm:], g_ffn.reshape(1, D_MODEL), w_router2)


def _outproj_router_pair(long, short, w_out, g_ffn, w_router2):
    tm = ROW_TILE
    assert long[0].shape[0] % tm == 0 and short[0].shape[0] % tm == 0
    cm = long[1].shape[1]
    nl, ns = long[0].shape[0] // tm, short[0].shape[0] // tm
    first = lambda w: pl.BlockSpec((tm, w), lambda i: (jnp.minimum(i, nl - 1), 0))
    second = lambda w: pl.BlockSpec((tm, w), lambda i: (jnp.maximum(i - nl, 0), 0))
    row = lambda w: pl.BlockSpec((tm, w), lambda i: (i, 0))
    m = (nl + ns) * tm
    return pl.pallas_call(
        functools.partial(_outproj_pair_kernel, sub=ROUTER_SUB, n_long=nl),
        grid=(nl + ns,),
        in_specs=[first(D_MODEL), first(cm), first(D_MODEL), second(D_MODEL), second(cm), second(D_MODEL),
                  _resident((cm, D_MODEL)), _resident((D_MODEL, D_MODEL)),
                  _resident((1, D_MODEL)), _resident((2 * ROUTER_ROWS, D_MODEL))],
        out_specs=[row(AUG_W), row(LANES)],
        out_shape=[jax.ShapeDtypeStruct((m, AUG_W), F32), jax.ShapeDtypeStruct((m, LANES), F32)],
        compiler_params=_cparams(("arbitrary",)),
        name="outproj_router_pair",
    )(*long, *short, w_out[:cm], w_out[cm:], g_ffn.reshape(1, D_MODEL), w_router2)


STEP_VALID, STEP_FIRST, STEP_LAST, STEP_NEWGROUP = 1, 2, 4, 8


def _moe_plan(gid, tm):
    m = gid.shape[0]
    ntile = m // tm
    nstep = ntile + N_GROUPS - 1
    counts = jnp.sum((gid[:, None] == jnp.arange(N_GROUPS, dtype=I32)[None, :]).astype(I32), axis=0)
    gend = jnp.cumsum(counts)
    gstart = gend - counts
    row_tok = jnp.argsort(gid, stable=True).astype(I32)
    inner_ok = (counts[1:] > 0) & (gstart[1:] % tm != 0)
    cuts = jnp.concatenate([jnp.arange(ntile, dtype=I32) * tm, jnp.where(inner_ok, gstart[1:], m)])
    cut_ok = jnp.concatenate([jnp.ones((ntile,), bool), inner_ok])
    rank = jnp.sum(((cuts[None, :] < cuts[:, None]) & cut_ok[None, :]).astype(I32), axis=1)
    sel = jnp.where(cut_ok, rank, nstep)[None, :] == jnp.arange(nstep, dtype=I32)[:, None]
    valid = jnp.any(sel, axis=1)
    start = jnp.sum(jnp.where(sel, cuts[None, :], 0), axis=1)
    start = jnp.where(valid, start, jnp.max(start))
    st_tile = start // tm
    st_gid = jnp.sum((gend[None, :] <= start[:, None]).astype(I32), axis=1)
    first = valid & (start % tm == 0)
    next_first = jnp.concatenate([first[1:], jnp.ones((1,), bool)])
    next_valid = jnp.concatenate([valid[1:], jnp.zeros((1,), bool)])
    last = valid & (next_first | ~next_valid)
    prev_gid = jnp.concatenate([jnp.full((1,), -1, I32), st_gid[:-1]])
    newgroup = valid & (st_gid != prev_gid)
    flags = valid * STEP_VALID + first * STEP_FIRST + last * STEP_LAST + newgroup * STEP_NEWGROUP
    return st_tile, st_gid, flags.astype(I32), row_tok


def _moe_kernel(tile_ref, gid_ref, flag_ref, tok_ref, xaug_hbm, g_ref, fin_ref, wgu_hbm, wd_hbm, out_hbm,
                xbuf, ybuf, wgu_bf, wd_bf, wgu_stage, wd_stage, gsem, ssem, wsem, *, tm, ntile, layer, final_norm):
    s = pl.program_id(0)
    tile = tile_ref[s]
    grp = gid_ref[s]
    flags = flag_ref[s]
    valid = (flags & STEP_VALID) > 0
    first = (flags & STEP_FIRST) > 0
    last = (flags & STEP_LAST) > 0
    newgroup = (flags & STEP_NEWGROUP) > 0
    slot = tile % 2

    def weight_copies(j):
        e = grp * EXPERTS_PER_GROUP + j
        return (pltpu.make_async_copy(wgu_hbm.at[layer, e], wgu_stage.at[j % 2], wsem.at[j % 2]),
                pltpu.make_async_copy(wd_hbm.at[layer, e], wd_stage.at[j % 2], wsem.at[j % 2]))

    def gather_start(t, sl):
        for r in range(tm):
            tok = tok_ref[t * tm + r]
            pltpu.make_async_copy(xaug_hbm.at[pl.ds(tok, 1)], xbuf.at[sl, pl.ds(r, 1)],
                                  gsem.at[sl]).start(priority=r % 2)

    def scatter_start(t, sl):
        for r in range(tm):
            tok = tok_ref[t * tm + r]
            pltpu.make_async_copy(ybuf.at[sl, pl.ds(r, 1)], out_hbm.at[pl.ds(tok, 1)],
                                  ssem.at[sl]).start(priority=r % 2)

    def gather_wait(sl):
        pltpu.make_async_copy(xaug_hbm.at[pl.ds(0, tm)], xbuf.at[sl], gsem.at[sl]).wait()

    def scatter_wait(sl):
        pltpu.make_async_copy(ybuf.at[sl], out_hbm.at[pl.ds(0, tm)], ssem.at[sl]).wait()

    @pl.when(s == 0)
    def _():
        gather_start(0, 0)

    @pl.when(first & (tile + 1 < ntile))
    def _():
        gather_start(tile + 1, 1 - slot)

    @pl.when(first)
    def _():
        gather_wait(slot)

        @pl.when(tile >= 2)
        def _():
            scatter_wait(slot)

        ybuf[slot] = xbuf[slot, :, 0:D_MODEL]

    def experts(stream_weights):
        xn = xbuf[slot, :, 0:D_MODEL]
        gl = xbuf[slot, :, GATE_LANE0:AUG_W]
        mine = gl[:, GID_LANE:GID_LANE + 1] == grp.astype(F32)
        h2 = _rms(xn, g_ref[...]).astype(BF16)
        acc = jnp.zeros((tm, D_MODEL), F32)
        if stream_weights:
            for c in weight_copies(0):
                c.start()
        for j in range(EXPERTS_PER_GROUP):
            if stream_weights:
                if j + 1 < EXPERTS_PER_GROUP:
                    for c in weight_copies(j + 1):
                        c.start()
                for c in weight_copies(j):
                    c.wait()
                wgu_bf[j] = wgu_stage[j % 2].astype(BF16)
                wd_bf[j] = wd_stage[j % 2].astype(BF16)
            gu = jnp.dot(h2, wgu_bf[j], preferred_element_type=F32)
            gate = jnp.where(mine, gl[:, j:j + 1], 0.0)
            act = _silu(gu[:, :EXPERT_FF]) * gu[:, EXPERT_FF:] * gate
            acc = acc + jnp.dot(act.astype(BF16), wd_bf[j], preferred_element_type=F32)
        ybuf[slot] = ybuf[slot] + acc

    @pl.when(newgroup)
    def _():
        experts(True)

    @pl.when(valid & jnp.logical_not(newgroup))
    def _():
        experts(False)

    @pl.when(last)
    def _():
        if final_norm:
            ybuf[slot] = _rms(ybuf[slot], fin_ref[...])
        scatter_start(tile, slot)

        @pl.when(tile == ntile - 1)
        def _():
            if ntile >= 2:
                scatter_wait(1 - slot)
            scatter_wait(slot)


def _moe(xaug, route, g_ffn, final_g, w_gu, w_down, layer, final_norm):
    m = xaug.shape[0]
    tm = min(MOE_TM, m)
    gid = route[:, GID_LANE].astype(I32)
    st_tile, st_gid, flags, row_tok = _moe_plan(gid, tm)
    epg = EXPERTS_PER_GROUP
    grid_spec = pltpu.PrefetchScalarGridSpec(
        num_scalar_prefetch=4,
        grid=(st_tile.shape[0],),
        in_specs=[
            pl.BlockSpec(memory_space=pl.ANY),
            pl.BlockSpec((1, D_MODEL), lambda s, *_: (0, 0)),
            pl.BlockSpec((1, D_MODEL), lambda s, *_: (0, 0)),
            pl.BlockSpec(memory_space=pl.ANY),
            pl.BlockSpec(memory_space=pl.ANY),
        ],
        out_specs=pl.BlockSpec(memory_space=pl.ANY),
        scratch_shapes=[
            pltpu.VMEM((2, tm, AUG_W), F32),
            pltpu.VMEM((2, tm, D_MODEL), F32),
            pltpu.VMEM((epg, D_MODEL, 2 * EXPERT_FF), BF16),
            pltpu.VMEM((epg, EXPERT_FF, D_MODEL), BF16),
            pltpu.VMEM((2, D_MODEL, 2 * EXPERT_FF), F32),
            pltpu.VMEM((2, EXPERT_FF, D_MODEL), F32),
            pltpu.SemaphoreType.DMA((2,)),
            pltpu.SemaphoreType.DMA((2,)),
            pltpu.SemaphoreType.DMA((2,)),
        ],
    )
    return pl.pallas_call(
        functools.partial(_moe_kernel, tm=tm, ntile=m // tm, layer=layer, final_norm=final_norm),
        grid_spec=grid_spec,
        out_shape=jax.ShapeDtypeStruct((m, D_MODEL), F32),
        compiler_params=_cparams(("arbitrary",)),
        name="moe",
    )(st_tile, st_gid, flags, row_tok, xaug, g_ffn.reshape(1, D_MODEL), final_g.reshape(1, D_MODEL), w_gu, w_down)


def _forward(x_long, x_short, pos_long, pos_short, conv_state, ret_state, mem_long, mem_short, wts):
    assert DEPTH == 2
    (bl, tl), (bs, ts) = x_long.shape[:2], x_short.shape[:2]
    ml, ms = bl * tl, bs * ts
    xl, xs = x_long.reshape(ml, D_MODEL), x_short.reshape(ms, D_MODEL)
    tq = min(XATTN_ROWS, tl)
    moe = lambda rows, route, i: _moe(rows, route, wts["norm_ffn"][i], wts["final_norm"], wts["moe_w_gate_up"],
                                      wts["moe_w_down"], i, final_norm=(i == DEPTH - 1))

    mix_l, q_l, conv_l = _inproj_conv(xl, wts["norm_mix"][0], wts["conv_w_in"], wts["conv_w"], bl, tl)
    zs = _inproj(xs, wts["norm_mix"][0], wts["conv_w_in"])
    mix_s, conv_s = _conv_short(zs, wts["conv_w"], conv_state, bs, ts)
    xa_l = _xattn(q_l, bl, tl, 0, *mem_long, 0, 1, tq)
    xa_s = _xattn(zs, bs, ts, 3, *mem_short, 0, SHORT_NB, ts)
    routed = _outproj_router_pair((xl, mix_l, xa_l), (xs, mix_s, xa_s), wts["conv_w_out"], wts["norm_ffn"][0],
                                  wts["w_router"][0])
    x1 = moe(*routed, 0)

    z = _inproj(x1, wts["norm_mix"][1], wts["ret_w_in"])
    zs = z[ml:]
    mix_l, ret_l = _ret_long(z, bl, tl, wts["ret_gn"], pos_long)
    mix_s, ret_s = _ret_short(zs, bs, ts, ret_state, wts["ret_gn"], pos_short)
    xa_l = _xattn(z, bl, tl, 6, *mem_long, 1, 1, tq)
    xa_s = _xattn(zs, bs, ts, 6, *mem_short, 1, SHORT_NB, ts)
    y_l = moe(*_outproj_router(x1, 0, mix_l, xa_l, wts["ret_w_out"], wts["norm_ffn"][1], wts["w_router"][1]), 1)
    y_s = moe(*_outproj_router(x1, ml, mix_s, xa_s, wts["ret_w_out"], wts["norm_ffn"][1], wts["w_router"][1]), 1)
    return (y_l.reshape(bl, tl, D_MODEL), y_s.reshape(bs, ts, D_MODEL), conv_l, conv_s, ret_l, ret_s)


def kernel(x_prompt, x_sample, state_conv, state_ret, cache_mem_k, cache_mem_v, mem_prompt, norm_mix, norm_ffn,
           mem_norm, w_mem_kv, conv_w_in, conv_w, conv_w_out, ret_w_in, ret_gn, ret_w_out, moe_w_group,
           moe_w_router, moe_w_gate_up, moe_w_down, final_norm):
    batch, seq = x_prompt.shape[0], x_prompt.shape[1]
    dec_seq = x_sample.shape[1]
    w_router = _router_weights(moe_w_group, moe_w_router)
    wts = dict(
        norm_mix=norm_mix, norm_ffn=norm_ffn, final_norm=final_norm, w_router=w_router,
        conv_w_in=conv_w_in[0].astype(BF16), conv_w=conv_w[0], conv_w_out=conv_w_out[0].astype(BF16),
        ret_w_in=ret_w_in[0].astype(BF16), ret_gn=ret_gn[0], ret_w_out=ret_w_out[0].astype(BF16),
        moe_w_gate_up=moe_w_gate_up, moe_w_down=moe_w_down,
    )
    mem_k_prompt, mem_v_prompt = _mem_kv(mem_prompt.reshape(batch * N_MEM, D_MODEL), mem_norm, w_mem_kv.astype(BF16))

    y_prompt, y_sample, conv_p, conv_s, ret_p, ret_s = _forward(
        x_prompt, x_sample, np.arange(seq), PAST_LEN + np.arange(dec_seq),
        state_conv[0], state_ret[0], (mem_k_prompt, mem_v_prompt), (cache_mem_k, cache_mem_v), wts)
    return (y_prompt, y_sample, conv_p[None], conv_s[None], ret_p[None], ret_s[None], mem_k_prompt, mem_v_prompt)
```

```python
import functools

import jax
import jax.numpy as jnp
import numpy as np
from jax import lax
from jax.experimental import pallas as pl
from jax.experimental.pallas import tpu as pltpu

F32 = jnp.float32
BF16 = jnp.bfloat16
I32 = jnp.int32

D_MODEL = 1024
DEPTH = 2
CONV_WIDTH = 3
RET_HEADS = 4
RET_DK = 256
RET_DV = 512
RET_CHUNK = 128
ROPE_BASE = 10000.0
N_MEM = 256
XA_HEADS = 4
XA_DIM = 256
N_GROUPS = 4
EXPERTS_PER_GROUP = 8
N_EXPERTS = N_GROUPS * EXPERTS_PER_GROUP
EXPERT_FF = 256
EPS = 1e-6

LANES = 128
AUG_W = D_MODEL + LANES
GATE_LANE0 = D_MODEL
GID_LANE = EXPERTS_PER_GROUP
ROUTER_ROWS = 64
VMEM_LIMIT = 56 * 1024 * 1024
ROW_TILE = 512
COL_CHUNK = 512
ROUTER_SUB = 128
XATTN_ROWS = 1024
MOE_TM = 512
SHORT_ROWS = 8
SHORT_NB = 8
PAST_LEN = 16384
NT_DIMS = (((1,), (1,)), ((), ()))
TN_DIMS = (((0,), (0,)), ((), ()))


def _cparams(sem):
    return pltpu.CompilerParams(dimension_semantics=sem, vmem_limit_bytes=VMEM_LIMIT)


def _rms(x, g):
    return x * lax.rsqrt(jnp.mean(x * x, axis=-1, keepdims=True) + EPS) * g


def _silu(x):
    return x * jax.nn.sigmoid(x)


def _resident(shape):
    nd = len(shape)
    return pl.BlockSpec(shape, lambda *_: (0,) * nd, pipeline_mode=pl.Buffered(1))


def _memkv_kernel(x_ref, g_ref, w_ref, k_ref, v_ref, *, nb):
    h = _rms(x_ref[...], g_ref[...]).astype(BF16)
    kv = jnp.dot(h, w_ref[...], preferred_element_type=F32)
    for b in range(nb):
        rows = slice(b * N_MEM, (b + 1) * N_MEM)
        for hd in range(XA_HEADS):
            k_ref[b, :, hd, :] = kv[rows, hd * XA_DIM:(hd + 1) * XA_DIM]
            v_ref[b, :, hd, :] = kv[rows, D_MODEL + hd * XA_DIM:D_MODEL + (hd + 1) * XA_DIM]


def _mem_kv(mem2d, mem_norm, w_kv):
    batch = mem2d.shape[0] // N_MEM
    nb = min(2, batch)
    tm = nb * N_MEM
    out = jax.ShapeDtypeStruct((DEPTH, batch, N_MEM, XA_HEADS, XA_DIM), F32)
    out_spec = pl.BlockSpec((None, nb, N_MEM, XA_HEADS, XA_DIM), lambda l, j: (l, j, 0, 0, 0))
    return pl.pallas_call(
        functools.partial(_memkv_kernel, nb=nb),
        grid=(DEPTH, batch // nb),
        in_specs=[
            pl.BlockSpec((tm, D_MODEL), lambda l, j: (j, 0)),
            pl.BlockSpec((None, 1, D_MODEL), lambda l, j: (l, 0, 0)),
            pl.BlockSpec((None, D_MODEL, 2 * D_MODEL), lambda l, j: (l, 0, 0)),
        ],
        out_specs=[out_spec, out_spec],
        out_shape=[out, out],
        compiler_params=_cparams(("arbitrary", "arbitrary")),
        name="mem_kv",
    )(mem2d, mem_norm.reshape(DEPTH, 1, D_MODEL), w_kv)


def _inproj_kernel(x_ref, g_ref, w_ref, o_ref, *, tn):
    h = _rms(x_ref[...], g_ref[...]).astype(BF16)
    for j in range(o_ref.shape[1] // tn):
        sl = slice(j * tn, (j + 1) * tn)
        o_ref[:, sl] = jnp.dot(h, w_ref[:, sl], preferred_element_type=F32).astype(BF16)


def _inproj(x, g, w):
    m, n = x.shape[0], w.shape[1]
    tm = min(ROW_TILE, m)
    return pl.pallas_call(
        functools.partial(_inproj_kernel, tn=COL_CHUNK),
        grid=(m // tm,),
        in_specs=[
            pl.BlockSpec((tm, D_MODEL), lambda i: (i, 0)),
            _resident((1, D_MODEL)),
            _resident((D_MODEL, n)),
        ],
        out_specs=pl.BlockSpec((tm, n), lambda i: (i, 0)),
        out_shape=jax.ShapeDtypeStruct((m, n), BF16),
        compiler_params=_cparams(("arbitrary",)),
        name="inproj",
    )(x, g.reshape(1, D_MODEL), w)


def _conv_taps(u, prev1, prev2, w_ref):
    return (w_ref[0:1, :] * prev2 + w_ref[1:2, :] * prev1) + w_ref[2:3, :] * u


def _inproj_conv_kernel(x_ref, g_ref, w_ref, cw_ref, mix_ref, q_ref, st_ref, carry, *, tiles_per_seq):
    i = pl.program_id(0)
    tt = x_ref.shape[0]
    col = lambda j: slice(j * D_MODEL, (j + 1) * D_MODEL)
    h = _rms(x_ref[...], g_ref[...]).astype(BF16)
    proj = lambda j: jnp.dot(h, w_ref[:, col(j)], preferred_element_type=F32)

    @pl.when(i % tiles_per_seq == 0)
    def _():
        carry[...] = jnp.zeros_like(carry)

    q_ref[...] = proj(3).astype(BF16)
    u = proj(1) * proj(2)
    pm2 = carry[6:7, :]
    pm1 = carry[7:8, :]
    row = lax.broadcasted_iota(I32, u.shape, 0)
    prev1 = jnp.where(row == 0, pm1, pltpu.roll(u, 1, 0))
    prev2 = jnp.where(row == 0, pm2, jnp.where(row == 1, pm1, pltpu.roll(u, 2, 0)))
    mix_ref[...] = (proj(0) * _conv_taps(u, prev1, prev2, cw_ref)).astype(BF16)
    carry[...] = u[tt - 8:tt, :]
    st_ref[0] = carry[6:8, :]


def _inproj_conv(x, g, w, conv_w, batch, seq):
    tt = min(ROW_TILE, seq)
    tps = seq // tt
    row = pl.BlockSpec((tt, D_MODEL), lambda i: (i, 0))
    out = jax.ShapeDtypeStruct((batch * seq, D_MODEL), BF16)
    return pl.pallas_call(
        functools.partial(_inproj_conv_kernel, tiles_per_seq=tps),
        grid=(batch * tps,),
        in_specs=[row, _resident((1, D_MODEL)), _resident(w.shape), _resident((CONV_WIDTH, D_MODEL))],
        out_specs=[row, row, pl.BlockSpec((1, 2, D_MODEL), lambda i: (i // tps, 0, 0))],
        out_shape=[out, out, jax.ShapeDtypeStruct((batch, 2, D_MODEL), F32)],
        scratch_shapes=[pltpu.VMEM((8, D_MODEL), F32)],
        compiler_params=_cparams(("arbitrary",)),
        name="inproj_conv",
    )(x, g.reshape(1, D_MODEL), w, conv_w)


def _conv_short_kernel(b_ref, c_ref, v_ref, w_ref, p1_ref, p2_ref, mix_ref, u_ref, *, seq):
    u = c_ref[...].astype(F32) * v_ref[...].astype(F32)
    t = lax.broadcasted_iota(I32, u.shape, 0) % seq
    prev1 = jnp.where(t == 0, p1_ref[...], pltpu.roll(u, 1, 0))
    prev2 = jnp.where(t <= 1, p2_ref[...], pltpu.roll(u, 2, 0))
    mix_ref[...] = (b_ref[...].astype(F32) * _conv_taps(u, prev1, prev2, w_ref)).astype(BF16)
    u_ref[...] = u


def _conv_short(z, conv_w, state0, batch, seq):
    m = batch * seq
    zeros = jnp.zeros((batch, seq - 1, D_MODEL), F32)
    p1 = jnp.concatenate([state0[:, 1:2], zeros], axis=1).reshape(m, D_MODEL)
    p2 = jnp.concatenate([state0, zeros[:, 1:]], axis=1).reshape(m, D_MODEL)
    col = lambda j: pl.BlockSpec((m, D_MODEL), lambda i, j=j: (0, j))
    full = pl.BlockSpec((m, D_MODEL), lambda i: (0, 0))
    mix, u = pl.pallas_call(
        functools.partial(_conv_short_kernel, seq=seq),
        grid=(1,),
        in_specs=[col(0), col(1), col(2), _resident((CONV_WIDTH, D_MODEL)), full, full],
        out_specs=[full, full],
        out_shape=[jax.ShapeDtypeStruct((m, D_MODEL), BF16), jax.ShapeDtypeStruct((m, D_MODEL), F32)],
        compiler_params=_cparams(("arbitrary",)),
        name="conv_short",
    )(z, z, z, conv_w, p1, p2)
    return mix, u.reshape(batch, seq, D_MODEL)[:, seq - 2:]


def _xattn_kernel(q_ref, k_hbm, v_hbm, o_ref, kbuf, vbuf, sem, *, nb, layer, n_bt):
    i = pl.program_id(0)
    t = pl.program_id(1)
    slot = i % 2

    def copies(bt, sl):
        out = []
        for b in range(nb):
            for h in range(XA_HEADS):
                out.append(pltpu.make_async_copy(k_hbm.at[layer, bt * nb + b, :, h, :], kbuf.at[sl, b, h], sem.at[0, sl]))
                out.append(pltpu.make_async_copy(v_hbm.at[layer, bt * nb + b, :, h, :], vbuf.at[sl, b, h], sem.at[1, sl]))
        return out

    @pl.when((i == 0) & (t == 0))
    def _():
        for c in copies(0, 0):
            c.start()

    @pl.when(t == 0)
    def _():
        @pl.when(i + 1 < n_bt)
        def _():
            for c in copies(i + 1, 1 - slot):
                c.start()

        for c in copies(i, slot):
            c.wait()

    scale = XA_DIM ** -0.5
    tq = q_ref.shape[1]
    rows = -(-tq // SHORT_ROWS) * SHORT_ROWS
    pairs = [(b, h) for b in range(nb) for h in range(XA_HEADS)]
    scores = []
    for b, h in pairs:
        q = q_ref[b, :, h * XA_DIM:(h + 1) * XA_DIM]
        if rows != tq:
            q = jnp.concatenate([q.astype(F32), jnp.zeros((rows - tq, XA_DIM), F32)], axis=0).astype(BF16)
        k = kbuf[slot, b, h].astype(BF16)
        scores.append(lax.dot_general(q, k, NT_DIMS, preferred_element_type=F32) * scale)
    s = jnp.concatenate(scores, axis=0)
    p = jnp.exp(s - jnp.max(s, axis=-1, keepdims=True))
    p = (p / jnp.sum(p, axis=-1, keepdims=True)).astype(BF16)
    for n, (b, h) in enumerate(pairs):
        v = vbuf[slot, b, h].astype(BF16)
        o = jnp.dot(p[n * rows:(n + 1) * rows], v, preferred_element_type=F32)
        o_ref[b, :, h * XA_DIM:(h + 1) * XA_DIM] = o[:tq].astype(BF16)


def _xattn(z, batch, seq, qcol, mem_k, mem_v, layer, nb, tq):
    n_bt = batch // nb
    plane = pltpu.VMEM((2, nb, XA_HEADS, N_MEM, XA_DIM), F32)
    if nb == 1:
        q_arr = z.reshape(1, z.shape[0], z.shape[1])
        nt = seq // tq
        q_spec = pl.BlockSpec((1, tq, D_MODEL), lambda i, t: (0, i * nt + t, qcol))
        o_spec = pl.BlockSpec((1, tq, D_MODEL), lambda i, t: (0, i * nt + t, 0))
        o_shape = (1, batch * seq, D_MODEL)
    else:
        q_arr = z.reshape(batch, seq, z.shape[1])
        q_spec = pl.BlockSpec((nb, tq, D_MODEL), lambda i, t: (i, t, qcol))
        o_spec = pl.BlockSpec((nb, tq, D_MODEL), lambda i, t: (i, t, 0))
        o_shape = (batch, seq, D_MODEL)
    return pl.pallas_call(
        functools.partial(_xattn_kernel, nb=nb, layer=layer, n_bt=n_bt),
        grid=(n_bt, seq // tq),
        in_specs=[q_spec, pl.BlockSpec(memory_space=pl.ANY), pl.BlockSpec(memory_space=pl.ANY)],
        out_specs=o_spec,
        out_shape=jax.ShapeDtypeStruct(o_shape, BF16),
        scratch_shapes=[plane, plane, pltpu.SemaphoreType.DMA((2, 2))],
        compiler_params=_cparams(("arbitrary", "arbitrary")),
        name="xattn",
    )(q_arr, mem_k, mem_v).reshape(batch * seq, D_MODEL)


def _rotary(x, cos, sin):
    half = x.shape[-1] // 2
    x1, x2 = x[:, :half], x[:, half:]
    return jnp.concatenate([x1 * cos - x2 * sin, x1 * sin + x2 * cos], axis=-1)


def _group_norm_gate(o, gate, gn):
    mu = jnp.mean(o, axis=-1, keepdims=True)
    var = jnp.mean(jnp.square(o - mu), axis=-1, keepdims=True)
    return _silu(gate) * ((o - mu) * lax.rsqrt(var + EPS) * gn)


def _ret_long_kernel(full_ref, q_ref, k_ref, v_ref, g_ref, cos_ref, sin_ref, dec_ref, inn_ref, tail_ref,
                     gn_ref, mix_ref, sfin_ref, s_acc, *, chunk):
    t = pl.program_id(1)
    heads = range(RET_HEADS)

    @pl.when(t == 0)
    def _():
        s_acc[...] = jnp.zeros_like(s_acc)

    cos, sin = cos_ref[...], sin_ref[...]
    qs = [_rotary(q_ref[0, :, h * RET_DK:(h + 1) * RET_DK].astype(F32), cos, sin) for h in heads]
    ks = [_rotary(k_ref[0, :, h * RET_DK:(h + 1) * RET_DK].astype(F32), cos, sin) * (RET_DK ** -0.5) for h in heads]
    for c in range(q_ref.shape[1] // chunk):
        sl = slice(c * chunk, (c + 1) * chunk)
        qc = [qs[h][sl].astype(BF16) for h in heads]
        kc = [ks[h][sl] for h in heads]
        vc = [v_ref[0, sl, h * RET_DV:(h + 1) * RET_DV] for h in heads]
        s_old = [s_acc[h] for h in heads]
        scores = [lax.dot_general(qc[h], kc[h].astype(BF16), NT_DIMS, preferred_element_type=F32) * dec_ref[h]
                  for h in heads]
        cross = [jnp.dot(qc[h], s_old[h].astype(BF16), preferred_element_type=F32) * inn_ref[h][:, 0:1]
                 for h in heads]
        o = [jnp.dot(scores[h].astype(BF16), vc[h], preferred_element_type=F32) + cross[h] for h in heads]
        for h in heads:
            kt = (kc[h] * tail_ref[h][:, 0:1]).astype(BF16)
            s_acc[h] = full_ref[h] * s_old[h] + lax.dot_general(kt, vc[h], TN_DIMS, preferred_element_type=F32)
        og = jnp.concatenate(o, axis=0)
        mu = jnp.mean(og, axis=-1, keepdims=True)
        var = jnp.mean(jnp.square(og - mu), axis=-1, keepdims=True)
        on = (og - mu) * lax.rsqrt(var + EPS)
        for h in heads:
            vs = slice(h * RET_DV, (h + 1) * RET_DV)
            gate = g_ref[0, sl, vs]
            mix_ref[0, sl, vs] = _silu(gate) * (on[h * chunk:(h + 1) * chunk] * gn_ref[:, vs]).astype(BF16)

    @pl.when(t == pl.num_programs(1) - 1)
    def _():
        sfin_ref[0] = s_acc[...]


def _ret_tables(pos, chunk):
    half = RET_DK // 2
    inv = np.power(ROPE_BASE, -np.arange(half, dtype=np.float64) / half)
    ang = pos.astype(np.float64)[:, None] * inv[None, :]
    log_g = np.log1p(-np.exp2(-5.0 - np.arange(RET_HEADS, dtype=np.float64)))
    idx = np.arange(chunk)
    diff = idx[:, None] - idx[None, :]
    decay = np.where(diff[None] >= 0, np.exp(np.maximum(diff, 0)[None] * log_g[:, None, None]), 0.0)
    inner = np.exp((idx + 1)[None, :] * log_g[:, None])
    tail = np.exp((chunk - 1 - idx)[None, :] * log_g[:, None])
    full = np.exp(chunk * log_g)
    return tuple(a.astype(np.float32) for a in (np.cos(ang), np.sin(ang), decay, inner, tail, full))


def _ret_long(z, batch, seq, gn, pos):
    z3 = z.reshape(1, z.shape[0], z.shape[1])
    tc = min(ROW_TILE, seq)
    nt = seq // tc
    chunk = 2 * RET_CHUNK if tc % (2 * RET_CHUNK) == 0 else RET_CHUNK
    cos, sin, decay, inner, tail, full = _ret_tables(pos, chunk)
    lane_b = lambda a: np.ascontiguousarray(np.broadcast_to(a[:, :, None], (RET_HEADS, chunk, LANES)))
    qw = RET_HEADS * RET_DK
    vw = RET_HEADS * RET_DV
    state_spec = pl.BlockSpec((1, RET_HEADS, RET_DK, RET_DV), lambda b, t: (b, 0, 0, 0))
    mix, state = pl.pallas_call(
        functools.partial(_ret_long_kernel, chunk=chunk),
        grid=(batch, seq // tc),
        in_specs=[
            pl.BlockSpec(memory_space=pltpu.SMEM),
            pl.BlockSpec((1, tc, qw), lambda b, t: (0, b * nt + t, 0)),
            pl.BlockSpec((1, tc, qw), lambda b, t: (0, b * nt + t, 1)),
            pl.BlockSpec((1, tc, vw), lambda b, t: (0, b * nt + t, 1)),
            pl.BlockSpec((1, tc, vw), lambda b, t: (0, b * nt + t, 2)),
            pl.BlockSpec((tc, RET_DK // 2), lambda b, t: (t, 0)),
            pl.BlockSpec((tc, RET_DK // 2), lambda b, t: (t, 0)),
            _resident((RET_HEADS, chunk, chunk)), _resident((RET_HEADS, chunk, LANES)),
            _resident((RET_HEADS, chunk, LANES)), _resident((1, vw)),
        ],
        out_specs=[pl.BlockSpec((1, tc, vw), lambda b, t: (0, b * nt + t, 0)), state_spec],
        out_shape=[jax.ShapeDtypeStruct((1, batch * seq, vw), BF16),
                   jax.ShapeDtypeStruct((batch, RET_HEADS, RET_DK, RET_DV), F32)],
        scratch_shapes=[pltpu.VMEM((RET_HEADS, RET_DK, RET_DV), F32)],
        compiler_params=_cparams(("arbitrary", "arbitrary")),
        name="ret_long",
    )(full, z3, z3, z3, z3, cos, sin, decay, lane_b(inner), lane_b(tail), gn.reshape(1, -1))
    return mix.reshape(batch * seq, vw), state


def _ret_short_kernel(full_ref, q_ref, k_ref, v_ref, g_ref, cos_ref, sin_ref, dec_ref, inn_ref, tail_ref,
                      gn_ref, s0_ref, mix_ref, snew_ref, *, nb, seq):
    def pad_rows(a):
        return jnp.concatenate([a, jnp.zeros((SHORT_ROWS - seq, a.shape[1]), F32)], axis=0)

    cos, sin = cos_ref[...], sin_ref[...]
    for b in range(nb):
        for h in range(RET_HEADS):
            qs = slice(h * RET_DK, (h + 1) * RET_DK)
            vs = slice(h * RET_DV, (h + 1) * RET_DV)
            q = pad_rows(_rotary(q_ref[b, :, qs].astype(F32), cos, sin)).astype(BF16)
            k = pad_rows(_rotary(k_ref[b, :, qs].astype(F32), cos, sin) * (RET_DK ** -0.5))
            v = pad_rows(v_ref[b, :, vs].astype(F32)).astype(BF16)
            s_old = s0_ref[b, h]
            scores = lax.dot_general(q, k.astype(BF16), NT_DIMS, preferred_element_type=F32) * dec_ref[h]
            o = jnp.dot(scores.astype(BF16), v, preferred_element_type=F32)
            o = o + jnp.dot(q, s_old.astype(BF16), preferred_element_type=F32) * inn_ref[h]
            snew_ref[b, h] = full_ref[h] * s_old + lax.dot_general((k * tail_ref[h]).astype(BF16), v, TN_DIMS,
                                                                   preferred_element_type=F32)
            mix_ref[b, :, vs] = _group_norm_gate(o[:seq], g_ref[b, :, vs].astype(F32), gn_ref[:, vs]).astype(BF16)


def _ret_short(z, batch, seq, state0, gn, pos, nb=4):
    z3 = z.reshape(batch, seq, z.shape[1])
    cos, sin, decay, inner, tail, full = _ret_tables(pos, seq)
    qw = RET_HEADS * RET_DK
    vw = RET_HEADS * RET_DV
    small = lambda a: _resident(a.shape)
    rp = SHORT_ROWS - seq
    decay = np.pad(decay, ((0, 0), (0, rp), (0, rp)))
    inner3 = np.pad(inner, ((0, 0), (0, rp)))[:, :, None]
    tail3 = np.pad(tail, ((0, 0), (0, rp)))[:, :, None]
    mix, state = pl.pallas_call(
        functools.partial(_ret_short_kernel, nb=nb, seq=seq),
        grid=(batch // nb,),
        in_specs=[
            pl.BlockSpec(memory_space=pltpu.SMEM),
            pl.BlockSpec((nb, seq, qw), lambda i: (i, 0, 0)),
            pl.BlockSpec((nb, seq, qw), lambda i: (i, 0, 1)),
            pl.BlockSpec((nb, seq, vw), lambda i: (i, 0, 1)),
            pl.BlockSpec((nb, seq, vw), lambda i: (i, 0, 2)),
            small(cos), small(sin), small(decay), small(inner3), small(tail3),
            _resident((1, vw)),
            pl.BlockSpec((nb, RET_HEADS, RET_DK, RET_DV), lambda i: (i, 0, 0, 0)),
        ],
        out_specs=[
            pl.BlockSpec((nb, seq, vw), lambda i: (i, 0, 0)),
            pl.BlockSpec((nb, RET_HEADS, RET_DK, RET_DV), lambda i: (i, 0, 0, 0)),
        ],
        out_shape=[jax.ShapeDtypeStruct((batch, seq, vw), BF16),
                   jax.ShapeDtypeStruct((batch, RET_HEADS, RET_DK, RET_DV), F32)],
        compiler_params=_cparams(("arbitrary",)),
        name="ret_short",
    )(full, z3, z3, z3, z3, cos, sin, decay, inner3, tail3, gn.reshape(1, -1), state0)
    return mix.reshape(batch * seq, vw), state


def _route(logits_t):
    n = logits_t.shape[1]
    epg = EXPERTS_PER_GROUP
    row = lax.broadcasted_iota(I32, (epg, n), 0).astype(F32)
    neg = jnp.float32(-1e30)
    big = jnp.float32(epg)
    is_g = row < N_GROUPS
    lg = jnp.where(is_g, logits_t[0:epg], neg)
    gmax = jnp.max(lg, axis=0, keepdims=True)
    gidx = jnp.min(jnp.where(lg == gmax, row, big), axis=0, keepdims=True)
    pg_sel = 1.0 / jnp.sum(jnp.where(is_g, jnp.exp(lg - gmax), 0.0), axis=0, keepdims=True)
    sel = logits_t[epg:2 * epg]
    for g in range(1, N_GROUPS):
        sel = jnp.where(gidx == g, logits_t[epg * (g + 1):epg * (g + 2)], sel)
    e = jnp.exp(sel - jnp.max(sel, axis=0, keepdims=True))
    pe = e / jnp.sum(e, axis=0, keepdims=True)
    t1 = jnp.max(pe, axis=0, keepdims=True)
    i1 = jnp.min(jnp.where(pe == t1, row, big), axis=0, keepdims=True)
    pe2 = jnp.where(row == i1, -1.0, pe)
    t2 = jnp.max(pe2, axis=0, keepdims=True)
    i2 = jnp.min(jnp.where(pe2 == t2, row, big), axis=0, keepdims=True)
    den = t1 + t2
    gates = jnp.where(row == i1, t1 / den * pg_sel, 0.0) + jnp.where(row == i2, t2 / den * pg_sel, 0.0)
    ids = jnp.where(row == 0, gidx, 0.0)
    block = jnp.concatenate([gates, ids, jnp.zeros((LANES - 2 * epg, n), F32)], axis=0)
    return block.T


def _outproj_pair_kernel(xl_ref, mixl_ref, xal_ref, xs_ref, mixs_ref, xas_ref, wa_ref, wb_ref, g_ref, wr_ref,
                         o_ref, r_ref, *, sub, n_long):
    i = pl.program_id(0)

    @pl.when(i < n_long)
    def _():
        _outproj_kernel(xl_ref, mixl_ref, xal_ref, wa_ref, wb_ref, g_ref, wr_ref, o_ref, r_ref, sub=sub)

    @pl.when(i >= n_long)
    def _():
        _outproj_kernel(xs_ref, mixs_ref, xas_ref, wa_ref, wb_ref, g_ref, wr_ref, o_ref, r_ref, sub=sub)


def _outproj_kernel(x_ref, mix_ref, xa_ref, wa_ref, wb_ref, g_ref, wr_ref, o_ref, r_ref, *, sub):
    acc = jnp.dot(mix_ref[...], wa_ref[...], preferred_element_type=F32)
    acc = acc + jnp.dot(xa_ref[...], wb_ref[...], preferred_element_type=F32)
    o_ref[:, :D_MODEL] = x_ref[...] + acc
    logits = []
    for s in range(x_ref.shape[0] // sub):
        rows = slice(s * sub, (s + 1) * sub)
        h2 = _rms(o_ref[rows, :D_MODEL], g_ref[...])
        hi = h2.astype(BF16)
        lo = (h2 - hi.astype(F32)).astype(BF16)
        p_hi = lax.dot_general(wr_ref[...], hi, NT_DIMS, preferred_element_type=F32)
        p_lo = lax.dot_general(wr_ref[0:ROUTER_ROWS, :], lo, NT_DIMS, preferred_element_type=F32)
        logits.append((p_hi[:ROUTER_ROWS] + p_lo) + p_hi[ROUTER_ROWS:])
    route = _route(jnp.concatenate(logits, axis=1))
    o_ref[:, D_MODEL:] = route
    r_ref[...] = route


def _router_weights(w_group, w_router):
    epg = EXPERTS_PER_GROUP
    zeros = lambda r: jnp.zeros((DEPTH, r, D_MODEL), F32)
    wt = jnp.concatenate([jnp.swapaxes(w_group, 1, 2), zeros(epg - N_GROUPS), jnp.swapaxes(w_router, 1, 2),
                          zeros(ROUTER_ROWS - epg - N_EXPERTS)], axis=1)
    hi = wt.astype(BF16)
    return jnp.concatenate([hi, (wt - hi.astype(F32)).astype(BF16)], axis=1)


def _outproj_router(x, x_off, mix, xa, w_out, g_ffn, w_router2):
    m, cm = mix.shape
    tm = min(ROW_TILE, m)
    xb = x_off // tm
    row = lambda w, off=0: pl.BlockSpec((tm, w), lambda i: (i + off, 0))
    return pl.pallas_call(
        functools.partial(_outproj_kernel, sub=min(ROUTER_SUB, tm)),
        grid=(m // tm,),
        in_specs=[row(D_MODEL, xb), row(cm), row(D_MODEL), _resident((cm, D_MODEL)), _resident((D_MODEL, D_MODEL)),
                  _resident((1, D_MODEL)), _resident((2 * ROUTER_ROWS, D_MODEL))],
        out_specs=[row(AUG_W), row(LANES)],
        out_shape=[jax.ShapeDtypeStruct((m, AUG_W), F32), jax.ShapeDtypeStruct((m, LANES), F32)],
        compiler_params=_cparams(("arbitrary",)),
        name="outproj_router",
    )(x, mix, xa, w_out[:cm], w_out[cm:], g_ffn.reshape(1, D_MODEL), w_router2)


def _outproj_router_pair(long, short, w_out, g_ffn, w_router2):
    tm = ROW_TILE
    assert long[0].shape[0] % tm == 0 and short[0].shape[0] % tm == 0
    cm = long[1].shape[1]
    nl, ns = long[0].shape[0] // tm, short[0].shape[0] // tm
    first = lambda w: pl.BlockSpec((tm, w), lambda i: (jnp.minimum(i, nl - 1), 0))
    second = lambda w: pl.BlockSpec((tm, w), lambda i: (jnp.maximum(i - nl, 0), 0))
    row = lambda w: pl.BlockSpec((tm, w), lambda i: (i, 0))
    m = (nl + ns) * tm
    return pl.pallas_call(
        functools.partial(_outproj_pair_kernel, sub=ROUTER_SUB, n_long=nl),
        grid=(nl + ns,),
        in_specs=[first(D_MODEL), first(cm), first(D_MODEL), second(D_MODEL), second(cm), second(D_MODEL),
                  _resident((cm, D_MODEL)), _resident((D_MODEL, D_MODEL)),
                  _resident((1, D_MODEL)), _resident((2 * ROUTER_ROWS, D_MODEL))],
        out_specs=[row(AUG_W), row(LANES)],
        out_shape=[jax.ShapeDtypeStruct((m, AUG_W), F32), jax.ShapeDtypeStruct((m, LANES), F32)],
        compiler_params=_cparams(("arbitrary",)),
        name="outproj_router_pair",
    )(*long, *short, w_out[:cm], w_out[cm:], g_ffn.reshape(1, D_MODEL), w_router2)


STEP_VALID, STEP_FIRST, STEP_LAST, STEP_NEWGROUP = 1, 2, 4, 8


def _moe_plan(gid, tm):
    m = gid.shape[0]
    ntile = m // tm
    nstep = ntile + N_GROUPS - 1
    counts = jnp.sum((gid[:, None] == jnp.arange(N_GROUPS, dtype=I32)[None, :]).astype(I32), axis=0)
    gend = jnp.cumsum(counts)
    gstart = gend - counts
    row_tok = jnp.argsort(gid, stable=True).astype(I32)
    inner_ok = (counts[1:] > 0) & (gstart[1:] % tm != 0)
    cuts = jnp.concatenate([jnp.arange(ntile, dtype=I32) * tm, jnp.where(inner_ok, gstart[1:], m)])
    cut_ok = jnp.concatenate([jnp.ones((ntile,), bool), inner_ok])
    rank = jnp.sum(((cuts[None, :] < cuts[:, None]) & cut_ok[None, :]).astype(I32), axis=1)
    sel = jnp.where(cut_ok, rank, nstep)[None, :] == jnp.arange(nstep, dtype=I32)[:, None]
    valid = jnp.any(sel, axis=1)
    start = jnp.sum(jnp.where(sel, cuts[None, :], 0), axis=1)
    start = jnp.where(valid, start, jnp.max(start))
    st_tile = start // tm
    st_gid = jnp.sum((gend[None, :] <= start[:, None]).astype(I32), axis=1)
    first = valid & (start % tm == 0)
    next_first = jnp.concatenate([first[1:], jnp.ones((1,), bool)])
    next_valid = jnp.concatenate([valid[1:], jnp.zeros((1,), bool)])
    last = valid & (next_first | ~next_valid)
    prev_gid = jnp.concatenate([jnp.full((1,), -1, I32), st_gid[:-1]])
    newgroup = valid & (st_gid != prev_gid)
    flags = valid * STEP_VALID + first * STEP_FIRST + last * STEP_LAST + newgroup * STEP_NEWGROUP
    return st_tile, st_gid, flags.astype(I32), row_tok


def _moe_kernel(tile_ref, gid_ref, flag_ref, tok_ref, xaug_hbm, g_ref, fin_ref, wgu_hbm, wd_hbm, out_hbm,
                xbuf, ybuf, wgu_bf, wd_bf, wgu_stage, wd_stage, gsem, ssem, wsem, *, tm, ntile, layer, final_norm):
    s = pl.program_id(0)
    tile = tile_ref[s]
    grp = gid_ref[s]
    flags = flag_ref[s]
    valid = (flags & STEP_VALID) > 0
    first = (flags & STEP_FIRST) > 0
    last = (flags & STEP_LAST) > 0
    newgroup = (flags & STEP_NEWGROUP) > 0
    slot = tile % 2

    def weight_copies(j):
        e = grp * EXPERTS_PER_GROUP + j
        return (pltpu.make_async_copy(wgu_hbm.at[layer, e], wgu_stage.at[j % 2], wsem.at[j % 2]),
                pltpu.make_async_copy(wd_hbm.at[layer, e], wd_stage.at[j % 2], wsem.at[j % 2]))

    def gather_start(t, sl):
        for r in range(tm):
            tok = tok_ref[t * tm + r]
            pltpu.make_async_copy(xaug_hbm.at[pl.ds(tok, 1)], xbuf.at[sl, pl.ds(r, 1)],
                                  gsem.at[sl]).start(priority=r % 2)

    def scatter_start(t, sl):
        for r in range(tm):
            tok = tok_ref[t * tm + r]
            pltpu.make_async_copy(ybuf.at[sl, pl.ds(r, 1)], out_hbm.at[pl.ds(tok, 1)],
                                  ssem.at[sl]).start(priority=r % 2)

    def gather_wait(sl):
        pltpu.make_async_copy(xaug_hbm.at[pl.ds(0, tm)], xbuf.at[sl], gsem.at[sl]).wait()

    def scatter_wait(sl):
        pltpu.make_async_copy(ybuf.at[sl], out_hbm.at[pl.ds(0, tm)], ssem.at[sl]).wait()

    @pl.when(s == 0)
    def _():
        gather_start(0, 0)

    def per_slot(sl, fn):
        for k in range(2):
            pl.when(sl == k)(functools.partial(fn, k))

    @pl.when(first & (tile + 1 < ntile))
    def _():
        per_slot(1 - slot, lambda k: gather_start(tile + 1, k))

    @pl.when(first)
    def _():
        gather_wait(slot)

        @pl.when(tile >= 2)
        def _():
            scatter_wait(slot)

        ybuf[slot] = xbuf[slot, :, 0:D_MODEL]

    def experts(stream_weights):
        xn = xbuf[slot, :, 0:D_MODEL]
        gl = xbuf[slot, :, GATE_LANE0:AUG_W]
        mine = gl[:, GID_LANE:GID_LANE + 1] == grp.astype(F32)
        h2 = _rms(xn, g_ref[...]).astype(BF16)
        acc = jnp.zeros((tm, D_MODEL), F32)
        if stream_weights:
            for c in weight_copies(0):
                c.start()
        for j in range(EXPERTS_PER_GROUP):
            if stream_weights:
                if j + 1 < EXPERTS_PER_GROUP:
                    for c in weight_copies(j + 1):
                        c.start()
                for c in weight_copies(j):
                    c.wait()
                wgu_bf[j] = wgu_stage[j % 2].astype(BF16)
                wd_bf[j] = wd_stage[j % 2].astype(BF16)
            gu = jnp.dot(h2, wgu_bf[j], preferred_element_type=F32)
            gate = jnp.where(mine, gl[:, j:j + 1], 0.0)
            act = _silu(gu[:, :EXPERT_FF]) * gu[:, EXPERT_FF:] * gate
            acc = acc + jnp.dot(act.astype(BF16), wd_bf[j], preferred_element_type=F32)
        ybuf[slot] = ybuf[slot] + acc

    @pl.when(newgroup)
    def _():
        experts(True)

    @pl.when(valid & jnp.logical_not(newgroup))
    def _():
        experts(False)

    @pl.when(last)
    def _():
        if final_norm:
            ybuf[slot] = _rms(ybuf[slot], fin_ref[...])
        per_slot(slot, lambda k: scatter_start(tile, k))

        @pl.when(tile == ntile - 1)
        def _():
            if ntile >= 2:
                scatter_wait(1 - slot)
            scatter_wait(slot)


def _moe(xaug, route, g_ffn, final_g, w_gu, w_down, layer, final_norm):
    m = xaug.shape[0]
    tm = min(MOE_TM, m)
    gid = route[:, GID_LANE].astype(I32)
    st_tile, st_gid, flags, row_tok = _moe_plan(gid, tm)
    epg = EXPERTS_PER_GROUP
    grid_spec = pltpu.PrefetchScalarGridSpec(
        num_scalar_prefetch=4,
        grid=(st_tile.shape[0],),
        in_specs=[
            pl.BlockSpec(memory_space=pl.ANY),
            pl.BlockSpec((1, D_MODEL), lambda s, *_: (0, 0)),
            pl.BlockSpec((1, D_MODEL), lambda s, *_: (0, 0)),
            pl.BlockSpec(memory_space=pl.ANY),
            pl.BlockSpec(memory_space=pl.ANY),
        ],
        out_specs=pl.BlockSpec(memory_space=pl.ANY),
        scratch_shapes=[
            pltpu.VMEM((2, tm, AUG_W), F32),
            pltpu.VMEM((2, tm, D_MODEL), F32),
            pltpu.VMEM((epg, D_MODEL, 2 * EXPERT_FF), BF16),
            pltpu.VMEM((epg, EXPERT_FF, D_MODEL), BF16),
            pltpu.VMEM((2, D_MODEL, 2 * EXPERT_FF), F32),
            pltpu.VMEM((2, EXPERT_FF, D_MODEL), F32),
            pltpu.SemaphoreType.DMA((2,)),
            pltpu.SemaphoreType.DMA((2,)),
            pltpu.SemaphoreType.DMA((2,)),
        ],
    )
    return pl.pallas_call(
        functools.partial(_moe_kernel, tm=tm, ntile=m // tm, layer=layer, final_norm=final_norm),
        grid_spec=grid_spec,
        out_shape=jax.ShapeDtypeStruct((m, D_MODEL), F32),
        compiler_params=_cparams(("arbitrary",)),
        name="moe",
    )(st_tile, st_gid, flags, row_tok, xaug, g_ffn.reshape(1, D_MODEL), final_g.reshape(1, D_MODEL), w_gu, w_down)


def _forward(x_long, x_short, pos_long, pos_short, conv_state, ret_state, mem_long, mem_short, wts):
    assert DEPTH == 2
    (bl, tl), (bs, ts) = x_long.shape[:2], x_short.shape[:2]
    ml, ms = bl * tl, bs * ts
    xl, xs = x_long.reshape(ml, D_MODEL), x_short.reshape(ms, D_MODEL)
    tq = min(XATTN_ROWS, tl)
    moe = lambda rows, route, i: _moe(rows, route, wts["norm_ffn"][i], wts["final_norm"], wts["moe_w_gate_up"],
                                      wts["moe_w_down"], i, final_norm=(i == DEPTH - 1))

    mix_l, q_l, conv_l = _inproj_conv(xl, wts["norm_mix"][0], wts["conv_w_in"], wts["conv_w"], bl, tl)
    zs = _inproj(xs, wts["norm_mix"][0], wts["conv_w_in"])
    mix_s, conv_s = _conv_short(zs, wts["conv_w"], conv_state, bs, ts)
    xa_l = _xattn(q_l, bl, tl, 0, *mem_long, 0, 1, tq)
    xa_s = _xattn(zs, bs, ts, 3, *mem_short, 0, SHORT_NB, ts)
    routed = _outproj_router_pair((xl, mix_l, xa_l), (xs, mix_s, xa_s), wts["conv_w_out"], wts["norm_ffn"][0],
                                  wts["w_router"][0])
    x1 = moe(*routed, 0)

    z = _inproj(x1, wts["norm_mix"][1], wts["ret_w_in"])
    zs = z[ml:]
    mix_l, ret_l = _ret_long(z, bl, tl, wts["ret_gn"], pos_long)
    mix_s, ret_s = _ret_short(zs, bs, ts, ret_state, wts["ret_gn"], pos_short)
    xa_l = _xattn(z, bl, tl, 6, *mem_long, 1, 1, tq)
    xa_s = _xattn(zs, bs, ts, 6, *mem_short, 1, SHORT_NB, ts)
    y_l = moe(*_outproj_router(x1, 0, mix_l, xa_l, wts["ret_w_out"], wts["norm_ffn"][1], wts["w_router"][1]), 1)
    y_s = moe(*_outproj_router(x1, ml, mix_s, xa_s, wts["ret_w_out"], wts["norm_ffn"][1], wts["w_router"][1]), 1)
    return (y_l.reshape(bl, tl, D_MODEL), y_s.reshape(bs, ts, D_MODEL), conv_l, conv_s, ret_l, ret_s)


def kernel(x_prompt, x_sample, state_conv, state_ret, cache_mem_k, cache_mem_v, mem_prompt, norm_mix, norm_ffn,
           mem_norm, w_mem_kv, conv_w_in, conv_w, conv_w_out, ret_w_in, ret_gn, ret_w_out, moe_w_group,
           moe_w_router, moe_w_gate_up, moe_w_down, final_norm):
    batch, seq = x_prompt.shape[0], x_prompt.shape[1]
    dec_seq = x_sample.shape[1]
    w_router = _router_weights(moe_w_group, moe_w_router)
    wts = dict(
        norm_mix=norm_mix, norm_ffn=norm_ffn, final_norm=final_norm, w_router=w_router,
        conv_w_in=conv_w_in[0].astype(BF16), conv_w=conv_w[0], conv_w_out=conv_w_out[0].astype(BF16),
        ret_w_in=ret_w_in[0].astype(BF16), ret_gn=ret_gn[0], ret_w_out=ret_w_out[0].astype(BF16),
        moe_w_gate_up=moe_w_gate_up, moe_w_down=moe_w_down,
    )
    mem_k_prompt, mem_v_prompt = _mem_kv(mem_prompt.reshape(batch * N_MEM, D_MODEL), mem_norm, w_mem_kv.astype(BF16))

    y_prompt, y_sample, conv_p, conv_s, ret_p, ret_s = _forward(
        x_prompt, x_sample, np.arange(seq), PAST_LEN + np.arange(dec_seq),
        state_conv[0], state_ret[0], (mem_k_prompt, mem_v_prompt), (cache_mem_k, cache_mem_v), wts)
    return (y_prompt, y_sample, conv_p[None], conv_s[None], ret_p[None], ret_s[None], mem_k_prompt, mem_v_prompt)
```
